```python
import jax
import jax.numpy as jnp
from jax import lax
import numpy as np

D_MODEL = 2048
BATCH = 4
SEQ = 4096
DEPTH = 4

CTX_LEN = 256
GRID_W = 64
NORM_EPS = 1e-6
N_MOD = 6
F32 = jnp.float32

LRU_WIDTH = 512
LRU_BLOCKS = 4
LRU_BLOCK = LRU_WIDTH // LRU_BLOCKS
LRU_C = 8.0
CONV_W = 4

GDN_HEADS = 4
GDN_DK = 128
GDN_DV = 128
GDN_CHUNK = 64
GDN_QKV = GDN_HEADS * (2 * GDN_DK + GDN_DV)

FFT_GROUPS = 4
FFT_GROUP = 128
FFT_WIDTH = FFT_GROUPS * FFT_GROUP

MLA_HEADS = 4
MLA_Q_RANK = 512
MLA_KV_RANK = 256
MLA_NOPE = 128
MLA_ROPE = 64
MLA_V = 128
ROPE_BASE = 10000.0
Q_BLOCK = 128

N_BRANCH = 4
BRANCH_WIDTH = 512

N_EXPERTS = 32
TOP_K = 4
D_EXPERT = 512
SWIGLU_LIMIT = 7.0
SWIGLU_ALPHA = 1.702
MOE_BLOCK = 128

IN_SIZES = (
    LRU_WIDTH,
    LRU_WIDTH,
    GDN_QKV,
    GDN_HEADS * GDN_DV,
    2 * GDN_HEADS,
    2 * GDN_HEADS,
    FFT_WIDTH,
    MLA_Q_RANK,
    MLA_KV_RANK,
    MLA_ROPE,
    N_BRANCH * D_MODEL,
)
D_IN = sum(IN_SIZES)

kernel_name = "hybrid_diffusion_trunk_lru_gdn_fnet_mla_moe"


def rmsnorm(x, g):
    xf = x.astype(F32)
    y = xf * lax.rsqrt(jnp.mean(xf * xf, axis=-1, keepdims=True) + NORM_EPS)
    return (y * g.astype(F32)).astype(x.dtype)


def modulate(h, shift, scale):
    return h * (1.0 + scale) + shift


def l2norm(x):
    return x * lax.rsqrt(jnp.sum(x * x, axis=-1, keepdims=True) + NORM_EPS)


def split_cols(z):
    parts, start = [], 0
    for size in IN_SIZES:
        parts.append(z[..., start:start + size])
        start += size
    return parts


def centred_dwconv(x, w, b):
    k = w.shape[0]
    left = k // 2
    y = lax.conv_general_dilated(
        x, w[:, None, :].astype(x.dtype), window_strides=(1,),
        padding=[(left, k - 1 - left)], dimension_numbers=('NWC', 'WIO', 'NWC'),
        feature_group_count=x.shape[-1])
    return y + b.astype(x.dtype)


def linear_scan(a, b, h0, reverse):
    first, last = (-1, 0) if reverse else (0, -1)
    b = b.at[:, first].add(a[:, first] * h0)

    def combine(e1, e2):
        return e1[0] * e2[0], e2[0] * e1[1] + e2[1]

    _, h = lax.associative_scan(combine, (a, b), reverse=reverse, axis=1)
    return h, h[:, last]


def blockdiag(x, w):
    bn, t, _ = x.shape
    g, bi, bo = w.shape
    return jnp.einsum('btgi,gio->btgo', x.reshape(bn, t, g, bi), w).reshape(bn, t, g * bo)


def rglru_coeffs(x, w_a, b_a, w_x, b_x, lam):
    r = jax.nn.sigmoid(blockdiag(x, w_a) + b_a)
    i = jax.nn.sigmoid(blockdiag(x, w_x) + b_x)
    log_a = -LRU_C * r * jax.nn.softplus(-lam)
    return jnp.exp(log_a), jnp.sqrt(-jnp.expm1(2.0 * log_a)) * (i * x)


def rglru_mixer(u_ctx, u_lat, conv_w, conv_b, w_a, b_a, w_x, b_x, lam):
    x_ctx = centred_dwconv(u_ctx, conv_w, conv_b).astype(F32)
    x_lat = centred_dwconv(u_lat, conv_w, conv_b).astype(F32)
    w_a, b_a, w_x, b_x, lam = (t.astype(F32) for t in (w_a, b_a, w_x, b_x, lam))
    h0 = jnp.zeros((x_ctx.shape[0], LRU_WIDTH), F32)
    y_ctx = jnp.zeros_like(x_ctx)
    y_lat = jnp.zeros_like(x_lat)
    for d, rev in enumerate((False, True)):
        a_c, b_c = rglru_coeffs(x_ctx, w_a[d], b_a[d], w_x[d], b_x[d], lam[d])
        h_c, s_c = linear_scan(a_c, b_c, h0, rev)
        a_l, b_l = rglru_coeffs(x_lat, w_a[d], b_a[d], w_x[d], b_x[d], lam[d])
        h_l, _ = linear_scan(a_l, b_l, s_c, rev)
        y_ctx = y_ctx + h_c
        y_lat = y_lat + h_l
    return y_ctx, y_lat


def gdn_prepare(qkv, beta_logit, alpha_logit, conv_w, conv_b, a_log, dt_bias):
    bn, t, _ = qkv.shape
    x = jax.nn.silu(centred_dwconv(qkv, conv_w, conv_b)).astype(F32)
    nq = GDN_HEADS * GDN_DK
    q = l2norm(x[..., :nq].reshape(bn, t, GDN_HEADS, GDN_DK)) * GDN_DK ** -0.5
    k = l2norm(x[..., nq:2 * nq].reshape(bn, t, GDN_HEADS, GDN_DK))
    v = x[..., 2 * nq:].reshape(bn, t, GDN_HEADS, GDN_DV)
    beta = jax.nn.sigmoid(beta_logit.astype(F32)).reshape(bn, t, 2, GDN_HEADS)
    g = -jnp.exp(a_log.astype(F32)) * jax.nn.softplus(
        alpha_logit.astype(F32).reshape(bn, t, 2, GDN_HEADS) + dt_bias.astype(F32))
    to_bhtd = lambda z: jnp.transpose(z, (0, 2, 1, 3))
    to_dbht = lambda z: jnp.transpose(z, (2, 0, 3, 1))
    return to_bhtd(q), to_bhtd(k), to_bhtd(v), to_dbht(beta), to_dbht(g)


def gdn_chunked(q, k, v, beta, g, s0):
    bn, nh, t, dk = q.shape
    dv = v.shape[-1]
    nc, cs = t // GDN_CHUNK, GDN_CHUNK
    q = q.reshape(bn, nh, nc, cs, dk)
    k = k.reshape(bn, nh, nc, cs, dk)
    v = v.reshape(bn, nh, nc, cs, dv)
    beta = beta.reshape(bn, nh, nc, cs)
    gam = jnp.cumsum(g.reshape(bn, nh, nc, cs), axis=-1)
    diff = gam[..., :, None] - gam[..., None, :]
    idx = jnp.arange(cs)
    incl = idx[:, None] >= idx[None, :]
    strict = idx[:, None] > idx[None, :]
    dec_incl = jnp.exp(jnp.where(incl, diff, -jnp.inf))
    dec_strict = jnp.where(strict, dec_incl, 0.0)
    kb = k * beta[..., None]
    m = jnp.einsum('bhncd,bhnjd->bhncj', kb, k) * dec_strict + jnp.eye(cs, dtype=F32)
    rhs = jnp.concatenate([v * beta[..., None], kb * jnp.exp(gam)[..., None]], axis=-1)
    sol = lax.linalg.triangular_solve(m, rhs, left_side=True, lower=True, unit_diagonal=True)
    u, w = sol[..., :dv], sol[..., dv:]
    qk = jnp.einsum('bhncd,bhnjd->bhncj', q, k) * dec_incl
    q_dec = q * jnp.exp(gam)[..., None]
    k_dec = k * jnp.exp(gam[..., -1:] - gam)[..., None]
    g_tot = jnp.exp(gam[..., -1])
    xs = tuple(jnp.moveaxis(z, 2, 0) for z in (u, w, qk, q_dec, k_dec, g_tot))

    def step(s, inp):
        u_n, w_n, qk_n, qd_n, kd_n, gt_n = inp
        v_new = u_n - jnp.einsum('bhcd,bhde->bhce', w_n, s)
        o_n = jnp.einsum('bhcd,bhde->bhce', qd_n, s) + jnp.einsum('bhcj,bhje->bhce', qk_n, v_new)
        s = s * gt_n[..., None, None] + jnp.einsum('bhcd,bhce->bhde', kd_n, v_new)
        return s, o_n

    s_fin, o = lax.scan(step, s0, xs)
    return jnp.moveaxis(o, 0, 2).reshape(bn, nh, t, dv), s_fin


def gdn_mixer(qkv_ctx, beta_ctx, alpha_ctx, qkv_lat, beta_lat, alpha_lat, conv_w, conv_b, a_log, dt_bias):
    qc, kc, vc, bc, gc = gdn_prepare(qkv_ctx, beta_ctx, alpha_ctx, conv_w, conv_b, a_log, dt_bias)
    ql, kl, vl, bl, gl = gdn_prepare(qkv_lat, beta_lat, alpha_lat, conv_w, conv_b, a_log, dt_bias)
    s0 = jnp.zeros((qc.shape[0], GDN_HEADS, GDN_DK, GDN_DV), F32)
    o_ctx = jnp.zeros_like(vc)
    o_lat = jnp.zeros_like(vl)
    for d, rev in enumerate((False, True)):
        fl = (lambda z: jnp.flip(z, axis=2)) if rev else (lambda z: z)
        oc, sc = gdn_chunked(fl(qc), fl(kc), fl(vc), fl(bc[d]), fl(gc[d]), s0)
        ol, _ = gdn_chunked(fl(ql), fl(kl), fl(vl), fl(bl[d]), fl(gl[d]), sc)
        o_ctx = o_ctx + fl(oc)
        o_lat = o_lat + fl(ol)
    return o_ctx, o_lat


def gdn_output(o, z, g):
    bn, nh, t, dv = o.shape
    o = jnp.transpose(o, (0, 2, 1, 3))
    o = o * lax.rsqrt(jnp.mean(o * o, axis=-1, keepdims=True) + NORM_EPS) * g.astype(F32)
    o = o * jax.nn.silu(z.astype(F32)).reshape(bn, t, nh, dv)
    return o.reshape(bn, t, nh * dv).astype(z.dtype)


def fourier_mixer(u):
    bn, t, _ = u.shape
    uf = jnp.transpose(u.astype(F32).reshape(bn, t, FFT_GROUPS, FFT_GROUP), (0, 2, 1, 3))
    y = jnp.fft.fft2(uf, axes=(-2, -1), norm='ortho').real
    return jnp.transpose(y, (0, 2, 1, 3)).reshape(bn, t, FFT_WIDTH).astype(u.dtype)


def rope1d(x, pos):
    half = x.shape[-1] // 2
    inv = jnp.power(ROPE_BASE, -jnp.arange(half, dtype=F32) / half)
    ang = pos.astype(F32)[:, None] * inv
    cos = jnp.cos(ang)[:, None, :].astype(x.dtype)
    sin = jnp.sin(ang)[:, None, :].astype(x.dtype)
    x1, x2 = x[..., :half], x[..., half:]
    return jnp.concatenate([x1 * cos - x2 * sin, x2 * cos + x1 * sin], axis=-1)


def rope2d(x, row, col):
    half = x.shape[-1] // 2
    return jnp.concatenate([rope1d(x[..., :half], row), rope1d(x[..., half:], col)], axis=-1)


def mla_heads(cq, ckv, q_norm_g, kv_norm_g, w_uq, w_ukv):
    bn, t, _ = cq.shape
    q = (rmsnorm(cq, q_norm_g) @ w_uq).reshape(bn, t, MLA_HEADS, MLA_NOPE + MLA_ROPE)
    kv = (rmsnorm(ckv, kv_norm_g) @ w_ukv).reshape(bn, t, MLA_HEADS, MLA_NOPE + MLA_V)
    return q[..., :MLA_NOPE], q[..., MLA_NOPE:], kv[..., :MLA_NOPE], kv[..., MLA_NOPE:]


def mla_assemble(q_nope, q_rope, k_nope, k_rope, v):
    q = jnp.concatenate([q_nope, q_rope], axis=-1)
    k = jnp.concatenate([k_nope, jnp.broadcast_to(k_rope, k_nope.shape[:-1] + (MLA_ROPE,))], axis=-1)
    to_bhtd = lambda z: jnp.transpose(z, (0, 2, 1, 3))
    return to_bhtd(q), to_bhtd(k), to_bhtd(v)


def block_attention(q, k, v):
    bn, nh, tq, dq = q.shape
    nb = tq // Q_BLOCK
    qb = jnp.moveaxis(q.reshape(bn, nh, nb, Q_BLOCK, dq), 2, 0)
    scale = dq ** -0.5

    def one_block(qi):
        s = jnp.einsum('bhqd,bhkd->bhqk', qi, k).astype(F32) * scale
        p = jax.nn.softmax(s, axis=-1).astype(v.dtype)
        return jnp.einsum('bhqk,bhkd->bhqd', p, v)

    o = lax.map(one_block, qb)
    return jnp.moveaxis(o, 0, 2).reshape(bn, nh, tq, v.shape[-1])


def mla_mixer(cq_ctx, ckv_ctx, kr_ctx, cq_lat, ckv_lat, kr_lat, row, col, q_norm_g, kv_norm_g, w_uq, w_ukv):
    qn_c, qr_c, kn_c, v_c = mla_heads(cq_ctx, ckv_ctx, q_norm_g, kv_norm_g, w_uq, w_ukv)
    qn_l, qr_l, kn_l, v_l = mla_heads(cq_lat, ckv_lat, q_norm_g, kv_norm_g, w_uq, w_ukv)
    qr_l = rope2d(qr_l, row, col)
    kr_l = rope2d(kr_lat[:, :, None, :], row, col)
    q_c, k_c, v_c = mla_assemble(qn_c, qr_c, kn_c, kr_ctx[:, :, None, :], v_c)
    q_l, k_l, v_l = mla_assemble(qn_l, qr_l, kn_l, kr_l, v_l)
    o_c = block_attention(q_c, k_c, v_c)
    o_l = block_attention(q_l, jnp.concatenate([k_l, k_c], axis=2), jnp.concatenate([v_l, v_c], axis=2))
    to_btd = lambda o: jnp.transpose(o, (0, 2, 1, 3)).reshape(o.shape[0], o.shape[2], -1)
    return to_btd(o_c), to_btd(o_l)


def merge_branches(branches, gate_logits, w_branch, w_out):
    bn, t, _ = gate_logits.shape
    gates = jax.nn.sigmoid(gate_logits.astype(F32)).astype(gate_logits.dtype).reshape(bn, t, N_BRANCH, D_MODEL)
    merged = jnp.zeros((bn, t, D_MODEL), gate_logits.dtype)
    for i, y in enumerate(branches):
        merged = merged + gates[:, :, i] * (y @ w_branch[i])
    return merged @ w_out


def token_mixer(hn_ctx, hn_lat, row, col, w_in, lru_conv_w, lru_conv_b, lru_w_a, lru_b_a, lru_w_x, lru_b_x,
                lru_lambda, gdn_conv_w, gdn_conv_b, gdn_a_log, gdn_dt_bias, gdn_norm_g, mla_q_norm_g,
                mla_kv_norm_g, mla_w_uq, mla_w_ukv, w_branch, w_out):
    (lu_c, lg_c, qkv_c, z_c, be_c, al_c, fu_c, cq_c, ckv_c, kr_c, mg_c) = split_cols(hn_ctx @ w_in)
    (lu_l, lg_l, qkv_l, z_l, be_l, al_l, fu_l, cq_l, ckv_l, kr_l, mg_l) = split_cols(hn_lat @ w_in)
    ra_c, ra_l = rglru_mixer(lu_c, lu_l, lru_conv_w, lru_conv_b, lru_w_a, lru_b_a, lru_w_x, lru_b_x, lru_lambda)
    ya_c = jax.nn.gelu(lg_c) * ra_c.astype(lg_c.dtype)
    ya_l = jax.nn.gelu(lg_l) * ra_l.astype(lg_l.dtype)
    ob_c, ob_l = gdn_mixer(qkv_c, be_c, al_c, qkv_l, be_l, al_l, gdn_conv_w, gdn_conv_b, gdn_a_log, gdn_dt_bias)
    yb_c = gdn_output(ob_c, z_c, gdn_norm_g)
    yb_l = gdn_output(ob_l, z_l, gdn_norm_g)
    yc_c = fourier_mixer(fu_c)
    yc_l = fourier_mixer(fu_l)
    yd_c, yd_l = mla_mixer(cq_c, ckv_c, kr_c, cq_l, ckv_l, kr_l, row, col,
                           mla_q_norm_g, mla_kv_norm_g, mla_w_uq, mla_w_ukv)
    m_ctx = merge_branches((ya_c, yb_c, yc_c, yd_c), mg_c, w_branch, w_out)
    m_lat = merge_branches((ya_l, yb_l, yc_l, yd_l), mg_l, w_branch, w_out)
    return m_ctx, m_lat


def moe_ffn(xt, router_w, router_b, w1, b1, w2, b2):
    n, d = xt.shape
    logits = xt.astype(F32) @ router_w.astype(F32) + router_b.astype(F32)
    top_val, top_idx = lax.top_k(logits, TOP_K)
    gate = jax.nn.softmax(top_val, axis=-1)
    nk = n * TOP_K
    flat_e = top_idx.reshape(nk)
    flat_tok = jnp.repeat(jnp.arange(n, dtype=jnp.int32), TOP_K)
    order = jnp.argsort(flat_e)
    se, stok, sgate = flat_e[order], flat_tok[order], gate.reshape(nk)[order]
    counts = jnp.bincount(flat_e, length=N_EXPERTS)
    padded = (counts + MOE_BLOCK - 1) // MOE_BLOCK * MOE_BLOCK
    start = jnp.cumsum(counts) - counts
    pend = jnp.cumsum(padded)
    pstart = pend - padded
    dest = pstart[se] + jnp.arange(nk, dtype=jnp.int32) - start[se]
    n_blocks = (nk + N_EXPERTS * (MOE_BLOCK - 1) + MOE_BLOCK - 1) // MOE_BLOCK
    p = n_blocks * MOE_BLOCK
    buf_tok = jnp.zeros((p,), jnp.int32).at[dest].set(stok)
    buf_gate = jnp.zeros((p,), F32).at[dest].set(sgate)
    block_e = jnp.minimum(jnp.searchsorted(pend, jnp.arange(n_blocks, dtype=jnp.int32) * MOE_BLOCK,
                                           side='right'), N_EXPERTS - 1)

    def expert_block(args):
        tok, e = args
        h = xt[tok] @ w1[e] + b1[e]
        h_glu = jnp.minimum(h[:, :D_EXPERT], SWIGLU_LIMIT)
        h_lin = jnp.clip(h[:, D_EXPERT:], -SWIGLU_LIMIT, SWIGLU_LIMIT)
        act = h_glu * jax.nn.sigmoid(SWIGLU_ALPHA * h_glu) * (h_lin + 1.0)
        return act @ w2[e] + b2[e]

    out = lax.map(expert_block, (buf_tok.reshape(n_blocks, MOE_BLOCK), block_e))
    out = out.reshape(p, d) * buf_gate[:, None].astype(xt.dtype)
    return jnp.zeros_like(xt).at[buf_tok].add(out)


def setup_inputs(seed: int = 0) -> dict:
    key = jax.random.key(seed)
    keys = list(jax.random.split(key, 40))

    def nrm(shape, scale):
        return jax.random.normal(keys.pop(), shape, jnp.float32) * scale

    def unif(shape, lo, hi):
        return jax.random.uniform(keys.pop(), shape, jnp.float32, lo, hi)

    a_tgt = unif((DEPTH, 2, LRU_WIDTH), 0.9, 0.999)
    p_lam = jnp.power(a_tgt, 1.0 / LRU_C)
    dt = jnp.exp(unif((DEPTH, 2, GDN_HEADS), float(np.log(1e-3)), float(np.log(1e-1))))
    return {
        'x': nrm((BATCH, SEQ, D_MODEL), 1.0),
        'c': nrm((BATCH, D_MODEL), 1.0),
        'ctx': nrm((BATCH, CTX_LEN, D_MODEL), 1.0),
        'c_ctx': nrm((D_MODEL,), 1.0),
        'w_ada': nrm((DEPTH, D_MODEL, N_MOD * D_MODEL), 0.5 * D_MODEL ** -0.5),
        'b_ada': nrm((DEPTH, N_MOD * D_MODEL), 0.01),
        'norm1_g': 1.0 + nrm((DEPTH, D_MODEL), 0.01),
        'norm2_g': 1.0 + nrm((DEPTH, D_MODEL), 0.01),
        'w_in': nrm((DEPTH, D_MODEL, D_IN), D_MODEL ** -0.5),
        'lru_conv_w': nrm((DEPTH, CONV_W, LRU_WIDTH), CONV_W ** -0.5),
        'lru_conv_b': nrm((DEPTH, LRU_WIDTH), 0.01),
        'lru_w_a': nrm((DEPTH, 2, LRU_BLOCKS, LRU_BLOCK, LRU_BLOCK), LRU_BLOCK ** -0.5),
        'lru_b_a': nrm((DEPTH, 2, LRU_WIDTH), 0.01),
        'lru_w_x': nrm((DEPTH, 2, LRU_BLOCKS, LRU_BLOCK, LRU_BLOCK), LRU_BLOCK ** -0.5),
        'lru_b_x': nrm((DEPTH, 2, LRU_WIDTH), 0.01),
        'lru_lambda': jnp.log(p_lam) - jnp.log1p(-p_lam),
        'gdn_conv_w': nrm((DEPTH, CONV_W, GDN_QKV), CONV_W ** -0.5),
        'gdn_conv_b': nrm((DEPTH, GDN_QKV), 0.01),
        'gdn_a_log': jnp.log(unif((DEPTH, 2, GDN_HEADS), 1.0, 16.0)),
        'gdn_dt_bias': dt + jnp.log(-jnp.expm1(-dt)),
        'gdn_norm_g': 1.0 + nrm((DEPTH, GDN_DV), 0.01),
        'mla_q_norm_g': 1.0 + nrm((DEPTH, MLA_Q_RANK), 0.01),
        'mla_kv_norm_g': 1.0 + nrm((DEPTH, MLA_KV_RANK), 0.01),
        'mla_w_uq': nrm((DEPTH, MLA_Q_RANK, MLA_HEADS * (MLA_NOPE + MLA_ROPE)), MLA_Q_RANK ** -0.5),
        'mla_w_ukv': nrm((DEPTH, MLA_KV_RANK, MLA_HEADS * (MLA_NOPE + MLA_V)), MLA_KV_RANK ** -0.5),
        'w_branch': nrm((DEPTH, N_BRANCH, BRANCH_WIDTH, D_MODEL), BRANCH_WIDTH ** -0.5),
        'w_out': nrm((DEPTH, D_MODEL, D_MODEL), D_MODEL ** -0.5),
        'router_w': nrm((DEPTH, D_MODEL, N_EXPERTS), D_MODEL ** -0.5),
        'router_b': nrm((DEPTH, N_EXPERTS), 0.01),
        'exp_w1': nrm((DEPTH, N_EXPERTS, D_MODEL, 2 * D_EXPERT), D_MODEL ** -0.5),
        'exp_b1': nrm((DEPTH, N_EXPERTS, 2 * D_EXPERT), 0.01),
        'exp_w2': nrm((DEPTH, N_EXPERTS, D_EXPERT, D_MODEL), D_EXPERT ** -0.5),
        'exp_b2': nrm((DEPTH, N_EXPERTS, D_MODEL), 0.01),
        'final_norm_g': 1.0 + nrm((D_MODEL,), 0.01),
    }


def reference(x, c, ctx, c_ctx, w_ada, b_ada, norm1_g, norm2_g, w_in, lru_conv_w, lru_conv_b, lru_w_a, lru_b_a,
              lru_w_x, lru_b_x, lru_lambda, gdn_conv_w, gdn_conv_b, gdn_a_log, gdn_dt_bias, gdn_norm_g,
              mla_q_norm_g, mla_kv_norm_g, mla_w_uq, mla_w_ukv, w_branch, w_out, router_w, router_b,
              exp_w1, exp_b1, exp_w2, exp_b2, final_norm_g):
    n_lat = x.shape[1]
    rows = n_lat // GRID_W
    row = jnp.broadcast_to(jnp.arange(rows)[:, None], (rows, GRID_W)).reshape(n_lat)
    col = jnp.broadcast_to(jnp.arange(GRID_W)[None, :], (rows, GRID_W)).reshape(n_lat)
    h_lat, h_ctx = x, ctx
    s_lat, s_ctx = jax.nn.silu(c), jax.nn.silu(c_ctx)
    for l in range(DEPTH):
        last = l == DEPTH - 1
        mod_lat = jnp.split((s_lat @ w_ada[l] + b_ada[l])[:, None, :], N_MOD, axis=-1)
        mod_ctx = jnp.split((s_ctx @ w_ada[l] + b_ada[l])[None, None, :], N_MOD, axis=-1)
        hn_lat = modulate(rmsnorm(h_lat, norm1_g[l]), mod_lat[0], mod_lat[1])
        hn_ctx = modulate(rmsnorm(h_ctx, norm1_g[l]), mod_ctx[0], mod_ctx[1])
        m_ctx, m_lat = token_mixer(
            hn_ctx, hn_lat, row, col, w_in[l], lru_conv_w[l], lru_conv_b[l], lru_w_a[l], lru_b_a[l],
            lru_w_x[l], lru_b_x[l], lru_lambda[l], gdn_conv_w[l], gdn_conv_b[l], gdn_a_log[l],
            gdn_dt_bias[l], gdn_norm_g[l], mla_q_norm_g[l], mla_kv_norm_g[l], mla_w_uq[l], mla_w_ukv[l],
            w_branch[l], w_out[l])
        h_lat = h_lat + mod_lat[2] * m_lat
        hn_lat = modulate(rmsnorm(h_lat, norm2_g[l]), mod_lat[3], mod_lat[4])
        bl, tl, dm = h_lat.shape
        if last:
            f_lat = moe_ffn(hn_lat.reshape(bl * tl, dm), router_w[l], router_b[l],
                            exp_w1[l], exp_b1[l], exp_w2[l], exp_b2[l])
        else:
            h_ctx = h_ctx + mod_ctx[2] * m_ctx
            hn_ctx = modulate(rmsnorm(h_ctx, norm2_g[l]), mod_ctx[3], mod_ctx[4])
            n_ctx_tok = h_ctx.shape[0] * h_ctx.shape[1]
            tokens = jnp.concatenate([hn_ctx.reshape(n_ctx_tok, dm), hn_lat.reshape(bl * tl, dm)], axis=0)
            f_all = moe_ffn(tokens, router_w[l], router_b[l], exp_w1[l], exp_b1[l], exp_w2[l], exp_b2[l])
            h_ctx = h_ctx + mod_ctx[5] * f_all[:n_ctx_tok].reshape(h_ctx.shape)
            f_lat = f_all[n_ctx_tok:]
        h_lat = h_lat + mod_lat[5] * f_lat.reshape(h_lat.shape)
    return rmsnorm(h_lat, final_norm_g)
```

```python
import functools
import math

import jax
import jax.numpy as jnp
import numpy as np
from jax import lax
from jax.experimental import pallas as pl
from jax.experimental.pallas import tpu as pltpu

F32 = jnp.float32
BF16 = jnp.bfloat16

GRID_W = 64
NORM_EPS = 1e-6
N_MOD = 6
LRU_WIDTH = 512
LRU_BLOCKS = 4
LRU_BLOCK = 128
LRU_C = 8.0
GDN_HEADS = 4
GDN_DK = 128
GDN_DV = 128
GDN_CHUNK = 64
GDN_QKV = GDN_HEADS * (2 * GDN_DK + GDN_DV)
FFT_GROUPS = 4
FFT_GROUP = 128
FFT_WIDTH = 512
MLA_HEADS = 4
MLA_Q_RANK = 512
MLA_KV_RANK = 256
MLA_NOPE = 128
MLA_ROPE = 64
MLA_V = 128
MLA_QK_PAD = 256
ROPE_BASE = 10000.0
N_BRANCH = 4
BRANCH_WIDTH = 512
N_EXPERTS = 32
TOP_K = 4
D_EXPERT = 512
SWIGLU_LIMIT = 7.0
SWIGLU_ALPHA = 1.702
MOE_ROWS = 256

COL_QKV = 0
COL_LU = 1536
COL_LG = 2048
COL_Z = 2560
COL_FU = 3072
COL_CQ = 3584
COL_CKV = 4096
COL_MISC = 4352
COL_BA = 4480
ZIN_W = 4608

VMEM_LIMIT = 56 * 1024 * 1024
HALO = 8


def _cp(sem, vmem=VMEM_LIMIT):
    return pltpu.CompilerParams(dimension_semantics=sem, vmem_limit_bytes=vmem)


def _pick(n, cands):
    for c in cands:
        if n % c == 0:
            return c
    raise ValueError(f"no tile for {n} in {cands}")


def _mm(a, b):
    return jnp.dot(a.astype(BF16), b.astype(BF16), preferred_element_type=F32)


def _mm_nt(a, b):
    return lax.dot_general(a.astype(BF16), b.astype(BF16), (((1,), (1,)), ((), ())), preferred_element_type=F32)


def _mm_tn(a, b):
    return lax.dot_general(a.astype(BF16), b.astype(BF16), (((0,), (0,)), ((), ())), preferred_element_type=F32)


def _mm_f32(a, b):
    return jnp.dot(a, b, preferred_element_type=F32, precision=lax.Precision.HIGHEST)


def _softplus(y):
    return jnp.maximum(y, 0.0) + jnp.log1p(jnp.exp(-jnp.abs(y)))


def _ada_kernel(c_ref, w_ref, b_ref, o_ref):
    cv = c_ref[...]
    s = cv * jax.nn.sigmoid(cv)
    o_ref[0] = _mm(s, w_ref[0]) + b_ref[0]


def ada_table(cmat, w_ada, b_ada):
    depth, d, n = w_ada.shape
    rows = cmat.shape[0]
    tn = _pick(n, (1024, 512, 256, 128))
    return pl.pallas_call(
        _ada_kernel,
        grid=(depth, n // tn),
        in_specs=[pl.BlockSpec((rows, d), lambda l, j: (0, 0)),
                  pl.BlockSpec((1, d, tn), lambda l, j: (l, 0, j)),
                  pl.BlockSpec((1, 1, tn), lambda l, j: (l, 0, j))],
        out_specs=pl.BlockSpec((1, rows, tn), lambda l, j: (l, 0, j)),
        out_shape=jax.ShapeDtypeStruct((depth, rows, n), F32),
        compiler_params=_cp(("arbitrary", "arbitrary")),
        name="ada_table",
    )(cmat, w_ada, b_ada.reshape(depth, 1, n))


def _norm_kernel(*refs, has_delta, has_mod, has_router, emit_h):
    it = iter(refs)
    h_ref = next(it)
    if has_delta:
        d_ref = next(it)
        gate_ref = next(it)
    g_ref = next(it)
    if has_mod:
        shift_ref = next(it)
        scale_ref = next(it)
    if has_router:
        rw_ref = next(it)
        rb_ref = next(it)
    if emit_h:
        ho_ref = next(it)
    y_ref = next(it)
    if has_router:
        lo_ref = next(it)

    h = h_ref[...]
    if has_delta:
        h = h + gate_ref[0] * d_ref[...]
    if emit_h:
        ho_ref[...] = h
    y = h * lax.rsqrt(jnp.mean(h * h, axis=-1, keepdims=True) + NORM_EPS) * g_ref[...]
    if has_mod:
        y = y * (1.0 + scale_ref[0]) + shift_ref[0]
    y_ref[...] = y.astype(y_ref.dtype)
    if has_router:
        lo_ref[...] = _mm_f32(y, rw_ref[...]) + rb_ref[...]


def norm_mod(h, g, geo, *, mod=None, shift_k=None, scale_k=None, delta=None, gate_k=None, prev_mod=None,
             router=None, emit_h=True, out_dtype=BF16, lat_only=False):
    m, d = h.shape
    tt, tps, ctx_t, nb = geo["tt"], geo["tiles_per_seq"], geo["ctx_tiles"], geo["batch"]
    if lat_only:
        lat_t = tps - ctx_t
        grid = (nb * lat_t,)
        in_row = lambda i: (i // lat_t) * tps + ctx_t + i % lat_t
        mod_row = lambda i: i // lat_t
        out_rows = nb * lat_t * tt
    else:
        grid = (m // tt,)
        in_row = lambda i: i
        mod_row = lambda i: jnp.where(i % tps < ctx_t, nb, i // tps)
        out_rows = m
    tok = pl.BlockSpec((tt, d), lambda i: (in_row(i), 0))
    out_tok = pl.BlockSpec((tt, d), lambda i: (i, 0))

    def mod_spec(k):
        return pl.BlockSpec((1, 1, d), lambda i: (mod_row(i) * N_MOD + k, 0, 0))

    args, specs = [h], [tok]
    if delta is not None:
        args += [delta, prev_mod]
        specs += [tok, mod_spec(gate_k)]
    args.append(g.reshape(1, d))
    specs.append(pl.BlockSpec((1, d), lambda i: (0, 0)))
    if mod is not None:
        args += [mod, mod]
        specs += [mod_spec(shift_k), mod_spec(scale_k)]
    if router is not None:
        rw, rb = router
        args += [rw, rb.reshape(1, -1)]
        specs += [pl.BlockSpec(rw.shape, lambda i: (0, 0)), pl.BlockSpec((1, rw.shape[1]), lambda i: (0, 0))]
    out_shape, out_specs = [], []
    if emit_h:
        out_shape.append(jax.ShapeDtypeStruct((out_rows, d), F32))
        out_specs.append(out_tok)
    out_shape.append(jax.ShapeDtypeStruct((out_rows, d), out_dtype))
    out_specs.append(out_tok)
    if router is not None:
        ne = router[0].shape[1]
        out_shape.append(jax.ShapeDtypeStruct((out_rows, ne), F32))
        out_specs.append(pl.BlockSpec((tt, ne), lambda i: (i, 0)))
    kern = functools.partial(_norm_kernel, has_delta=delta is not None, has_mod=mod is not None,
                             has_router=router is not None, emit_h=emit_h)
    return pl.pallas_call(kern, grid=grid, in_specs=specs, out_specs=out_specs, out_shape=out_shape,
                          compiler_params=_cp(("arbitrary",)), name="norm_mod")(*args)


def _matmul_kernel(x_ref, w_ref, o_ref):
    o_ref[...] = jnp.dot(x_ref[...], w_ref[...], preferred_element_type=F32).astype(o_ref.dtype)


def matmul(x, w, out_dtype=F32):
    m, k = x.shape
    n = w.shape[1]
    tm = _pick(m, (1024, 512, 256, 128))
    tn = _pick(n, (768, 512, 256, 128))
    return pl.pallas_call(
        _matmul_kernel,
        grid=(m // tm, n // tn),
        in_specs=[pl.BlockSpec((tm, k), lambda i, j: (i, 0)), pl.BlockSpec((k, tn), lambda i, j: (0, j))],
        out_specs=pl.BlockSpec((tm, tn), lambda i, j: (i, j)),
        out_shape=jax.ShapeDtypeStruct((m, n), out_dtype),
        compiler_params=_cp(("arbitrary", "arbitrary")),
        name="in_proj",
    )(x, w)


def _conv4(x, xp, xn, prev_ok, next_ok, cw_ref, cb_ref, ext_ref):
    tt = x.shape[0]
    ext_ref[0:HALO, :] = xp * prev_ok
    ext_ref[HALO:HALO + tt, :] = x
    ext_ref[HALO + tt:2 * HALO + tt, :] = xn * next_ok
    y = cb_ref[...] + cw_ref[2:3, :] * x
    y = y + cw_ref[0:1, :] * ext_ref[HALO - 2:HALO - 2 + tt, :]
    y = y + cw_ref[1:2, :] * ext_ref[HALO - 1:HALO - 1 + tt, :]
    y = y + cw_ref[3:4, :] * ext_ref[HALO + 1:HALO + 1 + tt, :]
    return y


def _seg_edges(tile, ctx_t, n_t):
    first = jnp.logical_or(tile == 0, tile == ctx_t)
    last = jnp.logical_or(tile == ctx_t - 1, tile == n_t - 1)
    return jnp.where(first, 0.0, 1.0).astype(F32), jnp.where(last, 0.0, 1.0).astype(F32)


def _rev_tile(j, ctx_t, n_t):
    return jnp.where(j < ctx_t, ctx_t - 1 - j, n_t - 1 - (j - ctx_t))


def _halo_specs(width, col_block, tile_of, tt, tps, m):
    r8 = tt // HALO
    last8 = m // HALO - 1

    def cur(b, j):
        return (b * tps + tile_of(j), col_block)

    def prev(b, j):
        return (jnp.maximum((b * tps + tile_of(j)) * r8 - 1, 0), col_block)

    def nxt(b, j):
        return (jnp.minimum((b * tps + tile_of(j) + 1) * r8, last8), col_block)

    return [pl.BlockSpec((tt, width), cur), pl.BlockSpec((HALO, width), prev), pl.BlockSpec((HALO, width), nxt)]


def _lru_kernel(xf_ref, xfp_ref, xfn_ref, xb_ref, xbp_ref, xbn_ref, cw_ref, cb_ref, wa_ref, ba_ref, wx_ref, bx_ref,
                lam_ref, hf_ref, hb_ref, ext_s, a_s, b_s, carry_s, *, tt, ctx_t, n_t):
    j = pl.program_id(1)

    @pl.when(j == 0)
    def _():
        carry_s[...] = jnp.zeros_like(carry_s)

    rows = lax.broadcasted_iota(jnp.int32, (HALO, LRU_WIDTH), 0)
    for d, (x_ref, xp_ref, xn_ref, out_ref) in enumerate(((xf_ref, xfp_ref, xfn_ref, hf_ref),
                                                          (xb_ref, xbp_ref, xbn_ref, hb_ref))):
        tile = j if d == 0 else _rev_tile(j, ctx_t, n_t)
        prev_ok, next_ok = _seg_edges(tile, ctx_t, n_t)
        xc = _conv4(x_ref[...], xp_ref[...], xn_ref[...], prev_ok, next_ok, cw_ref, cb_ref, ext_s)
        sp = _softplus(-lam_ref[d:d + 1, :])
        for g in range(LRU_BLOCKS):
            sl = slice(g * LRU_BLOCK, (g + 1) * LRU_BLOCK)
            xg = xc[:, sl]
            r = jax.nn.sigmoid(_mm(xg, wa_ref[d, g]) + ba_ref[d:d + 1, sl])
            ig = jax.nn.sigmoid(_mm(xg, wx_ref[d, g]) + bx_ref[d:d + 1, sl])
            log_a = -LRU_C * r * sp[:, sl]
            a = jnp.exp(log_a)
            a_s[d, :, sl] = a
            b_s[d, :, sl] = jnp.sqrt(-jnp.tanh(log_a) * (a * a + 1.0)) * (ig * xg)

        n_grp = tt // HALO

        def group(gi, carry, d=d, out_ref=out_ref):
            g0 = gi if d == 0 else n_grp - 1 - gi
            off = pl.multiple_of(g0 * HALO, HALO)
            av = a_s[d, pl.ds(off, HALO), :]
            bv = b_s[d, pl.ds(off, HALO), :]
            for s in (1, 2, 4):
                shift = s if d == 0 else HALO - s
                a_sh = pltpu.roll(av, shift, 0)
                b_sh = pltpu.roll(bv, shift, 0)
                ok = (rows >= s) if d == 0 else (rows < HALO - s)
                bv = jnp.where(ok, av * b_sh + bv, bv)
                av = jnp.where(ok, av * a_sh, av)
            hv = bv + av * carry
            out_ref[pl.ds(off, HALO), :] = hv
            return hv[HALO - 1:HALO, :] if d == 0 else hv[0:1, :]

        carry_s[d:d + 1, :] = lax.fori_loop(0, n_grp, group, carry_s[d:d + 1, :])


def lru_scan(zin, p, geo):
    m = zin.shape[0]
    tt, tps, ctx_t, nb = geo["tt"], geo["tiles_per_seq"], geo["ctx_tiles"], geo["batch"]
    cb = COL_LU // LRU_WIDTH
    fwd = _halo_specs(LRU_WIDTH, cb, lambda j: j, tt, tps, m)
    bwd = _halo_specs(LRU_WIDTH, cb, lambda j: _rev_tile(j, ctx_t, tps), tt, tps, m)
    full = lambda shape: pl.BlockSpec(shape, lambda b, j: (0,) * len(shape))
    w_specs = [full((4, LRU_WIDTH)), full((1, LRU_WIDTH)), full((2, LRU_BLOCKS, LRU_BLOCK, LRU_BLOCK)),
               full((2, LRU_WIDTH)), full((2, LRU_BLOCKS, LRU_BLOCK, LRU_BLOCK)), full((2, LRU_WIDTH)),
               full((2, LRU_WIDTH))]
    out_f = pl.BlockSpec((tt, LRU_WIDTH), lambda b, j: (b * tps + j, 0))
    out_b = pl.BlockSpec((tt, LRU_WIDTH), lambda b, j: (b * tps + _rev_tile(j, ctx_t, tps), 0))
    kern = functools.partial(_lru_kernel, tt=tt, ctx_t=ctx_t, n_t=tps)
    return pl.pallas_call(
        kern, grid=(nb, tps), in_specs=fwd + bwd + w_specs, out_specs=[out_f, out_b],
        out_shape=[jax.ShapeDtypeStruct((m, LRU_WIDTH), F32)] * 2,
        scratch_shapes=[pltpu.VMEM((tt + 2 * HALO, LRU_WIDTH), F32), pltpu.VMEM((2, tt, LRU_WIDTH), F32),
                        pltpu.VMEM((2, tt, LRU_WIDTH), F32), pltpu.VMEM((2, LRU_WIDTH), F32)],
        compiler_params=_cp(("arbitrary", "arbitrary")), name="lru_scan",
    )(zin, zin, zin, zin, zin, zin, p["lru_conv_w"], p["lru_conv_b"].reshape(1, -1), p["lru_w_a"], p["lru_b_a"],
      p["lru_w_x"], p["lru_b_x"], p["lru_lambda"])


def _gdn_prep_kernel(x_ref, xp_ref, xn_ref, ba_ref, cw_ref, cb_ref, nega_ref, dtb_ref, qkv_ref, bg_ref, ext_s,
                     *, ctx_t, n_t):
    tile = pl.program_id(1)
    prev_ok, next_ok = _seg_edges(tile, ctx_t, n_t)
    xc = _conv4(x_ref[...], xp_ref[...], xn_ref[...], prev_ok, next_ok, cw_ref, cb_ref, ext_s)
    xc = xc * jax.nn.sigmoid(xc)
    nq = GDN_HEADS * GDN_DK
    for hd in range(GDN_HEADS):
        for base, scale in ((0, GDN_DK ** -0.5), (nq, 1.0)):
            sl = slice(base + hd * GDN_DK, base + (hd + 1) * GDN_DK)
            v = xc[:, sl]
            qkv_ref[:, sl] = v * (lax.rsqrt(jnp.sum(v * v, axis=-1, keepdims=True) + NORM_EPS) * scale)
    qkv_ref[:, 2 * nq:] = xc[:, 2 * nq:]
    ba = ba_ref[...]
    cols = lax.broadcasted_iota(jnp.int32, ba.shape, 1)
    gval = nega_ref[...] * _softplus(ba + dtb_ref[...])
    bg_ref[...] = jnp.where(cols < 2 * GDN_HEADS, jax.nn.sigmoid(ba), gval)


def gdn_prep(zin, p, geo):
    m = zin.shape[0]
    tt, tps, ctx_t, nb = geo["tt"], geo["tiles_per_seq"], geo["ctx_tiles"], geo["batch"]
    specs = _halo_specs(GDN_QKV, COL_QKV // GDN_QKV, lambda j: j, tt, tps, m)
    specs.append(pl.BlockSpec((tt, 128), lambda b, j: (b * tps + j, COL_BA // 128)))
    full = lambda shape: pl.BlockSpec(shape, lambda b, j: (0,) * len(shape))
    specs += [full((4, GDN_QKV)), full((1, GDN_QKV)), full((1, 128)), full((1, 128))]
    pad = jnp.zeros((1, 128), F32)
    nega = pad.at[0, 8:16].set(-jnp.exp(p["gdn_a_log"].reshape(-1)))
    dtb = pad.at[0, 8:16].set(p["gdn_dt_bias"].reshape(-1))
    kern = functools.partial(_gdn_prep_kernel, ctx_t=ctx_t, n_t=tps)
    return pl.pallas_call(
        kern, grid=(nb, tps), in_specs=specs,
        out_specs=[pl.BlockSpec((tt, GDN_QKV), lambda b, j: (b * tps + j, 0)),
                   pl.BlockSpec((tt, 128), lambda b, j: (b * tps + j, 0))],
        out_shape=[jax.ShapeDtypeStruct((m, GDN_QKV), F32), jax.ShapeDtypeStruct((m, 128), F32)],
        scratch_shapes=[pltpu.VMEM((tt + 2 * HALO, GDN_QKV), F32)],
        compiler_params=_cp(("arbitrary", "arbitrary")), name="gdn_prep",
    )(zin, zin, zin, zin, p["gdn_conv_w"], p["gdn_conv_b"].reshape(1, -1), nega, dtb)


def _gdn_chunk(q, k, v, beta, g, gam, s_prev, rev):
    c = GDN_CHUNK
    ii = lax.broadcasted_iota(jnp.int32, (c, c), 0)
    jj = lax.broadcasted_iota(jnp.int32, (c, c), 1)
    incl = (ii <= jj) if rev else (ii >= jj)
    strict = (ii < jj) if rev else (ii > jj)
    blk = (ii // 16) == (jj // 16)
    eye = (ii == jj).astype(F32)
    lf = incl.astype(F32)
    uf = strict.astype(F32)
    diff = _mm_f32(lf, g * uf)
    dec_incl = jnp.where(incl, jnp.exp(diff), 0.0)
    dec_strict = jnp.where(strict, dec_incl, 0.0)
    g_last = gam[0:1, :] if rev else gam[c - 1:c, :]
    eg = jnp.exp(gam)
    kb = k * beta
    a = _mm_nt(kb, k) * dec_strict
    rhs = jnp.concatenate([v * beta, kb * eg], axis=1)
    a_d = jnp.where(blk, a, 0.0)
    a_o = a - a_d
    a2 = _mm(a_d, a_d)
    a4 = _mm(a2, a2)
    a8 = _mm(a4, a4)
    dinv = eye - a_d
    dinv = dinv + _mm(dinv, a2)
    dinv = dinv + _mm(dinv, a4)
    dinv = dinv + _mm(dinv, a8)
    n1 = _mm(dinv, a_o)
    n2 = _mm(n1, n1)
    x = _mm(dinv, rhs)
    x = x + _mm(n2, x)
    x = x - _mm(n1, x)
    u = x[:, :GDN_DV]
    w = x[:, GDN_DV:]
    qk = _mm_nt(q, k) * dec_incl
    v_new = u - _mm(w, s_prev)
    o = _mm(q * eg, s_prev) + _mm(qk, v_new)
    s_new = s_prev * jnp.exp(g_last) + _mm_tn(k * jnp.exp(g_last - gam), v_new)
    return o, s_new


def _gdn_kernel(xf_ref, bgf_ref, xb_ref, bgb_ref, of_ref, ob_ref, s_s):
    j = pl.program_id(1)

    @pl.when(j == 0)
    def _():
        s_s[...] = jnp.zeros_like(s_s)

    c = GDN_CHUNK
    ii = lax.broadcasted_iota(jnp.int32, (c, c), 0)
    jj = lax.broadcasted_iota(jnp.int32, (c, c), 1)
    nq = GDN_HEADS * GDN_DK
    for d, (x_ref, bg_ref, o_ref) in enumerate(((xf_ref, bgf_ref, of_ref), (xb_ref, bgb_ref, ob_ref))):
        rev = d == 1
        lf = ((ii <= jj) if rev else (ii >= jj)).astype(F32)
        bg = bg_ref[...]
        gam_all = _mm_f32(lf, bg)
        for hd in range(GDN_HEADS):
            cb = d * GDN_HEADS + hd
            cg = 2 * GDN_HEADS + cb
            q = x_ref[:, hd * GDN_DK:(hd + 1) * GDN_DK]
            k = x_ref[:, nq + hd * GDN_DK:nq + (hd + 1) * GDN_DK]
            v = x_ref[:, 2 * nq + hd * GDN_DV:2 * nq + (hd + 1) * GDN_DV]
            o, s_new = _gdn_chunk(q, k, v, bg[:, cb:cb + 1], bg[:, cg:cg + 1], gam_all[:, cg:cg + 1],
                                  s_s[d, hd], rev)
            o_ref[:, hd * GDN_DV:(hd + 1) * GDN_DV] = o
            s_s[d, hd] = s_new


def gdn_scan(qkvn, bg, geo):
    m = qkvn.shape[0]
    c = GDN_CHUNK
    nb, s_len, ctx = geo["batch"], geo["s_len"], geo["ctx"]
    n_c, ctx_c = s_len // c, ctx // c
    fwd = lambda b, j: (b * n_c + j, 0)
    bwd = lambda b, j: (b * n_c + _rev_tile(j, ctx_c, n_c), 0)
    w = GDN_HEADS * GDN_DV
    return pl.pallas_call(
        _gdn_kernel, grid=(nb, n_c),
        in_specs=[pl.BlockSpec((c, GDN_QKV), fwd), pl.BlockSpec((c, 128), fwd),
                  pl.BlockSpec((c, GDN_QKV), bwd), pl.BlockSpec((c, 128), bwd)],
        out_specs=[pl.BlockSpec((c, w), fwd), pl.BlockSpec((c, w), bwd)],
        out_shape=[jax.ShapeDtypeStruct((m, w), F32)] * 2,
        scratch_shapes=[pltpu.VMEM((2, GDN_HEADS, GDN_DK, GDN_DV), F32)],
        compiler_params=_cp(("arbitrary", "arbitrary")), name="gdn_scan",
    )(qkvn, bg, qkvn, bg)


def _fourier_kernel(c_ref, s_ref, u_ref, cc_ref, sc_ref, o_ref, *, scale):
    u = u_ref[0]
    a = jnp.dot(c_ref[...], u, preferred_element_type=F32)
    b = jnp.dot(s_ref[...], u, preferred_element_type=F32)
    o = _mm(a, cc_ref[...]) - _mm(b, sc_ref[...])
    o_ref[0] = o * scale


def _dft_mats(t):
    idx = jnp.arange(t, dtype=jnp.int32)
    ang = ((idx[:, None] * idx[None, :]) % t).astype(F32) * (2.0 * math.pi / t)
    return jnp.cos(ang).astype(BF16), jnp.sin(ang).astype(BF16)


def fourier(u, chan_c, chan_s):
    nb, t, w = u.shape
    cm, sm = _dft_mats(t)
    tf = _pick(t, (512, 256, 128))
    kern = functools.partial(_fourier_kernel, scale=1.0 / math.sqrt(t * FFT_GROUP))
    return pl.pallas_call(
        kern, grid=(t // tf, nb),
        in_specs=[pl.BlockSpec((tf, t), lambda f, b: (f, 0)), pl.BlockSpec((tf, t), lambda f, b: (f, 0)),
                  pl.BlockSpec((1, t, w), lambda f, b: (b, 0, 0)),
                  pl.BlockSpec((w, w), lambda f, b: (0, 0)), pl.BlockSpec((w, w), lambda f, b: (0, 0))],
        out_specs=pl.BlockSpec((1, tf, w), lambda f, b: (b, f, 0)),
        out_shape=jax.ShapeDtypeStruct((nb, t, w), F32),
        compiler_params=_cp(("arbitrary", "arbitrary")), name="fourier",
    )(cm, sm, u, chan_c, chan_s)


def _mla_prep_kernel(cq_ref, ckv_ref, misc_ref, cos_ref, sin_ref, gq_ref, gkv_ref, wq_ref, wkv_ref,
                     q_ref, k_ref, v_ref, *, q_scale):
    def rms(x, g):
        return x * lax.rsqrt(jnp.mean(x * x, axis=-1, keepdims=True) + NORM_EPS) * g

    cos = cos_ref[...]
    sin = sin_ref[...]
    qa = _mm(rms(cq_ref[...], gq_ref[...]), wq_ref[...])
    kva = _mm(rms(ckv_ref[...], gkv_ref[...]), wkv_ref[...])
    misc = misc_ref[...]
    k_rope = misc[:, :MLA_ROPE] * cos + misc[:, MLA_ROPE:] * sin
    zpad = jnp.zeros((cos.shape[0], MLA_QK_PAD - MLA_NOPE - MLA_ROPE), F32)
    for hd in range(MLA_HEADS):
        qh = qa[:, hd * 256:(hd + 1) * 256]
        q_rope = qh[:, MLA_NOPE:MLA_NOPE + MLA_ROPE] * cos + qh[:, MLA_NOPE + MLA_ROPE:] * sin
        q_ref[0, hd] = (jnp.concatenate([qh[:, :MLA_NOPE], q_rope, zpad], axis=1) * q_scale).astype(BF16)
        kh = kva[:, hd * 256:hd * 256 + MLA_NOPE]
        k_ref[0, hd] = jnp.concatenate([kh, k_rope, zpad], axis=1).astype(BF16)
        v_ref[0, hd] = kva[:, hd * 256 + MLA_NOPE:(hd + 1) * 256].astype(BF16)


def mla_prep(zin, cos_t, sin_t, p, geo):
    tt, tps, nb, s_len = geo["tt"], geo["tiles_per_seq"], geo["batch"], geo["s_len"]
    row = lambda b, j: b * tps + j
    full = lambda shape: pl.BlockSpec(shape, lambda b, j: (0,) * len(shape))
    hs = lambda w: pl.BlockSpec((1, MLA_HEADS, tt, w), lambda b, j: (b, 0, j, 0))
    kern = functools.partial(_mla_prep_kernel, q_scale=(MLA_NOPE + MLA_ROPE) ** -0.5)
    return pl.pallas_call(
        kern, grid=(nb, tps),
        in_specs=[pl.BlockSpec((tt, MLA_Q_RANK), lambda b, j: (row(b, j), COL_CQ // MLA_Q_RANK)),
                  pl.BlockSpec((tt, MLA_KV_RANK), lambda b, j: (row(b, j), COL_CKV // MLA_KV_RANK)),
                  pl.BlockSpec((tt, 128), lambda b, j: (row(b, j), COL_MISC // 128)),
                  pl.BlockSpec((tt, MLA_ROPE), lambda b, j: (j, 0)), pl.BlockSpec((tt, MLA_ROPE), lambda b, j: (j, 0)),
                  full((1, MLA_Q_RANK)), full((1, MLA_KV_RANK)),
                  full((MLA_Q_RANK, MLA_HEADS * 256)), full((MLA_KV_RANK, MLA_HEADS * 256))],
        out_specs=[hs(MLA_QK_PAD), hs(MLA_QK_PAD), hs(MLA_V)],
        out_shape=[jax.ShapeDtypeStruct((nb, MLA_HEADS, s_len, MLA_QK_PAD), BF16),
                   jax.ShapeDtypeStruct((nb, MLA_HEADS, s_len, MLA_QK_PAD), BF16),
                   jax.ShapeDtypeStruct((nb, MLA_HEADS, s_len, MLA_V), BF16)],
        compiler_params=_cp(("arbitrary", "arbitrary")), name="mla_prep",
    )(zin, zin, zin, cos_t, sin_t, p["mla_q_norm_g"].reshape(1, -1), p["mla_kv_norm_g"].reshape(1, -1),
      p["mla_wq"], p["mla_wkv"])


def _attn_kernel(q_ref, k_ref, v_ref, o_ref, *, tk, ctx_t, ctx_kv, n_kv):
    qi = pl.program_id(2)
    q = q_ref[0, 0]
    tq = q.shape[0]
    n_chunks = jnp.where(qi < ctx_t, ctx_kv, n_kv)

    def body(c, carry):
        m_i, l_i, acc = carry
        off = pl.multiple_of(c * tk, tk)
        kc = k_ref[0, 0, pl.ds(off, tk), :]
        vc = v_ref[0, 0, pl.ds(off, tk), :]
        s = lax.dot_general(q, kc, (((1,), (1,)), ((), ())), preferred_element_type=F32)
        m_new = jnp.maximum(m_i, jnp.max(s, axis=-1, keepdims=True))
        alpha = jnp.exp(m_i - m_new)
        pr = jnp.exp(s - m_new)
        l_new = alpha * l_i + jnp.sum(pr, axis=-1, keepdims=True)
        acc = alpha * acc + jnp.dot(pr.astype(BF16), vc, preferred_element_type=F32)
        return m_new, l_new, acc

    init = (jnp.full((tq, 1), -1e30, F32), jnp.zeros((tq, 1), F32), jnp.zeros((tq, MLA_V), F32))
    _, l_f, acc = lax.fori_loop(0, n_chunks, body, init)
    o_ref[0] = acc / l_f


def mla_attention(q, k, v, geo):
    nb, nh, s_len, _ = q.shape
    tt, ctx_t = geo["tt"], geo["ctx_tiles"]
    kern = functools.partial(_attn_kernel, tk=tt, ctx_t=ctx_t, ctx_kv=ctx_t, n_kv=s_len // tt)
    return pl.pallas_call(
        kern, grid=(nb, nh, s_len // tt),
        in_specs=[pl.BlockSpec((1, 1, tt, MLA_QK_PAD), lambda b, h, i: (b, h, i, 0)),
                  pl.BlockSpec((1, 1, s_len, MLA_QK_PAD), lambda b, h, i: (b, h, 0, 0)),
                  pl.BlockSpec((1, 1, s_len, MLA_V), lambda b, h, i: (b, h, 0, 0))],
        out_specs=pl.BlockSpec((1, tt, MLA_V), lambda b, h, i: (b, i, h)),
        out_shape=jax.ShapeDtypeStruct((nb, s_len, nh * MLA_V), F32),
        compiler_params=_cp(("arbitrary", "arbitrary", "arbitrary")), name="mla_attention",
    )(q, k, v)


def _merge_kernel(hn_ref, hf_ref, hb_ref, lg_ref, of_ref, ob_ref, z_ref, gg_ref, yc_ref, yd_ref, wg_ref, wb_ref,
                  wo_ref, o_ref, y_s, acc_s, *, n_j):
    j = pl.program_id(1)

    @pl.when(j == 0)
    def _():
        lg = lg_ref[...]
        gelu = 0.5 * lg * (1.0 + jnp.tanh(math.sqrt(2.0 / math.pi) * (lg + 0.044715 * (lg * lg * lg))))
        y_s[0] = (gelu * (hf_ref[...] + hb_ref[...])).astype(BF16)
        for hd in range(GDN_HEADS):
            sl = slice(hd * GDN_DV, (hd + 1) * GDN_DV)
            o = of_ref[:, sl] + ob_ref[:, sl]
            o = o * lax.rsqrt(jnp.mean(o * o, axis=-1, keepdims=True) + NORM_EPS) * gg_ref[...]
            z = z_ref[:, sl]
            y_s[1, :, sl] = (o * (z * jax.nn.sigmoid(z))).astype(BF16)
        y_s[2] = yc_ref[...].astype(BF16)
        y_s[3] = yd_ref[...].astype(BF16)
        acc_s[...] = jnp.zeros_like(acc_s)

    hn = hn_ref[...]
    merged = None
    for i in range(N_BRANCH):
        gate = jax.nn.sigmoid(jnp.dot(hn, wg_ref[i], preferred_element_type=F32))
        term = gate * jnp.dot(y_s[i], wb_ref[i], preferred_element_type=F32)
        merged = term if merged is None else merged + term
    acc_s[...] += jnp.dot(merged.astype(BF16), wo_ref[...], preferred_element_type=F32)

    @pl.when(j == n_j - 1)
    def _():
        o_ref[...] = acc_s[...]


def merge(hn, zin, hf, hb, of, ob, yc, yd, p):
    m, d = hn.shape
    tm = _pick(m, (512, 256, 128))
    tn = 256
    n_j = d // tn
    w = BRANCH_WIDTH
    tok = lambda a_w, cb: pl.BlockSpec((tm, a_w), lambda i, j: (i, cb))
    kern = functools.partial(_merge_kernel, n_j=n_j)
    return pl.pallas_call(
        kern, grid=(m // tm, n_j),
        in_specs=[tok(d, 0), tok(w, 0), tok(w, 0), tok(w, COL_LG // w), tok(w, 0), tok(w, 0), tok(w, COL_Z // w),
                  pl.BlockSpec((1, GDN_DV), lambda i, j: (0, 0)), tok(w, 0), tok(w, 0),
                  pl.BlockSpec((N_BRANCH, d, tn), lambda i, j: (0, 0, j)),
                  pl.BlockSpec((N_BRANCH, w, tn), lambda i, j: (0, 0, j)),
                  pl.BlockSpec((tn, d), lambda i, j: (j, 0))],
        out_specs=pl.BlockSpec((tm, d), lambda i, j: (i, 0)),
        out_shape=jax.ShapeDtypeStruct((m, d), F32),
        scratch_shapes=[pltpu.VMEM((N_BRANCH, tm, w), BF16), pltpu.VMEM((tm, d), F32)],
        compiler_params=_cp(("arbitrary", "arbitrary")), name="merge",
    )(hn, hf, hb, zin, of, ob, zin, p["gdn_norm_g"].reshape(1, -1), yc, yd, p["w_gate"], p["w_branch"], p["w_out"])


def _moe_kernel(be_ref, nv_ref, x_ref, w1_ref, b1_ref, w2_ref, b2_ref, g_ref, o_ref):
    i = pl.program_id(0)

    @pl.when(i < nv_ref[0])
    def _():
        h = jnp.dot(x_ref[...], w1_ref[0], preferred_element_type=F32) + b1_ref[0]
        h_glu = jnp.minimum(h[:, :D_EXPERT], SWIGLU_LIMIT)
        h_lin = jnp.clip(h[:, D_EXPERT:], -SWIGLU_LIMIT, SWIGLU_LIMIT)
        act = h_glu * jax.nn.sigmoid(SWIGLU_ALPHA * h_glu) * (h_lin + 1.0)
        y = jnp.dot(act.astype(BF16), w2_ref[0], preferred_element_type=F32) + b2_ref[0]
        o_ref[...] = y * g_ref[...]

    @pl.when(i >= nv_ref[0])
    def _():
        o_ref[...] = jnp.zeros_like(o_ref)


def moe_experts(xg, gate, block_e, n_valid, w1, b1, w2, b2):
    p_rows, d = xg.shape
    bm = MOE_ROWS
    n_blocks = p_rows // bm
    ne, _, dh = w1.shape
    grid_spec = pltpu.PrefetchScalarGridSpec(
        num_scalar_prefetch=2, grid=(n_blocks,),
        in_specs=[pl.BlockSpec((bm, d), lambda i, be, nv: (i, 0)),
                  pl.BlockSpec((1, d, dh), lambda i, be, nv: (be[i], 0, 0)),
                  pl.BlockSpec((1, 1, dh), lambda i, be, nv: (be[i], 0, 0)),
                  pl.BlockSpec((1, D_EXPERT, d), lambda i, be, nv: (be[i], 0, 0)),
                  pl.BlockSpec((1, 1, d), lambda i, be, nv: (be[i], 0, 0)),
                  pl.BlockSpec((bm, 1), lambda i, be, nv: (i, 0))],
        out_specs=pl.BlockSpec((bm, d), lambda i, be, nv: (i, 0)))
    return pl.pallas_call(
        _moe_kernel, grid_spec=grid_spec, out_shape=jax.ShapeDtypeStruct((p_rows, d), F32),
        compiler_params=_cp(("arbitrary",)), name="moe_experts",
    )(block_e, n_valid, xg, w1, b1.reshape(ne, 1, dh), w2, b2.reshape(ne, 1, d), gate.reshape(p_rows, 1))


def moe_ffn(hn, logits, w1, b1, w2, b2):
    n, d = hn.shape
    bm = MOE_ROWS
    top_val, top_idx = lax.top_k(logits, TOP_K)
    gate = jax.nn.softmax(top_val, axis=-1)
    nk = n * TOP_K
    flat_e = top_idx.reshape(nk).astype(jnp.int32)
    order = jnp.argsort(flat_e, stable=True).astype(jnp.int32)
    se = flat_e[order]
    counts = jnp.bincount(flat_e, length=N_EXPERTS).astype(jnp.int32)
    padded = (counts + bm - 1) // bm * bm
    start = jnp.cumsum(counts) - counts
    pend = jnp.cumsum(padded)
    pstart = pend - padded
    dest = pstart[se] + jnp.arange(nk, dtype=jnp.int32) - start[se]
    n_blocks = (nk + N_EXPERTS * (bm - 1) + bm - 1) // bm
    p_rows = n_blocks * bm
    buf_tok = jnp.zeros((p_rows,), jnp.int32).at[dest].set(order // TOP_K)
    buf_gate = jnp.zeros((p_rows,), F32).at[dest].set(gate.reshape(nk)[order])
    block_e = jnp.minimum(jnp.searchsorted(pend, jnp.arange(n_blocks, dtype=jnp.int32) * bm, side='right'),
                          N_EXPERTS - 1).astype(jnp.int32)
    n_valid = (pend[-1:] // bm).astype(jnp.int32)
    pos = jnp.zeros((nk,), jnp.int32).at[order].set(dest).reshape(n, TOP_K)
    y = moe_experts(hn[buf_tok], buf_gate, block_e, n_valid, w1, b1, w2, b2)
    out = y[pos[:, 0]]
    for kk in range(1, TOP_K):
        out = out + y[pos[:, kk]]
    return out


def _rope_rot(w):
    h = MLA_ROPE // 4
    a, b, c, d = w[..., :h], w[..., h:2 * h], w[..., 2 * h:3 * h], w[..., 3 * h:]
    return jnp.concatenate([-b, a, -d, c], axis=-1)


def _prep_layer(w_in, mla_w_uq, mla_w_ukv):
    d = w_in.shape[0]
    o = np.cumsum([0, 512, 512, GDN_QKV, 512, 8, 8, 512, 512, 256, 64, N_BRANCH * d])
    lu, lg, qkv, z, be, al, fu, cq, ckv, kr, mg = [w_in[:, o[i]:o[i + 1]] for i in range(11)]
    pad = jnp.zeros((d, 128 - 16), w_in.dtype)
    w_main = jnp.concatenate([qkv, lu, lg, z, fu, cq, ckv, kr, _rope_rot(kr), be, al, pad], axis=1).astype(BF16)
    w_gate = jnp.transpose(mg.reshape(d, N_BRANCH, d), (1, 0, 2)).astype(BF16)
    uq = mla_w_uq.reshape(MLA_Q_RANK, MLA_HEADS, MLA_NOPE + MLA_ROPE)
    wq = jnp.concatenate([uq, _rope_rot(uq[..., MLA_NOPE:])], axis=-1).reshape(MLA_Q_RANK, MLA_HEADS * 256)
    return w_main, w_gate, wq.astype(BF16), mla_w_ukv.astype(BF16)


def _rope_tables(ctx, n_lat):
    half = MLA_ROPE // 4
    inv = jnp.power(ROPE_BASE, -jnp.arange(half, dtype=F32) / half)
    t = jnp.arange(n_lat)
    row = (t // GRID_W).astype(F32)[:, None] * inv
    col = (t % GRID_W).astype(F32)[:, None] * inv
    cos = jnp.concatenate([jnp.cos(row), jnp.cos(row), jnp.cos(col), jnp.cos(col)], axis=1)
    sin = jnp.concatenate([jnp.sin(row), jnp.sin(row), jnp.sin(col), jnp.sin(col)], axis=1)
    cos = jnp.concatenate([jnp.ones((ctx, MLA_ROPE), F32), cos], axis=0)
    sin = jnp.concatenate([jnp.zeros((ctx, MLA_ROPE), F32), sin], axis=0)
    return cos, sin


def _chan_dft():
    idx = np.arange(FFT_GROUP)
    ang = 2.0 * np.pi * ((idx[:, None] * idx[None, :]) % FFT_GROUP) / FFT_GROUP
    eye = np.eye(FFT_GROUPS)
    return (jnp.asarray(np.kron(eye, np.cos(ang)), BF16), jnp.asarray(np.kron(eye, np.sin(ang)), BF16))


def kernel(x, c, ctx, c_ctx, w_ada, b_ada, norm1_g, norm2_g, w_in, lru_conv_w, lru_conv_b, lru_w_a, lru_b_a, lru_w_x, lru_b_x, lru_lambda, gdn_conv_w, gdn_conv_b, gdn_a_log, gdn_dt_bias, gdn_norm_g, mla_q_norm_g, mla_kv_norm_g, mla_w_uq, mla_w_ukv, w_branch, w_out, router_w, router_b, exp_w1, exp_b1, exp_w2, exp_b2, final_norm_g):
    nb, n_lat, d = x.shape
    n_ctx = ctx.shape[1]
    depth = w_ada.shape[0]
    s_len = n_ctx + n_lat
    m = nb * s_len
    tt = _pick(math.gcd(n_ctx, n_lat), (256, 128, 64))
    geo = dict(tt=tt, tiles_per_seq=s_len // tt, ctx_tiles=n_ctx // tt, batch=nb, s_len=s_len, ctx=n_ctx)

    mod_rows = 8 * ((nb + 1 + 7) // 8)
    cmat = jnp.zeros((mod_rows, d), F32).at[:nb].set(c).at[nb].set(c_ctx)
    mod_all = ada_table(cmat, w_ada, b_ada).reshape(depth, mod_rows * N_MOD, 1, d)

    cos_t, sin_t = _rope_tables(n_ctx, n_lat)
    chan_c, chan_s = _chan_dft()
    h = jnp.concatenate([ctx, x], axis=1).reshape(m, d)

    delta, prev_mod, gate_k = None, None, None
    for l in range(depth):
        w_main, w_gate, wq, wkv = _prep_layer(w_in[l], mla_w_uq[l], mla_w_ukv[l])
        p = dict(lru_conv_w=lru_conv_w[l], lru_conv_b=lru_conv_b[l], lru_w_a=lru_w_a[l].astype(BF16),
                 lru_b_a=lru_b_a[l], lru_w_x=lru_w_x[l].astype(BF16), lru_b_x=lru_b_x[l], lru_lambda=lru_lambda[l],
                 gdn_conv_w=gdn_conv_w[l], gdn_conv_b=gdn_conv_b[l], gdn_a_log=gdn_a_log[l],
                 gdn_dt_bias=gdn_dt_bias[l], gdn_norm_g=gdn_norm_g[l], mla_q_norm_g=mla_q_norm_g[l],
                 mla_kv_norm_g=mla_kv_norm_g[l], mla_wq=wq, mla_wkv=wkv, w_gate=w_gate,
                 w_branch=w_branch[l].astype(BF16), w_out=w_out[l].astype(BF16))
        mod = mod_all[l]
        h, hn = norm_mod(h, norm1_g[l], geo, mod=mod, shift_k=0, scale_k=1, delta=delta, gate_k=gate_k,
                         prev_mod=prev_mod)
        zin = matmul(hn, w_main)
        hf, hb = lru_scan(zin, p, geo)
        qkvn, bg = gdn_prep(zin, p, geo)
        of, ob = gdn_scan(qkvn, bg, geo)
        fu = zin[:, COL_FU:COL_FU + FFT_WIDTH].astype(BF16).reshape(nb, s_len, FFT_WIDTH)
        yc = jnp.concatenate([fourier(fu[:, :n_ctx], chan_c, chan_s), fourier(fu[:, n_ctx:], chan_c, chan_s)],
                             axis=1).reshape(m, FFT_WIDTH)
        q, k, v = mla_prep(zin, cos_t, sin_t, p, geo)
        yd = mla_attention(q, k, v, geo).reshape(m, MLA_HEADS * MLA_V)
        mixed = merge(hn, zin, hf, hb, of, ob, yc, yd, p)
        h, hn2, logits = norm_mod(h, norm2_g[l], geo, mod=mod, shift_k=3, scale_k=4, delta=mixed, gate_k=2,
                                  prev_mod=mod, router=(router_w[l], router_b[l]))
        delta = moe_ffn(hn2, logits, exp_w1[l].astype(BF16), exp_b1[l], exp_w2[l].astype(BF16), exp_b2[l])
        prev_mod, gate_k = mod, 5
    (out,) = norm_mod(h, final_norm_g, geo, delta=delta, gate_k=5, prev_mod=prev_mod, emit_h=False,
                      out_dtype=F32, lat_only=True)
    return out.reshape(nb, n_lat, d)
```

```python
import functools
import math

import jax
import jax.numpy as jnp
import numpy as np
from jax import lax
from jax.experimental import pallas as pl
from jax.experimental.pallas import tpu as pltpu

F32 = jnp.float32
BF16 = jnp.bfloat16

GRID_W = 64
NORM_EPS = 1e-6
N_MOD = 6
LRU_WIDTH = 512
LRU_BLOCKS = 4
LRU_BLOCK = 128
LRU_C = 8.0
GDN_HEADS = 4
GDN_DK = 128
GDN_DV = 128
GDN_CHUNK = 64
GDN_QKV = GDN_HEADS * (2 * GDN_DK + GDN_DV)
FFT_GROUPS = 4
FFT_GROUP = 128
FFT_WIDTH = 512
MLA_HEADS = 4
MLA_Q_RANK = 512
MLA_KV_RANK = 256
MLA_NOPE = 128
MLA_ROPE = 64
MLA_V = 128
MLA_QK_PAD = 256
ROPE_BASE = 10000.0
N_BRANCH = 4
BRANCH_WIDTH = 512
N_EXPERTS = 32
TOP_K = 4
D_EXPERT = 512
SWIGLU_LIMIT = 7.0
SWIGLU_ALPHA = 1.702
MOE_ROWS = 256
KEY_EXPERT_SHIFT = 20
KEY_PAD_BIT = 19

COL_QKV = 0
COL_LU = 1536
COL_LG = 2048
COL_Z = 2560
COL_FU = 3072
COL_CQ = 3584
COL_CKV = 4096
COL_MISC = 4352
COL_BA = 4480
ZIN_W = 4608

VMEM_LIMIT = 56 * 1024 * 1024
HALO = 8


def _cp(sem, vmem=VMEM_LIMIT):
    return pltpu.CompilerParams(dimension_semantics=sem, vmem_limit_bytes=vmem)


def _pick(n, cands):
    for c in cands:
        if n % c == 0:
            return c
    raise ValueError(f"no tile for {n} in {cands}")


def _mm(a, b):
    return jnp.dot(a.astype(BF16), b.astype(BF16), preferred_element_type=F32)


def _mm_nt(a, b):
    return lax.dot_general(a.astype(BF16), b.astype(BF16), (((1,), (1,)), ((), ())), preferred_element_type=F32)


def _mm_tn(a, b):
    return lax.dot_general(a.astype(BF16), b.astype(BF16), (((0,), (0,)), ((), ())), preferred_element_type=F32)


def _mm_f32(a, b):
    return jnp.dot(a, b, preferred_element_type=F32, precision=lax.Precision.HIGHEST)


def _softplus(y):
    return jnp.maximum(y, 0.0) + jnp.log1p(jnp.exp(-jnp.abs(y)))


def _ada_kernel(c_ref, w_ref, b_ref, o_ref):
    cv = c_ref[...]
    s = cv * jax.nn.sigmoid(cv)
    o_ref[0] = _mm(s, w_ref[0]) + b_ref[0]


def ada_table(cmat, w_ada, b_ada):
    depth, d, n = w_ada.shape
    rows = cmat.shape[0]
    tn = _pick(n, (1024, 512, 256, 128))
    return pl.pallas_call(
        _ada_kernel,
        grid=(depth, n // tn),
        in_specs=[pl.BlockSpec((rows, d), lambda l, j: (0, 0)),
                  pl.BlockSpec((1, d, tn), lambda l, j: (l, 0, j)),
                  pl.BlockSpec((1, 1, tn), lambda l, j: (l, 0, j))],
        out_specs=pl.BlockSpec((1, rows, tn), lambda l, j: (l, 0, j)),
        out_shape=jax.ShapeDtypeStruct((depth, rows, n), F32),
        compiler_params=_cp(("arbitrary", "arbitrary")),
        name="ada_table",
    )(cmat, w_ada, b_ada.reshape(depth, 1, n))


def _norm_kernel(*refs, delta_kind, has_mod, has_router, emit_h):
    it = iter(refs)
    h_ref = next(it)
    if delta_kind == "dense":
        d_ref = next(it)
        gate_ref = next(it)
    elif delta_kind == "moe":
        yg_ref = next(it)
        mg_ref = next(it)
        gate_ref = next(it)
    g_ref = next(it)
    if has_mod:
        shift_ref = next(it)
        scale_ref = next(it)
    if has_router:
        rw_ref = next(it)
        rb_ref = next(it)
    if emit_h:
        ho_ref = next(it)
    y_ref = next(it)
    if has_router:
        idx_ref = next(it)
        gt_ref = next(it)
        rank_ref = next(it)
        cnt_ref = next(it)

    h = h_ref[...]
    d = h.shape[1]
    if delta_kind == "dense":
        h = h + gate_ref[0] * d_ref[...]
    elif delta_kind == "moe":
        mg = mg_ref[...]
        f = mg[:, 0:1] * yg_ref[:, 0:d].astype(F32)
        for kk in range(1, TOP_K):
            f = f + mg[:, kk:kk + 1] * yg_ref[:, kk * d:(kk + 1) * d].astype(F32)
        h = h + gate_ref[0] * f
    if emit_h:
        ho_ref[...] = h
    y = h * lax.rsqrt(jnp.mean(h * h, axis=-1, keepdims=True) + NORM_EPS) * g_ref[...]
    if has_mod:
        y = y * (1.0 + scale_ref[0]) + shift_ref[0]
    y_ref[...] = y.astype(y_ref.dtype)
    if has_router:
        @pl.when(pl.program_id(0) == 0)
        def _():
            cnt_ref[...] = jnp.zeros_like(cnt_ref)

        logit = _mm_f32(y, rw_ref[...]) + rb_ref[...]
        tt, ne = logit.shape
        lane = lax.broadcasted_iota(jnp.int32, (tt, ne), 1).astype(F32)
        vals, hots = [], []
        for kk in range(TOP_K):
            top = jnp.max(logit, axis=-1, keepdims=True)
            arg = jnp.min(jnp.where(logit == top, lane, float(ne)), axis=-1, keepdims=True)
            hot = lane == arg
            idx_ref[:, kk:kk + 1] = arg.astype(jnp.int32)
            vals.append(top)
            hots.append(hot)
            logit = jnp.where(hot, -jnp.inf, logit)
        exps = [jnp.exp(v - vals[0]) for v in vals]
        den = exps[0] + exps[1] + exps[2] + exps[3]
        for kk in range(TOP_K):
            gt_ref[:, kk:kk + 1] = exps[kk] / den
        hot_all = hots[0].astype(F32) + hots[1].astype(F32) + hots[2].astype(F32) + hots[3].astype(F32)
        ii = lax.broadcasted_iota(jnp.int32, (tt, tt), 0)
        jj = lax.broadcasted_iota(jnp.int32, (tt, tt), 1)
        before = _mm((jj < ii).astype(F32), hot_all) + cnt_ref[...]
        for kk in range(TOP_K):
            rank_ref[:, kk:kk + 1] = jnp.sum(jnp.where(hots[kk], before, 0.0), axis=-1, keepdims=True).astype(jnp.int32)
        cnt_ref[...] += jnp.sum(hot_all, axis=0, keepdims=True)


def norm_mod(h, g, geo, *, mod=None, shift_k=None, scale_k=None, delta=None, gate_k=None, prev_mod=None,
             router=None, emit_h=True, out_dtype=None, lat_only=False):
    m, d = h.shape
    out_dtype = BF16 if out_dtype is None else out_dtype
    tt, tps, ctx_t, nb = geo["tt"], geo["tiles_per_seq"], geo["ctx_tiles"], geo["batch"]
    if lat_only:
        lat_t = tps - ctx_t
        grid = (nb * lat_t,)
        in_row = lambda i: (i // lat_t) * tps + ctx_t + i % lat_t
        mod_row = lambda i: i // lat_t
        out_rows = nb * lat_t * tt
    else:
        grid = (m // tt,)
        in_row = lambda i: i
        mod_row = lambda i: jnp.where(i % tps < ctx_t, nb, i // tps)
        out_rows = m
    tok = lambda w: pl.BlockSpec((tt, w), lambda i: (in_row(i), 0))
    out_tok = lambda w: pl.BlockSpec((tt, w), lambda i: (i, 0))

    def mod_spec(k):
        return pl.BlockSpec((1, 1, d), lambda i: (mod_row(i) * N_MOD + k, 0, 0))

    args, specs = [h], [tok(d)]
    delta_kind = None if delta is None else delta[0]
    if delta_kind == "dense":
        args += [delta[1], prev_mod]
        specs += [tok(d), mod_spec(gate_k)]
    elif delta_kind == "moe":
        args += [delta[1], delta[2], prev_mod]
        specs += [tok(TOP_K * d), tok(TOP_K), mod_spec(gate_k)]
    args.append(g.reshape(1, d))
    specs.append(pl.BlockSpec((1, d), lambda i: (0, 0)))
    if mod is not None:
        args += [mod, mod]
        specs += [mod_spec(shift_k), mod_spec(scale_k)]
    if router is not None:
        rw, rb = router
        ne = rw.shape[1]
        args += [rw, rb.reshape(1, ne)]
        specs += [pl.BlockSpec(rw.shape, lambda i: (0, 0)), pl.BlockSpec((1, ne), lambda i: (0, 0))]
    out_shape, out_specs = [], []
    if emit_h:
        out_shape.append(jax.ShapeDtypeStruct((out_rows, d), F32))
        out_specs.append(out_tok(d))
    out_shape.append(jax.ShapeDtypeStruct((out_rows, d), out_dtype))
    out_specs.append(out_tok(d))
    if router is not None:
        out_shape += [jax.ShapeDtypeStruct((out_rows, TOP_K), jnp.int32), jax.ShapeDtypeStruct((out_rows, TOP_K), F32),
                      jax.ShapeDtypeStruct((out_rows, TOP_K), jnp.int32), jax.ShapeDtypeStruct((1, ne), F32)]
        out_specs += [out_tok(TOP_K), out_tok(TOP_K), out_tok(TOP_K), pl.BlockSpec((1, ne), lambda i: (0, 0))]
    kern = functools.partial(_norm_kernel, delta_kind=delta_kind, has_mod=mod is not None,
                             has_router=router is not None, emit_h=emit_h)
    return pl.pallas_call(kern, grid=grid, in_specs=specs, out_specs=out_specs, out_shape=out_shape,
                          compiler_params=_cp(("arbitrary",)), name="norm_mod")(*args)


def _matmul_kernel(x_ref, w_ref, o_ref):
    o_ref[...] = jnp.dot(x_ref[...], w_ref[...], preferred_element_type=F32).astype(o_ref.dtype)


def matmul(x, w, out_dtype=F32):
    m, k = x.shape
    n = w.shape[1]
    tm = _pick(m, (1024, 512, 256, 128))
    tn = _pick(n, (768, 512, 256, 128))
    return pl.pallas_call(
        _matmul_kernel,
        grid=(m // tm, n // tn),
        in_specs=[pl.BlockSpec((tm, k), lambda i, j: (i, 0)), pl.BlockSpec((k, tn), lambda i, j: (0, j))],
        out_specs=pl.BlockSpec((tm, tn), lambda i, j: (i, j)),
        out_shape=jax.ShapeDtypeStruct((m, n), out_dtype),
        compiler_params=_cp(("arbitrary", "arbitrary")),
        name="in_proj",
    )(x, w)


def _conv4(x, xp, xn, prev_ok, next_ok, cw_ref, cb_ref, ext_ref):
    tt = x.shape[0]
    ext_ref[0:HALO, :] = xp * prev_ok
    ext_ref[HALO:HALO + tt, :] = x
    ext_ref[HALO + tt:2 * HALO + tt, :] = xn * next_ok
    y = cb_ref[...] + cw_ref[2:3, :] * x
    y = y + cw_ref[0:1, :] * ext_ref[HALO - 2:HALO - 2 + tt, :]
    y = y + cw_ref[1:2, :] * ext_ref[HALO - 1:HALO - 1 + tt, :]
    y = y + cw_ref[3:4, :] * ext_ref[HALO + 1:HALO + 1 + tt, :]
    return y


def _seg_edges(tile, ctx_t, n_t):
    first = jnp.logical_or(tile == 0, tile == ctx_t)
    last = jnp.logical_or(tile == ctx_t - 1, tile == n_t - 1)
    return jnp.where(first, 0.0, 1.0).astype(F32), jnp.where(last, 0.0, 1.0).astype(F32)


def _rev_tile(j, ctx_t, n_t):
    return jnp.where(j < ctx_t, ctx_t - 1 - j, n_t - 1 - (j - ctx_t))


def _halo_specs(width, col_block, tile_of, tt, tps, m):
    r8 = tt // HALO
    last8 = m // HALO - 1

    def cur(b, j):
        return (b * tps + tile_of(j), col_block)

    def prev(b, j):
        return (jnp.maximum((b * tps + tile_of(j)) * r8 - 1, 0), col_block)

    def nxt(b, j):
        return (jnp.minimum((b * tps + tile_of(j) + 1) * r8, last8), col_block)

    return [pl.BlockSpec((tt, width), cur), pl.BlockSpec((HALO, width), prev), pl.BlockSpec((HALO, width), nxt)]


def _lru_kernel(xf_ref, xfp_ref, xfn_ref, xb_ref, xbp_ref, xbn_ref, cw_ref, cb_ref, wa_ref, ba_ref, wx_ref, bx_ref,
                lam_ref, hf_ref, hb_ref, ext_s, a_s, b_s, carry_s, *, tt, ctx_t, n_t):
    j = pl.program_id(1)

    @pl.when(j == 0)
    def _():
        carry_s[...] = jnp.zeros_like(carry_s)

    rows = lax.broadcasted_iota(jnp.int32, (HALO, LRU_WIDTH), 0)
    for d, (x_ref, xp_ref, xn_ref, out_ref) in enumerate(((xf_ref, xfp_ref, xfn_ref, hf_ref),
                                                          (xb_ref, xbp_ref, xbn_ref, hb_ref))):
        tile = j if d == 0 else _rev_tile(j, ctx_t, n_t)
        prev_ok, next_ok = _seg_edges(tile, ctx_t, n_t)
        xc = _conv4(x_ref[...], xp_ref[...], xn_ref[...], prev_ok, next_ok, cw_ref, cb_ref, ext_s)
        sp = _softplus(-lam_ref[d:d + 1, :])
        for g in range(LRU_BLOCKS):
            sl = slice(g * LRU_BLOCK, (g + 1) * LRU_BLOCK)
            xg = xc[:, sl]
            r = jax.nn.sigmoid(_mm(xg, wa_ref[d, g]) + ba_ref[d:d + 1, sl])
            ig = jax.nn.sigmoid(_mm(xg, wx_ref[d, g]) + bx_ref[d:d + 1, sl])
            log_a = -LRU_C * r * sp[:, sl]
            a = jnp.exp(log_a)
            a_s[d, :, sl] = a
            b_s[d, :, sl] = jnp.sqrt(-jnp.tanh(log_a) * (a * a + 1.0)) * (ig * xg)

        n_grp = tt // HALO

        def group(gi, carry, d=d, out_ref=out_ref):
            g0 = gi if d == 0 else n_grp - 1 - gi
            off = pl.multiple_of(g0 * HALO, HALO)
            av = a_s[d, pl.ds(off, HALO), :]
            bv = b_s[d, pl.ds(off, HALO), :]
            for s in (1, 2, 4):
                shift = s if d == 0 else HALO - s
                a_sh = pltpu.roll(av, shift, 0)
                b_sh = pltpu.roll(bv, shift, 0)
                ok = (rows >= s) if d == 0 else (rows < HALO - s)
                bv = jnp.where(ok, av * b_sh + bv, bv)
                av = jnp.where(ok, av * a_sh, av)
            hv = bv + av * carry
            out_ref[pl.ds(off, HALO), :] = hv
            return hv[HALO - 1:HALO, :] if d == 0 else hv[0:1, :]

        carry_s[d:d + 1, :] = lax.fori_loop(0, n_grp, group, carry_s[d:d + 1, :])


def lru_scan(zin, p, geo):
    m = zin.shape[0]
    tt, tps, ctx_t, nb = geo["tt"], geo["tiles_per_seq"], geo["ctx_tiles"], geo["batch"]
    cb = COL_LU // LRU_WIDTH
    fwd = _halo_specs(LRU_WIDTH, cb, lambda j: j, tt, tps, m)
    bwd = _halo_specs(LRU_WIDTH, cb, lambda j: _rev_tile(j, ctx_t, tps), tt, tps, m)
    full = lambda shape: pl.BlockSpec(shape, lambda b, j: (0,) * len(shape))
    w_specs = [full((4, LRU_WIDTH)), full((1, LRU_WIDTH)), full((2, LRU_BLOCKS, LRU_BLOCK, LRU_BLOCK)),
               full((2, LRU_WIDTH)), full((2, LRU_BLOCKS, LRU_BLOCK, LRU_BLOCK)), full((2, LRU_WIDTH)),
               full((2, LRU_WIDTH))]
    out_f = pl.BlockSpec((tt, LRU_WIDTH), lambda b, j: (b * tps + j, 0))
    out_b = pl.BlockSpec((tt, LRU_WIDTH), lambda b, j: (b * tps + _rev_tile(j, ctx_t, tps), 0))
    kern = functools.partial(_lru_kernel, tt=tt, ctx_t=ctx_t, n_t=tps)
    return pl.pallas_call(
        kern, grid=(nb, tps), in_specs=fwd + bwd + w_specs, out_specs=[out_f, out_b],
        out_shape=[jax.ShapeDtypeStruct((m, LRU_WIDTH), F32)] * 2,
        scratch_shapes=[pltpu.VMEM((tt + 2 * HALO, LRU_WIDTH), F32), pltpu.VMEM((2, tt, LRU_WIDTH), F32),
                        pltpu.VMEM((2, tt, LRU_WIDTH), F32), pltpu.VMEM((2, LRU_WIDTH), F32)],
        compiler_params=_cp(("arbitrary", "arbitrary")), name="lru_scan",
    )(zin, zin, zin, zin, zin, zin, p["lru_conv_w"], p["lru_conv_b"].reshape(1, -1), p["lru_w_a"], p["lru_b_a"],
      p["lru_w_x"], p["lru_b_x"], p["lru_lambda"])


def _gdn_prep_kernel(x_ref, xp_ref, xn_ref, ba_ref, cw_ref, cb_ref, nega_ref, dtb_ref, qkv_ref, bg_ref, ext_s,
                     *, ctx_t, n_t):
    tile = pl.program_id(1)
    prev_ok, next_ok = _seg_edges(tile, ctx_t, n_t)
    xc = _conv4(x_ref[...], xp_ref[...], xn_ref[...], prev_ok, next_ok, cw_ref, cb_ref, ext_s)
    xc = xc * jax.nn.sigmoid(xc)
    nq = GDN_HEADS * GDN_DK
    for hd in range(GDN_HEADS):
        for base, scale in ((0, GDN_DK ** -0.5), (nq, 1.0)):
            sl = slice(base + hd * GDN_DK, base + (hd + 1) * GDN_DK)
            v = xc[:, sl]
            qkv_ref[:, sl] = v * (lax.rsqrt(jnp.sum(v * v, axis=-1, keepdims=True) + NORM_EPS) * scale)
    qkv_ref[:, 2 * nq:] = xc[:, 2 * nq:]
    ba = ba_ref[...]
    cols = lax.broadcasted_iota(jnp.int32, ba.shape, 1)
    gval = nega_ref[...] * _softplus(ba + dtb_ref[...])
    bg_ref[...] = jnp.where(cols < 2 * GDN_HEADS, jax.nn.sigmoid(ba), gval)


def gdn_prep(zin, p, geo):
    m = zin.shape[0]
    tt, tps, ctx_t, nb = geo["tt"], geo["tiles_per_seq"], geo["ctx_tiles"], geo["batch"]
    specs = _halo_specs(GDN_QKV, COL_QKV // GDN_QKV, lambda j: j, tt, tps, m)
    specs.append(pl.BlockSpec((tt, 128), lambda b, j: (b * tps + j, COL_BA // 128)))
    full = lambda shape: pl.BlockSpec(shape, lambda b, j: (0,) * len(shape))
    specs += [full((4, GDN_QKV)), full((1, GDN_QKV)), full((1, 128)), full((1, 128))]
    pad = jnp.zeros((1, 128), F32)
    nega = pad.at[0, 8:16].set(-jnp.exp(p["gdn_a_log"].reshape(-1)))
    dtb = pad.at[0, 8:16].set(p["gdn_dt_bias"].reshape(-1))
    kern = functools.partial(_gdn_prep_kernel, ctx_t=ctx_t, n_t=tps)
    return pl.pallas_call(
        kern, grid=(nb, tps), in_specs=specs,
        out_specs=[pl.BlockSpec((tt, GDN_QKV), lambda b, j: (b * tps + j, 0)),
                   pl.BlockSpec((tt, 128), lambda b, j: (b * tps + j, 0))],
        out_shape=[jax.ShapeDtypeStruct((m, GDN_QKV), F32), jax.ShapeDtypeStruct((m, 128), F32)],
        scratch_shapes=[pltpu.VMEM((tt + 2 * HALO, GDN_QKV), F32)],
        compiler_params=_cp(("arbitrary", "arbitrary")), name="gdn_prep",
    )(zin, zin, zin, zin, p["gdn_conv_w"], p["gdn_conv_b"].reshape(1, -1), nega, dtb)


def _gdn_kernel(xf_ref, bgf_ref, xb_ref, bgb_ref, of_ref, ob_ref, s_s, *, gb):
    j = pl.program_id(1)

    @pl.when(j == 0)
    def _():
        s_s[...] = jnp.zeros_like(s_s)

    c = GDN_CHUNK
    nq = GDN_HEADS * GDN_DK
    ii = lax.broadcasted_iota(jnp.int32, (c, c), 0)
    jj = lax.broadcasted_iota(jnp.int32, (c, c), 1)
    incl = ((ii >= jj), (ii <= jj))
    strict = ((ii > jj), (ii < jj))
    blk = jnp.right_shift(ii, 4) == jnp.right_shift(jj, 4)
    eye = (ii == jj).astype(F32)

    chains = []
    for g in range(gb):
        for d, (x_ref, bg_ref) in enumerate(((xf_ref, bgf_ref), (xb_ref, bgb_ref))):
            bg = bg_ref[g]
            gam_all = _mm_f32(incl[d].astype(F32), bg)
            gam_t = gam_all.T
            for hd in range(GDN_HEADS):
                cb = d * GDN_HEADS + hd
                cg = 2 * GDN_HEADS + cb
                chains.append(dict(
                    g=g, d=d, hd=hd,
                    q=x_ref[g, :, hd * GDN_DK:(hd + 1) * GDN_DK],
                    k=x_ref[g, :, nq + hd * GDN_DK:nq + (hd + 1) * GDN_DK],
                    v=x_ref[g, :, 2 * nq + hd * GDN_DV:2 * nq + (hd + 1) * GDN_DV],
                    beta=bg[:, cb:cb + 1], gam=gam_all[:, cg:cg + 1], gam_row=gam_t[cg:cg + 1, :]))

    def put(name, fn):
        vals = [fn(ch) for ch in chains]
        for ch, val in zip(chains, vals):
            ch[name] = val

    put("dec_incl", lambda ch: jnp.where(
        incl[ch["d"]], jnp.exp(jnp.where(incl[ch["d"]], ch["gam"] - ch["gam_row"], 0.0)), 0.0))
    put("dec_strict", lambda ch: jnp.where(strict[ch["d"]], ch["dec_incl"], 0.0))
    put("g_last", lambda ch: ch["gam"][0:1, :] if ch["d"] == 1 else ch["gam"][c - 1:c, :])
    put("eg", lambda ch: jnp.exp(ch["gam"]))
    put("kb", lambda ch: ch["k"] * ch["beta"])
    put("a", lambda ch: _mm_nt(ch["kb"], ch["k"]) * ch["dec_strict"])
    put("rhs", lambda ch: jnp.concatenate([ch["v"] * ch["beta"], ch["kb"] * ch["eg"]], axis=1))
    put("a_d", lambda ch: jnp.where(blk, ch["a"], 0.0))
    put("a_o", lambda ch: ch["a"] - ch["a_d"])
    put("a2", lambda ch: _mm(ch["a_d"], ch["a_d"]))
    put("a4", lambda ch: _mm(ch["a2"], ch["a2"]))
    put("a8", lambda ch: _mm(ch["a4"], ch["a4"]))
    put("dinv", lambda ch: eye - ch["a_d"])
    put("dinv", lambda ch: ch["dinv"] + _mm(ch["dinv"], ch["a2"]))
    put("dinv", lambda ch: ch["dinv"] + _mm(ch["dinv"], ch["a4"]))
    put("dinv", lambda ch: ch["dinv"] + _mm(ch["dinv"], ch["a8"]))
    put("n1", lambda ch: _mm(ch["dinv"], ch["a_o"]))
    put("n2", lambda ch: _mm(ch["n1"], ch["n1"]))
    put("x", lambda ch: _mm(ch["dinv"], ch["rhs"]))
    put("x", lambda ch: ch["x"] + _mm(ch["n2"], ch["x"]))
    put("x", lambda ch: ch["x"] - _mm(ch["n1"], ch["x"]))
    put("qk", lambda ch: _mm_nt(ch["q"], ch["k"]) * ch["dec_incl"])
    put("s", lambda ch: s_s[ch["g"], ch["d"], ch["hd"]])
    put("v_new", lambda ch: ch["x"][:, :GDN_DV] - _mm(ch["x"][:, GDN_DV:], ch["s"]))
    put("o", lambda ch: _mm(ch["q"] * ch["eg"], ch["s"]) + _mm(ch["qk"], ch["v_new"]))
    put("s_new", lambda ch: ch["s"] * jnp.exp(ch["g_last"])
        + _mm_tn(ch["k"] * jnp.exp(ch["g_last"] - ch["gam"]), ch["v_new"]))
    for ch in chains:
        o_ref = of_ref if ch["d"] == 0 else ob_ref
        o_ref[ch["g"], :, ch["hd"] * GDN_DV:(ch["hd"] + 1) * GDN_DV] = ch["o"]
        s_s[ch["g"], ch["d"], ch["hd"]] = ch["s_new"]


def gdn_scan(qkvn, bg, geo):
    m = qkvn.shape[0]
    c = GDN_CHUNK
    nb, s_len, ctx = geo["batch"], geo["s_len"], geo["ctx"]
    gb = 2 if nb % 2 == 0 else 1
    n_c, ctx_c = s_len // c, ctx // c
    fwd = lambda b, j: (b, j, 0)
    bwd = lambda b, j: (b, _rev_tile(j, ctx_c, n_c), 0)
    w = GDN_HEADS * GDN_DV
    kern = functools.partial(_gdn_kernel, gb=gb)
    qkv3 = qkvn.reshape(nb, s_len, GDN_QKV)
    bg3 = bg.reshape(nb, s_len, 128)
    of, ob = pl.pallas_call(
        kern, grid=(nb // gb, n_c),
        in_specs=[pl.BlockSpec((gb, c, GDN_QKV), fwd), pl.BlockSpec((gb, c, 128), fwd),
                  pl.BlockSpec((gb, c, GDN_QKV), bwd), pl.BlockSpec((gb, c, 128), bwd)],
        out_specs=[pl.BlockSpec((gb, c, w), fwd), pl.BlockSpec((gb, c, w), bwd)],
        out_shape=[jax.ShapeDtypeStruct((nb, s_len, w), F32)] * 2,
        scratch_shapes=[pltpu.VMEM((gb, 2, GDN_HEADS, GDN_DK, GDN_DV), F32)],
        compiler_params=_cp(("arbitrary", "arbitrary")), name="gdn_scan",
    )(qkv3, bg3, qkv3, bg3)
    return of.reshape(m, w), ob.reshape(m, w)


def _fourier_kernel(c_ref, s_ref, u_ref, cc_ref, sc_ref, o_ref, *, scale):
    u = u_ref[0]
    a = jnp.dot(c_ref[...], u, preferred_element_type=F32)
    b = jnp.dot(s_ref[...], u, preferred_element_type=F32)
    o = _mm(a, cc_ref[...]) - _mm(b, sc_ref[...])
    o_ref[0] = o * scale


def _dft_mats(t):
    idx = jnp.arange(t, dtype=jnp.int32)
    ang = ((idx[:, None] * idx[None, :]) % t).astype(F32) * (2.0 * math.pi / t)
    return jnp.cos(ang).astype(BF16), jnp.sin(ang).astype(BF16)


def fourier(u, chan_c, chan_s):
    nb, t, w = u.shape
    cm, sm = _dft_mats(t)
    tf = _pick(t, (512, 256, 128))
    kern = functools.partial(_fourier_kernel, scale=1.0 / math.sqrt(t * FFT_GROUP))
    return pl.pallas_call(
        kern, grid=(t // tf, nb),
        in_specs=[pl.BlockSpec((tf, t), lambda f, b: (f, 0)), pl.BlockSpec((tf, t), lambda f, b: (f, 0)),
                  pl.BlockSpec((1, t, w), lambda f, b: (b, 0, 0)),
                  pl.BlockSpec((w, w), lambda f, b: (0, 0)), pl.BlockSpec((w, w), lambda f, b: (0, 0))],
        out_specs=pl.BlockSpec((1, tf, w), lambda f, b: (b, f, 0)),
        out_shape=jax.ShapeDtypeStruct((nb, t, w), F32),
        compiler_params=_cp(("arbitrary", "arbitrary")), name="fourier",
    )(cm, sm, u, chan_c, chan_s)


def _mla_prep_kernel(cq_ref, ckv_ref, misc_ref, cos_ref, sin_ref, gq_ref, gkv_ref, wq_ref, wkv_ref,
                     q_ref, k_ref, v_ref, *, q_scale):
    def rms(x, g):
        return x * lax.rsqrt(jnp.mean(x * x, axis=-1, keepdims=True) + NORM_EPS) * g

    cos = cos_ref[...]
    sin = sin_ref[...]
    qa = _mm(rms(cq_ref[...], gq_ref[...]), wq_ref[...])
    kva = _mm(rms(ckv_ref[...], gkv_ref[...]), wkv_ref[...])
    misc = misc_ref[...]
    k_rope = misc[:, :MLA_ROPE] * cos + misc[:, MLA_ROPE:] * sin
    zpad = jnp.zeros((cos.shape[0], MLA_QK_PAD - MLA_NOPE - MLA_ROPE), F32)
    for hd in range(MLA_HEADS):
        qh = qa[:, hd * 256:(hd + 1) * 256]
        q_rope = qh[:, MLA_NOPE:MLA_NOPE + MLA_ROPE] * cos + qh[:, MLA_NOPE + MLA_ROPE:] * sin
        q_ref[0, hd] = (jnp.concatenate([qh[:, :MLA_NOPE], q_rope, zpad], axis=1) * q_scale).astype(BF16)
        kh = kva[:, hd * 256:hd * 256 + MLA_NOPE]
        k_ref[0, hd] = jnp.concatenate([kh, k_rope, zpad], axis=1).astype(BF16)
        v_ref[0, hd] = kva[:, hd * 256 + MLA_NOPE:(hd + 1) * 256].astype(BF16)


def mla_prep(zin, cos_t, sin_t, p, geo):
    tt, tps, nb, s_len = geo["tt"], geo["tiles_per_seq"], geo["batch"], geo["s_len"]
    row = lambda b, j: b * tps + j
    full = lambda shape: pl.BlockSpec(shape, lambda b, j: (0,) * len(shape))
    hs = lambda w: pl.BlockSpec((1, MLA_HEADS, tt, w), lambda b, j: (b, 0, j, 0))
    kern = functools.partial(_mla_prep_kernel, q_scale=(MLA_NOPE + MLA_ROPE) ** -0.5)
    return pl.pallas_call(
        kern, grid=(nb, tps),
        in_specs=[pl.BlockSpec((tt, MLA_Q_RANK), lambda b, j: (row(b, j), COL_CQ // MLA_Q_RANK)),
                  pl.BlockSpec((tt, MLA_KV_RANK), lambda b, j: (row(b, j), COL_CKV // MLA_KV_RANK)),
                  pl.BlockSpec((tt, 128), lambda b, j: (row(b, j), COL_MISC // 128)),
                  pl.BlockSpec((tt, MLA_ROPE), lambda b, j: (j, 0)), pl.BlockSpec((tt, MLA_ROPE), lambda b, j: (j, 0)),
                  full((1, MLA_Q_RANK)), full((1, MLA_KV_RANK)),
                  full((MLA_Q_RANK, MLA_HEADS * 256)), full((MLA_KV_RANK, MLA_HEADS * 256))],
        out_specs=[hs(MLA_QK_PAD), hs(MLA_QK_PAD), hs(MLA_V)],
        out_shape=[jax.ShapeDtypeStruct((nb, MLA_HEADS, s_len, MLA_QK_PAD), BF16),
                   jax.ShapeDtypeStruct((nb, MLA_HEADS, s_len, MLA_QK_PAD), BF16),
                   jax.ShapeDtypeStruct((nb, MLA_HEADS, s_len, MLA_V), BF16)],
        compiler_params=_cp(("arbitrary", "arbitrary")), name="mla_prep",
    )(zin, zin, zin, cos_t, sin_t, p["mla_q_norm_g"].reshape(1, -1), p["mla_kv_norm_g"].reshape(1, -1),
      p["mla_wq"], p["mla_wkv"])


def _attn_kernel(q_ref, k_ref, v_ref, o_ref, *, ctx, ctx_t, s_len):
    qi = pl.program_id(2)
    q = q_ref[0, 0]

    def attend(n_keys):
        s = lax.dot_general(q, k_ref[0, 0, 0:n_keys, :], (((1,), (1,)), ((), ())), preferred_element_type=F32)
        pr = jnp.exp(s - jnp.max(s, axis=-1, keepdims=True))
        den = jnp.sum(pr, axis=-1, keepdims=True)
        o_ref[0] = jnp.dot(pr.astype(BF16), v_ref[0, 0, 0:n_keys, :], preferred_element_type=F32) / den

    @pl.when(qi < ctx_t)
    def _():
        attend(ctx)

    @pl.when(qi >= ctx_t)
    def _():
        attend(s_len)


def mla_attention(q, k, v, geo):
    nb, nh, s_len, _ = q.shape
    tt, ctx_t = geo["tt"], geo["ctx_tiles"]
    kern = functools.partial(_attn_kernel, ctx=geo["ctx"], ctx_t=ctx_t, s_len=s_len)
    return pl.pallas_call(
        kern, grid=(nb, nh, s_len // tt),
        in_specs=[pl.BlockSpec((1, 1, tt, MLA_QK_PAD), lambda b, h, i: (b, h, i, 0)),
                  pl.BlockSpec((1, 1, s_len, MLA_QK_PAD), lambda b, h, i: (b, h, 0, 0)),
                  pl.BlockSpec((1, 1, s_len, MLA_V), lambda b, h, i: (b, h, 0, 0))],
        out_specs=pl.BlockSpec((1, tt, MLA_V), lambda b, h, i: (b, i, h)),
        out_shape=jax.ShapeDtypeStruct((nb, s_len, nh * MLA_V), F32),
        compiler_params=_cp(("arbitrary", "arbitrary", "arbitrary")), name="mla_attention",
    )(q, k, v)


def _merge_kernel(hn_ref, hf_ref, hb_ref, lg_ref, of_ref, ob_ref, z_ref, gg_ref, yc_ref, yd_ref, wg_ref, wb_ref,
                  wo_ref, o_ref, y_s, acc_s, *, n_j):
    j = pl.program_id(1)

    @pl.when(j == 0)
    def _():
        lg = lg_ref[...]
        gelu = 0.5 * lg * (1.0 + jnp.tanh(math.sqrt(2.0 / math.pi) * (lg + 0.044715 * (lg * lg * lg))))
        y_s[0] = (gelu * (hf_ref[...] + hb_ref[...])).astype(BF16)
        for hd in range(GDN_HEADS):
            sl = slice(hd * GDN_DV, (hd + 1) * GDN_DV)
            o = of_ref[:, sl] + ob_ref[:, sl]
            o = o * lax.rsqrt(jnp.mean(o * o, axis=-1, keepdims=True) + NORM_EPS) * gg_ref[...]
            z = z_ref[:, sl]
            y_s[1, :, sl] = (o * (z * jax.nn.sigmoid(z))).astype(BF16)
        y_s[2] = yc_ref[...].astype(BF16)
        y_s[3] = yd_ref[...].astype(BF16)
        acc_s[...] = jnp.zeros_like(acc_s)

    hn = hn_ref[...]
    merged = None
    for i in range(N_BRANCH):
        gate = jax.nn.sigmoid(jnp.dot(hn, wg_ref[i], preferred_element_type=F32))
        term = gate * jnp.dot(y_s[i], wb_ref[i], preferred_element_type=F32)
        merged = term if merged is None else merged + term
    acc_s[...] += jnp.dot(merged.astype(BF16), wo_ref[...], preferred_element_type=F32)

    @pl.when(j == n_j - 1)
    def _():
        o_ref[...] = acc_s[...]


def merge(hn, zin, hf, hb, of, ob, yc, yd, p):
    m, d = hn.shape
    tm = _pick(m, (512, 256, 128))
    tn = 256
    n_j = d // tn
    w = BRANCH_WIDTH
    tok = lambda a_w, cb: pl.BlockSpec((tm, a_w), lambda i, j: (i, cb))
    kern = functools.partial(_merge_kernel, n_j=n_j)
    return pl.pallas_call(
        kern, grid=(m // tm, n_j),
        in_specs=[tok(d, 0), tok(w, 0), tok(w, 0), tok(w, COL_LG // w), tok(w, 0), tok(w, 0), tok(w, COL_Z // w),
                  pl.BlockSpec((1, GDN_DV), lambda i, j: (0, 0)), tok(w, 0), tok(w, 0),
                  pl.BlockSpec((N_BRANCH, d, tn), lambda i, j: (0, 0, j)),
                  pl.BlockSpec((N_BRANCH, w, tn), lambda i, j: (0, 0, j)),
                  pl.BlockSpec((tn, d), lambda i, j: (j, 0))],
        out_specs=pl.BlockSpec((tm, d), lambda i, j: (i, 0)),
        out_shape=jax.ShapeDtypeStruct((m, d), F32),
        scratch_shapes=[pltpu.VMEM((N_BRANCH, tm, w), BF16), pltpu.VMEM((tm, d), F32)],
        compiler_params=_cp(("arbitrary", "arbitrary")), name="merge",
    )(hn, hf, hb, zin, of, ob, zin, p["gdn_norm_g"].reshape(1, -1), yc, yd, p["w_gate"], p["w_branch"], p["w_out"])


def _moe_kernel(be_ref, nv_ref, x_ref, w1_ref, b1_ref, w2_ref, b2_ref, o_ref, w1_s, w2_s):
    i = pl.program_id(0)
    new_expert = jnp.logical_or(i == 0, be_ref[i] != be_ref[jnp.maximum(i - 1, 0)])

    @pl.when(new_expert)
    def _():
        w1_s[...] = w1_ref[0, 0].astype(BF16)
        w2_s[...] = w2_ref[0, 0].astype(BF16)

    @pl.when(i < nv_ref[0])
    def _():
        h = jnp.dot(x_ref[...], w1_s[...], preferred_element_type=F32) + b1_ref[0, 0]
        h_glu = jnp.minimum(h[:, :D_EXPERT], SWIGLU_LIMIT)
        h_lin = jnp.clip(h[:, D_EXPERT:], -SWIGLU_LIMIT, SWIGLU_LIMIT)
        act = h_glu * jax.nn.sigmoid(SWIGLU_ALPHA * h_glu) * (h_lin + 1.0)
        y = jnp.dot(act.astype(BF16), w2_s[...], preferred_element_type=F32) + b2_ref[0, 0]
        o_ref[...] = y.astype(o_ref.dtype)

    @pl.when(i >= nv_ref[0])
    def _():
        o_ref[...] = jnp.zeros_like(o_ref)


def moe_experts(xg, block_e, n_valid, layer, w1, b1, w2, b2):
    p_rows, d = xg.shape
    bm = MOE_ROWS
    n_blocks = p_rows // bm
    depth, ne, _, dh = w1.shape
    grid_spec = pltpu.PrefetchScalarGridSpec(
        num_scalar_prefetch=2, grid=(n_blocks,),
        in_specs=[pl.BlockSpec((bm, d), lambda i, be, nv: (i, 0)),
                  pl.BlockSpec((1, 1, d, dh), lambda i, be, nv: (layer, be[i], 0, 0)),
                  pl.BlockSpec((1, 1, 1, dh), lambda i, be, nv: (layer, be[i], 0, 0)),
                  pl.BlockSpec((1, 1, D_EXPERT, d), lambda i, be, nv: (layer, be[i], 0, 0)),
                  pl.BlockSpec((1, 1, 1, d), lambda i, be, nv: (layer, be[i], 0, 0))],
        out_specs=pl.BlockSpec((bm, d), lambda i, be, nv: (i, 0)),
        scratch_shapes=[pltpu.VMEM((d, dh), BF16), pltpu.VMEM((D_EXPERT, d), BF16)])
    return pl.pallas_call(
        _moe_kernel, grid_spec=grid_spec, out_shape=jax.ShapeDtypeStruct((p_rows, d), BF16),
        compiler_params=_cp(("arbitrary",)), name="moe_experts",
    )(block_e, n_valid, xg, w1, b1.reshape(depth, ne, 1, dh), w2, b2.reshape(depth, ne, 1, d))


def moe_ffn(hn, idx, rank, counts, layer, w1, b1, w2, b2):
    n, d = hn.shape
    bm = MOE_ROWS
    nk = n * TOP_K
    counts = counts.astype(jnp.int32)
    padded = (counts + bm - 1) // bm * bm
    pend = jnp.cumsum(padded)
    pstart = pend - padded
    experts = jnp.arange(N_EXPERTS, dtype=jnp.int32)
    pos = rank + jnp.sum(jnp.where(idx[..., None] == experts, pstart, 0), axis=-1)
    n_blocks = (nk + N_EXPERTS * (bm - 1) + bm - 1) // bm
    p_rows = n_blocks * bm
    real = (idx.reshape(nk) << KEY_EXPERT_SHIFT) + jnp.arange(nk, dtype=jnp.int32)
    fill = jnp.arange(bm - 1, dtype=jnp.int32)
    pad_keys = jnp.where(fill[None, :] < (padded - counts)[:, None],
                         (experts[:, None] << KEY_EXPERT_SHIFT) + (1 << KEY_PAD_BIT) + fill[None, :],
                         jnp.iinfo(jnp.int32).max)
    tail = jnp.full((p_rows - nk - N_EXPERTS * (bm - 1),), jnp.iinfo(jnp.int32).max, jnp.int32)
    keys = lax.sort(jnp.concatenate([real, pad_keys.reshape(-1), tail]))
    is_pad = (keys >> KEY_PAD_BIT) & 1
    buf_tok = jnp.where(is_pad == 1, 0, (keys & ((1 << KEY_PAD_BIT) - 1)) // TOP_K)
    block_e = jnp.minimum(jnp.searchsorted(pend, jnp.arange(n_blocks, dtype=jnp.int32) * bm, side='right'),
                          N_EXPERTS - 1).astype(jnp.int32)
    n_valid = (pend[-1:] // bm).astype(jnp.int32)
    y = moe_experts(hn[buf_tok], block_e, n_valid, layer, w1, b1, w2, b2)
    return y[pos.reshape(nk)].reshape(n, TOP_K * d)


def _rope_rot(w):
    h = MLA_ROPE // 4
    a, b, c, d = w[..., :h], w[..., h:2 * h], w[..., 2 * h:3 * h], w[..., 3 * h:]
    return jnp.concatenate([-b, a, -d, c], axis=-1)


def _prep_layer(w_in, mla_w_uq, mla_w_ukv):
    d = w_in.shape[0]
    o = np.cumsum([0, 512, 512, GDN_QKV, 512, 8, 8, 512, 512, 256, 64, N_BRANCH * d])
    lu, lg, qkv, z, be, al, fu, cq, ckv, kr, mg = [w_in[:, o[i]:o[i + 1]] for i in range(11)]
    pad = jnp.zeros((d, 128 - 16), w_in.dtype)
    w_main = jnp.concatenate([qkv, lu, lg, z, fu, cq, ckv, kr, _rope_rot(kr), be, al, pad], axis=1).astype(BF16)
    w_gate = jnp.transpose(mg.reshape(d, N_BRANCH, d), (1, 0, 2)).astype(BF16)
    uq = mla_w_uq.reshape(MLA_Q_RANK, MLA_HEADS, MLA_NOPE + MLA_ROPE)
    wq = jnp.concatenate([uq, _rope_rot(uq[..., MLA_NOPE:])], axis=-1).reshape(MLA_Q_RANK, MLA_HEADS * 256)
    return w_main, w_gate, wq.astype(BF16), mla_w_ukv.astype(BF16)


def _rope_tables(ctx, n_lat):
    half = MLA_ROPE // 4
    inv = jnp.power(ROPE_BASE, -jnp.arange(half, dtype=F32) / half)
    t = jnp.arange(n_lat)
    row = (t // GRID_W).astype(F32)[:, None] * inv
    col = (t % GRID_W).astype(F32)[:, None] * inv
    cos = jnp.concatenate([jnp.cos(row), jnp.cos(row), jnp.cos(col), jnp.cos(col)], axis=1)
    sin = jnp.concatenate([jnp.sin(row), jnp.sin(row), jnp.sin(col), jnp.sin(col)], axis=1)
    cos = jnp.concatenate([jnp.ones((ctx, MLA_ROPE), F32), cos], axis=0)
    sin = jnp.concatenate([jnp.zeros((ctx, MLA_ROPE), F32), sin], axis=0)
    return cos, sin


def _chan_dft():
    idx = np.arange(FFT_GROUP)
    ang = 2.0 * np.pi * ((idx[:, None] * idx[None, :]) % FFT_GROUP) / FFT_GROUP
    eye = np.eye(FFT_GROUPS)
    return (jnp.asarray(np.kron(eye, np.cos(ang)), BF16), jnp.asarray(np.kron(eye, np.sin(ang)), BF16))


def kernel(x, c, ctx, c_ctx, w_ada, b_ada, norm1_g, norm2_g, w_in, lru_conv_w, lru_conv_b, lru_w_a, lru_b_a, lru_w_x, lru_b_x, lru_lambda, gdn_conv_w, gdn_conv_b, gdn_a_log, gdn_dt_bias, gdn_norm_g, mla_q_norm_g, mla_kv_norm_g, mla_w_uq, mla_w_ukv, w_branch, w_out, router_w, router_b, exp_w1, exp_b1, exp_w2, exp_b2, final_norm_g):
    nb, n_lat, d = x.shape
    n_ctx = ctx.shape[1]
    depth = w_ada.shape[0]
    s_len = n_ctx + n_lat
    m = nb * s_len
    tt = _pick(math.gcd(n_ctx, n_lat), (256, 128, 64))
    geo = dict(tt=tt, tiles_per_seq=s_len // tt, ctx_tiles=n_ctx // tt, batch=nb, s_len=s_len, ctx=n_ctx)

    mod_rows = 8 * ((nb + 1 + 7) // 8)
    cmat = jnp.zeros((mod_rows, d), F32).at[:nb].set(c).at[nb].set(c_ctx)
    mod_all = ada_table(cmat, w_ada, b_ada).reshape(depth, mod_rows * N_MOD, 1, d)

    cos_t, sin_t = _rope_tables(n_ctx, n_lat)
    chan_c, chan_s = _chan_dft()
    h = jnp.concatenate([ctx, x], axis=1).reshape(m, d)

    delta, prev_mod, gate_k = None, None, None
    for l in range(depth):
        w_main, w_gate, wq, wkv = _prep_layer(w_in[l], mla_w_uq[l], mla_w_ukv[l])
        p = dict(lru_conv_w=lru_conv_w[l], lru_conv_b=lru_conv_b[l], lru_w_a=lru_w_a[l].astype(BF16),
                 lru_b_a=lru_b_a[l], lru_w_x=lru_w_x[l].astype(BF16), lru_b_x=lru_b_x[l], lru_lambda=lru_lambda[l],
                 gdn_conv_w=gdn_conv_w[l], gdn_conv_b=gdn_conv_b[l], gdn_a_log=gdn_a_log[l],
                 gdn_dt_bias=gdn_dt_bias[l], gdn_norm_g=gdn_norm_g[l], mla_q_norm_g=mla_q_norm_g[l],
                 mla_kv_norm_g=mla_kv_norm_g[l], mla_wq=wq, mla_wkv=wkv, w_gate=w_gate,
                 w_branch=w_branch[l].astype(BF16), w_out=w_out[l].astype(BF16))
        mod = mod_all[l]
        h, hn = norm_mod(h, norm1_g[l], geo, mod=mod, shift_k=0, scale_k=1, delta=delta, gate_k=gate_k,
                         prev_mod=prev_mod)
        zin = matmul(hn, w_main)
        hf, hb = lru_scan(zin, p, geo)
        qkvn, bg = gdn_prep(zin, p, geo)
        of, ob = gdn_scan(qkvn, bg, geo)
        fu = zin[:, COL_FU:COL_FU + FFT_WIDTH].astype(BF16).reshape(nb, s_len, FFT_WIDTH)
        yc = jnp.concatenate([fourier(fu[:, :n_ctx], chan_c, chan_s), fourier(fu[:, n_ctx:], chan_c, chan_s)],
                             axis=1).reshape(m, FFT_WIDTH)
        q, k, v = mla_prep(zin, cos_t, sin_t, p, geo)
        yd = mla_attention(q, k, v, geo).reshape(m, MLA_HEADS * MLA_V)
        mixed = merge(hn, zin, hf, hb, of, ob, yc, yd, p)
        h, hn2, idx, gates, rank, counts = norm_mod(
            h, norm2_g[l], geo, mod=mod, shift_k=3, scale_k=4, delta=("dense", mixed), gate_k=2, prev_mod=mod,
            router=(router_w[l], router_b[l]))
        ygath = moe_ffn(hn2, idx, rank, counts[0], l, exp_w1, exp_b1, exp_w2, exp_b2)
        delta, prev_mod, gate_k = ("moe", ygath, gates), mod, 5
    (out,) = norm_mod(h, final_norm_g, geo, delta=delta, gate_k=5, prev_mod=prev_mod, emit_h=False,
                      out_dtype=F32, lat_only=True)
    return out.reshape(nb, n_lat, d)
```

```python
import functools
import math

import jax
import jax.numpy as jnp
import numpy as np
from jax import lax
from jax.experimental import pallas as pl
from jax.experimental.pallas import tpu as pltpu

F32 = jnp.float32
BF16 = jnp.bfloat16

GRID_W = 64
NORM_EPS = 1e-6
N_MOD = 6
LRU_WIDTH = 512
LRU_BLOCKS = 4
LRU_BLOCK = 128
LRU_C = 8.0
GDN_HEADS = 4
GDN_DK = 128
GDN_DV = 128
GDN_CHUNK = 64
GDN_QKV = GDN_HEADS * (2 * GDN_DK + GDN_DV)
FFT_GROUPS = 4
FFT_GROUP = 128
FFT_WIDTH = 512
MLA_HEADS = 4
MLA_Q_RANK = 512
MLA_KV_RANK = 256
MLA_NOPE = 128
MLA_ROPE = 64
MLA_V = 128
MLA_QK_PAD = 256
ROPE_BASE = 10000.0
N_BRANCH = 4
BRANCH_WIDTH = 512
N_EXPERTS = 32
TOP_K = 4
D_EXPERT = 512
SWIGLU_LIMIT = 7.0
SWIGLU_ALPHA = 1.702
MOE_ROWS = 512
KEY_EXPERT_SHIFT = 20
KEY_PAD_BIT = 19

COL_QKV = 0
COL_LU = 1536
COL_LG = 2048
COL_Z = 2560
COL_FU = 3072
COL_CQ = 3584
COL_CKV = 4096
COL_MISC = 4352
COL_BA = 4480
ZIN_W = 4608

VMEM_LIMIT = 56 * 1024 * 1024
HALO = 8


def _cp(sem, vmem=VMEM_LIMIT):
    return pltpu.CompilerParams(dimension_semantics=sem, vmem_limit_bytes=vmem)


def _pick(n, cands):
    for c in cands:
        if n % c == 0:
            return c
    raise ValueError(f"no tile for {n} in {cands}")


def _mm(a, b):
    return jnp.dot(a.astype(BF16), b.astype(BF16), preferred_element_type=F32)


def _mm_nt(a, b):
    return lax.dot_general(a.astype(BF16), b.astype(BF16), (((1,), (1,)), ((), ())), preferred_element_type=F32)


def _mm_tn(a, b):
    return lax.dot_general(a.astype(BF16), b.astype(BF16), (((0,), (0,)), ((), ())), preferred_element_type=F32)


def _mm_f32(a, b):
    return jnp.dot(a, b, preferred_element_type=F32, precision=lax.Precision.HIGHEST)


def _softplus(y):
    return jnp.maximum(y, 0.0) + jnp.log1p(jnp.exp(-jnp.abs(y)))


def _ada_kernel(c_ref, w_ref, b_ref, o_ref):
    cv = c_ref[...]
    s = cv * jax.nn.sigmoid(cv)
    o_ref[0] = _mm(s, w_ref[0]) + b_ref[0]


def ada_table(cmat, w_ada, b_ada):
    depth, d, n = w_ada.shape
    rows = cmat.shape[0]
    tn = _pick(n, (1024, 512, 256, 128))
    return pl.pallas_call(
        _ada_kernel,
        grid=(depth, n // tn),
        in_specs=[pl.BlockSpec((rows, d), lambda l, j: (0, 0)),
                  pl.BlockSpec((1, d, tn), lambda l, j: (l, 0, j)),
                  pl.BlockSpec((1, 1, tn), lambda l, j: (l, 0, j))],
        out_specs=pl.BlockSpec((1, rows, tn), lambda l, j: (l, 0, j)),
        out_shape=jax.ShapeDtypeStruct((depth, rows, n), F32),
        compiler_params=_cp(("arbitrary", "arbitrary")),
        name="ada_table",
    )(cmat, w_ada, b_ada.reshape(depth, 1, n))


def _norm_kernel(*refs, delta_kind, has_mod, has_router, emit_h):
    it = iter(refs)
    h_ref = next(it)
    if delta_kind == "dense":
        d_ref = next(it)
        gate_ref = next(it)
    elif delta_kind == "moe":
        yg_refs = [next(it) for _ in range(TOP_K)]
        mg_ref = next(it)
        gate_ref = next(it)
    g_ref = next(it)
    if has_mod:
        shift_ref = next(it)
        scale_ref = next(it)
    if has_router:
        rw_ref = next(it)
        rb_ref = next(it)
    if emit_h:
        ho_ref = next(it)
    y_ref = next(it)
    if has_router:
        idx_ref = next(it)
        gt_ref = next(it)
        rank_ref = next(it)
        cnt_ref = next(it)

    h = h_ref[...]
    d = h.shape[1]
    if delta_kind == "dense":
        h = h + gate_ref[0] * d_ref[...]
    elif delta_kind == "moe":
        mg = mg_ref[...]
        f = mg[:, 0:1] * yg_refs[0][...].astype(F32)
        for kk in range(1, TOP_K):
            f = f + mg[:, kk:kk + 1] * yg_refs[kk][...].astype(F32)
        h = h + gate_ref[0] * f
    if emit_h:
        ho_ref[...] = h
    y = h * lax.rsqrt(jnp.mean(h * h, axis=-1, keepdims=True) + NORM_EPS) * g_ref[...]
    if has_mod:
        y = y * (1.0 + scale_ref[0]) + shift_ref[0]
    y_ref[...] = y.astype(y_ref.dtype)
    if has_router:
        @pl.when(pl.program_id(0) == 0)
        def _():
            cnt_ref[...] = jnp.zeros_like(cnt_ref)

        logit = _mm_f32(y, rw_ref[...]) + rb_ref[...]
        tt, ne = logit.shape
        lane = lax.broadcasted_iota(jnp.int32, (tt, ne), 1).astype(F32)
        vals, hots = [], []
        for kk in range(TOP_K):
            top = jnp.max(logit, axis=-1, keepdims=True)
            arg = jnp.min(jnp.where(logit == top, lane, float(ne)), axis=-1, keepdims=True)
            hot = lane == arg
            idx_ref[:, kk:kk + 1] = arg.astype(jnp.int32)
            vals.append(top)
            hots.append(hot)
            logit = jnp.where(hot, -jnp.inf, logit)
        exps = [jnp.exp(v - vals[0]) for v in vals]
        den = exps[0] + exps[1] + exps[2] + exps[3]
        for kk in range(TOP_K):
            gt_ref[:, kk:kk + 1] = exps[kk] / den
        hot_all = hots[0].astype(F32) + hots[1].astype(F32) + hots[2].astype(F32) + hots[3].astype(F32)
        ii = lax.broadcasted_iota(jnp.int32, (tt, tt), 0)
        jj = lax.broadcasted_iota(jnp.int32, (tt, tt), 1)
        before = _mm((jj < ii).astype(F32), hot_all) + cnt_ref[...]
        for kk in range(TOP_K):
            rank_ref[:, kk:kk + 1] = jnp.sum(jnp.where(hots[kk], before, 0.0), axis=-1, keepdims=True).astype(jnp.int32)
        cnt_ref[...] += jnp.sum(hot_all, axis=0, keepdims=True)


def norm_mod(h, g, geo, *, mod=None, shift_k=None, scale_k=None, delta=None, gate_k=None, prev_mod=None,
             router=None, emit_h=True, out_dtype=None, lat_only=False):
    m, d = h.shape
    out_dtype = BF16 if out_dtype is None else out_dtype
    tt, tps, ctx_t, nb = geo["tt"], geo["tiles_per_seq"], geo["ctx_tiles"], geo["batch"]
    if lat_only:
        lat_t = tps - ctx_t
        grid = (nb * lat_t,)
        in_row = lambda i: (i // lat_t) * tps + ctx_t + i % lat_t
        mod_row = lambda i: i // lat_t
        out_rows = nb * lat_t * tt
    else:
        grid = (m // tt,)
        in_row = lambda i: i
        mod_row = lambda i: jnp.where(i % tps < ctx_t, nb, i // tps)
        out_rows = m
    tok = lambda w: pl.BlockSpec((tt, w), lambda i: (in_row(i), 0))
    out_tok = lambda w: pl.BlockSpec((tt, w), lambda i: (i, 0))

    def mod_spec(k):
        return pl.BlockSpec((1, 1, d), lambda i: (mod_row(i) * N_MOD + k, 0, 0))

    args, specs = [h], [tok(d)]
    delta_kind = None if delta is None else delta[0]
    if delta_kind == "dense":
        args += [delta[1], prev_mod]
        specs += [tok(d), mod_spec(gate_k)]
    elif delta_kind == "moe":
        args += [delta[1]] * TOP_K + [delta[2], prev_mod]
        specs += [pl.BlockSpec((tt, d), lambda i, kk=kk: (kk * (m // tt) + in_row(i), 0)) for kk in range(TOP_K)]
        specs += [tok(TOP_K), mod_spec(gate_k)]
    args.append(g.reshape(1, d))
    specs.append(pl.BlockSpec((1, d), lambda i: (0, 0)))
    if mod is not None:
        args += [mod, mod]
        specs += [mod_spec(shift_k), mod_spec(scale_k)]
    if router is not None:
        rw, rb = router
        ne = rw.shape[1]
        args += [rw, rb.reshape(1, ne)]
        specs += [pl.BlockSpec(rw.shape, lambda i: (0, 0)), pl.BlockSpec((1, ne), lambda i: (0, 0))]
    out_shape, out_specs = [], []
    if emit_h:
        out_shape.append(jax.ShapeDtypeStruct((out_rows, d), F32))
        out_specs.append(out_tok(d))
    out_shape.append(jax.ShapeDtypeStruct((out_rows, d), out_dtype))
    out_specs.append(out_tok(d))
    if router is not None:
        out_shape += [jax.ShapeDtypeStruct((out_rows, TOP_K), jnp.int32), jax.ShapeDtypeStruct((out_rows, TOP_K), F32),
                      jax.ShapeDtypeStruct((out_rows, TOP_K), jnp.int32), jax.ShapeDtypeStruct((1, ne), F32)]
        out_specs += [out_tok(TOP_K), out_tok(TOP_K), out_tok(TOP_K), pl.BlockSpec((1, ne), lambda i: (0, 0))]
    kern = functools.partial(_norm_kernel, delta_kind=delta_kind, has_mod=mod is not None,
                             has_router=router is not None, emit_h=emit_h)
    return pl.pallas_call(kern, grid=grid, in_specs=specs, out_specs=out_specs, out_shape=out_shape,
                          compiler_params=_cp(("arbitrary",)), name="norm_mod")(*args)


def _matmul_kernel(x_ref, w_ref, o_ref):
    o_ref[...] = jnp.dot(x_ref[...], w_ref[...], preferred_element_type=F32).astype(o_ref.dtype)


def matmul(x, w, out_dtype=F32):
    m, k = x.shape
    n = w.shape[1]
    tm = _pick(m, (1024, 512, 256, 128))
    tn = _pick(n, (768, 512, 256, 128))
    return pl.pallas_call(
        _matmul_kernel,
        grid=(m // tm, n // tn),
        in_specs=[pl.BlockSpec((tm, k), lambda i, j: (i, 0)), pl.BlockSpec((k, tn), lambda i, j: (0, j))],
        out_specs=pl.BlockSpec((tm, tn), lambda i, j: (i, j)),
        out_shape=jax.ShapeDtypeStruct((m, n), out_dtype),
        compiler_params=_cp(("arbitrary", "arbitrary")),
        name="in_proj",
    )(x, w)


def _conv4(x, xp, xn, prev_ok, next_ok, cw_ref, cb_ref, ext_ref):
    tt = x.shape[0]
    ext_ref[0:HALO, :] = xp * prev_ok
    ext_ref[HALO:HALO + tt, :] = x
    ext_ref[HALO + tt:2 * HALO + tt, :] = xn * next_ok
    y = cb_ref[...] + cw_ref[2:3, :] * x
    y = y + cw_ref[0:1, :] * ext_ref[HALO - 2:HALO - 2 + tt, :]
    y = y + cw_ref[1:2, :] * ext_ref[HALO - 1:HALO - 1 + tt, :]
    y = y + cw_ref[3:4, :] * ext_ref[HALO + 1:HALO + 1 + tt, :]
    return y


def _seg_edges(tile, ctx_t, n_t):
    first = jnp.logical_or(tile == 0, tile == ctx_t)
    last = jnp.logical_or(tile == ctx_t - 1, tile == n_t - 1)
    return jnp.where(first, 0.0, 1.0).astype(F32), jnp.where(last, 0.0, 1.0).astype(F32)


def _rev_tile(j, ctx_t, n_t):
    return jnp.where(j < ctx_t, ctx_t - 1 - j, n_t - 1 - (j - ctx_t))


def _halo_specs(width, col_block, tile_of, tt, tps, m):
    r8 = tt // HALO
    last8 = m // HALO - 1

    def cur(b, j):
        return (b * tps + tile_of(j), col_block)

    def prev(b, j):
        return (jnp.maximum((b * tps + tile_of(j)) * r8 - 1, 0), col_block)

    def nxt(b, j):
        return (jnp.minimum((b * tps + tile_of(j) + 1) * r8, last8), col_block)

    return [pl.BlockSpec((tt, width), cur), pl.BlockSpec((HALO, width), prev), pl.BlockSpec((HALO, width), nxt)]


def _lru_kernel(xf_ref, xfp_ref, xfn_ref, xb_ref, xbp_ref, xbn_ref, cw_ref, cb_ref, wa_ref, ba_ref, wx_ref, bx_ref,
                lam_ref, hf_ref, hb_ref, ext_s, a_s, b_s, carry_s, *, tt, ctx_t, n_t):
    j = pl.program_id(1)

    @pl.when(j == 0)
    def _():
        carry_s[...] = jnp.zeros_like(carry_s)

    rows = lax.broadcasted_iota(jnp.int32, (HALO, LRU_WIDTH), 0)
    for d, (x_ref, xp_ref, xn_ref, out_ref) in enumerate(((xf_ref, xfp_ref, xfn_ref, hf_ref),
                                                          (xb_ref, xbp_ref, xbn_ref, hb_ref))):
        tile = j if d == 0 else _rev_tile(j, ctx_t, n_t)
        prev_ok, next_ok = _seg_edges(tile, ctx_t, n_t)
        xc = _conv4(x_ref[...], xp_ref[...], xn_ref[...], prev_ok, next_ok, cw_ref, cb_ref, ext_s)
        sp = _softplus(-lam_ref[d:d + 1, :])
        for g in range(LRU_BLOCKS):
            sl = slice(g * LRU_BLOCK, (g + 1) * LRU_BLOCK)
            xg = xc[:, sl]
            r = jax.nn.sigmoid(_mm(xg, wa_ref[d, g]) + ba_ref[d:d + 1, sl])
            ig = jax.nn.sigmoid(_mm(xg, wx_ref[d, g]) + bx_ref[d:d + 1, sl])
            log_a = -LRU_C * r * sp[:, sl]
            a = jnp.exp(log_a)
            a_s[d, :, sl] = a
            b_s[d, :, sl] = jnp.sqrt(-jnp.tanh(log_a) * (a * a + 1.0)) * (ig * xg)

        n_grp = tt // HALO

        def group(gi, carry, d=d, out_ref=out_ref):
            g0 = gi if d == 0 else n_grp - 1 - gi
            off = pl.multiple_of(g0 * HALO, HALO)
            av = a_s[d, pl.ds(off, HALO), :]
            bv = b_s[d, pl.ds(off, HALO), :]
            for s in (1, 2, 4):
                shift = s if d == 0 else HALO - s
                a_sh = pltpu.roll(av, shift, 0)
                b_sh = pltpu.roll(bv, shift, 0)
                ok = (rows >= s) if d == 0 else (rows < HALO - s)
                bv = jnp.where(ok, av * b_sh + bv, bv)
                av = jnp.where(ok, av * a_sh, av)
            hv = bv + av * carry
            out_ref[pl.ds(off, HALO), :] = hv
            return hv[HALO - 1:HALO, :] if d == 0 else hv[0:1, :]

        carry_s[d:d + 1, :] = lax.fori_loop(0, n_grp, group, carry_s[d:d + 1, :])


def lru_scan(zin, p, geo):
    m = zin.shape[0]
    tt, tps, ctx_t, nb = geo["tt"], geo["tiles_per_seq"], geo["ctx_tiles"], geo["batch"]
    cb = COL_LU // LRU_WIDTH
    fwd = _halo_specs(LRU_WIDTH, cb, lambda j: j, tt, tps, m)
    bwd = _halo_specs(LRU_WIDTH, cb, lambda j: _rev_tile(j, ctx_t, tps), tt, tps, m)
    full = lambda shape: pl.BlockSpec(shape, lambda b, j: (0,) * len(shape))
    w_specs = [full((4, LRU_WIDTH)), full((1, LRU_WIDTH)), full((2, LRU_BLOCKS, LRU_BLOCK, LRU_BLOCK)),
               full((2, LRU_WIDTH)), full((2, LRU_BLOCKS, LRU_BLOCK, LRU_BLOCK)), full((2, LRU_WIDTH)),
               full((2, LRU_WIDTH))]
    out_f = pl.BlockSpec((tt, LRU_WIDTH), lambda b, j: (b * tps + j, 0))
    out_b = pl.BlockSpec((tt, LRU_WIDTH), lambda b, j: (b * tps + _rev_tile(j, ctx_t, tps), 0))
    kern = functools.partial(_lru_kernel, tt=tt, ctx_t=ctx_t, n_t=tps)
    return pl.pallas_call(
        kern, grid=(nb, tps), in_specs=fwd + bwd + w_specs, out_specs=[out_f, out_b],
        out_shape=[jax.ShapeDtypeStruct((m, LRU_WIDTH), F32)] * 2,
        scratch_shapes=[pltpu.VMEM((tt + 2 * HALO, LRU_WIDTH), F32), pltpu.VMEM((2, tt, LRU_WIDTH), F32),
                        pltpu.VMEM((2, tt, LRU_WIDTH), F32), pltpu.VMEM((2, LRU_WIDTH), F32)],
        compiler_params=_cp(("arbitrary", "arbitrary")), name="lru_scan",
    )(zin, zin, zin, zin, zin, zin, p["lru_conv_w"], p["lru_conv_b"].reshape(1, -1), p["lru_w_a"], p["lru_b_a"],
      p["lru_w_x"], p["lru_b_x"], p["lru_lambda"])


def _gdn_prep_kernel(x_ref, xp_ref, xn_ref, ba_ref, cw_ref, cb_ref, nega_ref, dtb_ref, qkv_ref, bg_ref, ext_s,
                     *, ctx_t, n_t):
    tile = pl.program_id(1)
    prev_ok, next_ok = _seg_edges(tile, ctx_t, n_t)
    xc = _conv4(x_ref[...], xp_ref[...], xn_ref[...], prev_ok, next_ok, cw_ref, cb_ref, ext_s)
    xc = xc * jax.nn.sigmoid(xc)
    nq = GDN_HEADS * GDN_DK
    for hd in range(GDN_HEADS):
        for base, scale in ((0, GDN_DK ** -0.5), (nq, 1.0)):
            sl = slice(base + hd * GDN_DK, base + (hd + 1) * GDN_DK)
            v = xc[:, sl]
            qkv_ref[:, sl] = v * (lax.rsqrt(jnp.sum(v * v, axis=-1, keepdims=True) + NORM_EPS) * scale)
    qkv_ref[:, 2 * nq:] = xc[:, 2 * nq:]
    ba = ba_ref[...]
    cols = lax.broadcasted_iota(jnp.int32, ba.shape, 1)
    gval = nega_ref[...] * _softplus(ba + dtb_ref[...])
    bg_ref[...] = jnp.where(cols < 2 * GDN_HEADS, jax.nn.sigmoid(ba), gval)


def gdn_prep(zin, p, geo):
    m = zin.shape[0]
    tt, tps, ctx_t, nb = geo["tt"], geo["tiles_per_seq"], geo["ctx_tiles"], geo["batch"]
    specs = _halo_specs(GDN_QKV, COL_QKV // GDN_QKV, lambda j: j, tt, tps, m)
    specs.append(pl.BlockSpec((tt, 128), lambda b, j: (b * tps + j, COL_BA // 128)))
    full = lambda shape: pl.BlockSpec(shape, lambda b, j: (0,) * len(shape))
    specs += [full((4, GDN_QKV)), full((1, GDN_QKV)), full((1, 128)), full((1, 128))]
    pad = jnp.zeros((1, 128), F32)
    nega = pad.at[0, 8:16].set(-jnp.exp(p["gdn_a_log"].reshape(-1)))
    dtb = pad.at[0, 8:16].set(p["gdn_dt_bias"].reshape(-1))
    kern = functools.partial(_gdn_prep_kernel, ctx_t=ctx_t, n_t=tps)
    return pl.pallas_call(
        kern, grid=(nb, tps), in_specs=specs,
        out_specs=[pl.BlockSpec((tt, GDN_QKV), lambda b, j: (b * tps + j, 0)),
                   pl.BlockSpec((tt, 128), lambda b, j: (b * tps + j, 0))],
        out_shape=[jax.ShapeDtypeStruct((m, GDN_QKV), F32), jax.ShapeDtypeStruct((m, 128), F32)],
        scratch_shapes=[pltpu.VMEM((tt + 2 * HALO, GDN_QKV), F32)],
        compiler_params=_cp(("arbitrary", "arbitrary")), name="gdn_prep",
    )(zin, zin, zin, zin, p["gdn_conv_w"], p["gdn_conv_b"].reshape(1, -1), nega, dtb)


def _gdn_kernel(xf_ref, bgf_ref, xb_ref, bgb_ref, of_ref, ob_ref, s_s, *, gb):
    j = pl.program_id(1)

    @pl.when(j == 0)
    def _():
        s_s[...] = jnp.zeros_like(s_s)

    c = GDN_CHUNK
    nq = GDN_HEADS * GDN_DK
    ii = lax.broadcasted_iota(jnp.int32, (c, c), 0)
    jj = lax.broadcasted_iota(jnp.int32, (c, c), 1)
    incl = ((ii >= jj), (ii <= jj))
    strict = ((ii > jj), (ii < jj))
    blk = jnp.right_shift(ii, 4) == jnp.right_shift(jj, 4)
    eye = (ii == jj).astype(F32)

    chains = []
    for g in range(gb):
        for d, (x_ref, bg_ref) in enumerate(((xf_ref, bgf_ref), (xb_ref, bgb_ref))):
            bg = bg_ref[g]
            gam_all = _mm_f32(incl[d].astype(F32), bg)
            gam_t = gam_all.T
            for hd in range(GDN_HEADS):
                cb = d * GDN_HEADS + hd
                cg = 2 * GDN_HEADS + cb
                chains.append(dict(
                    g=g, d=d, hd=hd,
                    q=x_ref[g, :, hd * GDN_DK:(hd + 1) * GDN_DK],
                    k=x_ref[g, :, nq + hd * GDN_DK:nq + (hd + 1) * GDN_DK],
                    v=x_ref[g, :, 2 * nq + hd * GDN_DV:2 * nq + (hd + 1) * GDN_DV],
                    beta=bg[:, cb:cb + 1], gam=gam_all[:, cg:cg + 1], gam_row=gam_t[cg:cg + 1, :]))

    def put(name, fn):
        vals = [fn(ch) for ch in chains]
        for ch, val in zip(chains, vals):
            ch[name] = val

    put("dec_incl", lambda ch: jnp.where(
        incl[ch["d"]], jnp.exp(jnp.where(incl[ch["d"]], ch["gam"] - ch["gam_row"], 0.0)), 0.0))
    put("dec_strict", lambda ch: jnp.where(strict[ch["d"]], ch["dec_incl"], 0.0))
    put("g_last", lambda ch: ch["gam"][0:1, :] if ch["d"] == 1 else ch["gam"][c - 1:c, :])
    put("eg", lambda ch: jnp.exp(ch["gam"]))
    put("kb", lambda ch: ch["k"] * ch["beta"])
    put("a", lambda ch: _mm_nt(ch["kb"], ch["k"]) * ch["dec_strict"])
    put("rhs", lambda ch: jnp.concatenate([ch["v"] * ch["beta"], ch["kb"] * ch["eg"]], axis=1))
    put("a_d", lambda ch: jnp.where(blk, ch["a"], 0.0))
    put("a_o", lambda ch: ch["a"] - ch["a_d"])
    put("a2", lambda ch: _mm(ch["a_d"], ch["a_d"]))
    put("a4", lambda ch: _mm(ch["a2"], ch["a2"]))
    put("a8", lambda ch: _mm(ch["a4"], ch["a4"]))
    put("dinv", lambda ch: eye - ch["a_d"])
    put("dinv", lambda ch: ch["dinv"] + _mm(ch["dinv"], ch["a2"]))
    put("dinv", lambda ch: ch["dinv"] + _mm(ch["dinv"], ch["a4"]))
    put("dinv", lambda ch: ch["dinv"] + _mm(ch["dinv"], ch["a8"]))
    put("n1", lambda ch: _mm(ch["dinv"], ch["a_o"]))
    put("n2", lambda ch: _mm(ch["n1"], ch["n1"]))
    put("x", lambda ch: _mm(ch["dinv"], ch["rhs"]))
    put("x", lambda ch: ch["x"] + _mm(ch["n2"], ch["x"]))
    put("x", lambda ch: ch["x"] - _mm(ch["n1"], ch["x"]))
    put("qk", lambda ch: _mm_nt(ch["q"], ch["k"]) * ch["dec_incl"])
    put("s", lambda ch: s_s[ch["g"], ch["d"], ch["hd"]])
    put("v_new", lambda ch: ch["x"][:, :GDN_DV] - _mm(ch["x"][:, GDN_DV:], ch["s"]))
    put("o", lambda ch: _mm(ch["q"] * ch["eg"], ch["s"]) + _mm(ch["qk"], ch["v_new"]))
    put("s_new", lambda ch: ch["s"] * jnp.exp(ch["g_last"])
        + _mm_tn(ch["k"] * jnp.exp(ch["g_last"] - ch["gam"]), ch["v_new"]))
    for ch in chains:
        o_ref = of_ref if ch["d"] == 0 else ob_ref
        o_ref[ch["g"], :, ch["hd"] * GDN_DV:(ch["hd"] + 1) * GDN_DV] = ch["o"]
        s_s[ch["g"], ch["d"], ch["hd"]] = ch["s_new"]


def gdn_scan(qkvn, bg, geo):
    m = qkvn.shape[0]
    c = GDN_CHUNK
    nb, s_len, ctx = geo["batch"], geo["s_len"], geo["ctx"]
    gb = 2 if nb % 2 == 0 else 1
    n_c, ctx_c = s_len // c, ctx // c
    fwd = lambda b, j: (b, j, 0)
    bwd = lambda b, j: (b, _rev_tile(j, ctx_c, n_c), 0)
    w = GDN_HEADS * GDN_DV
    kern = functools.partial(_gdn_kernel, gb=gb)
    qkv3 = qkvn.reshape(nb, s_len, GDN_QKV)
    bg3 = bg.reshape(nb, s_len, 128)
    of, ob = pl.pallas_call(
        kern, grid=(nb // gb, n_c),
        in_specs=[pl.BlockSpec((gb, c, GDN_QKV), fwd), pl.BlockSpec((gb, c, 128), fwd),
                  pl.BlockSpec((gb, c, GDN_QKV), bwd), pl.BlockSpec((gb, c, 128), bwd)],
        out_specs=[pl.BlockSpec((gb, c, w), fwd), pl.BlockSpec((gb, c, w), bwd)],
        out_shape=[jax.ShapeDtypeStruct((nb, s_len, w), F32)] * 2,
        scratch_shapes=[pltpu.VMEM((gb, 2, GDN_HEADS, GDN_DK, GDN_DV), F32)],
        compiler_params=_cp(("arbitrary", "arbitrary")), name="gdn_scan",
    )(qkv3, bg3, qkv3, bg3)
    return of.reshape(m, w), ob.reshape(m, w)


def _fourier_kernel(c_ref, s_ref, u_ref, ur_ref, um_ref, cc_ref, sc_ref, o_ref, *, scale, tf):
    u = u_ref[0].astype(F32)
    ur = ur_ref[0].astype(F32)
    rows = lax.broadcasted_iota(jnp.int32, u.shape, 0)
    even = jnp.where(rows == 0, u, u + ur)
    odd = u - ur
    freq = pl.program_id(0) * tf + lax.broadcasted_iota(jnp.int32, (tf, 1), 0)
    sign = (1 - 2 * (freq & 1)).astype(F32)
    a = _mm(c_ref[...], even) + sign * um_ref[0].astype(F32)
    b = _mm(s_ref[...], odd)
    o = _mm(a, cc_ref[...]) - _mm(b, sc_ref[...])
    o_ref[0] = o * scale


def _dft_mats(t):
    f = jnp.arange(t, dtype=jnp.int32)
    ang = ((f[:, None] * f[None, :t // 2]) % t).astype(F32) * (2.0 * math.pi / t)
    return jnp.cos(ang).astype(BF16), jnp.sin(ang).astype(BF16)


def fourier(u, chan_c, chan_s):
    nb, t, w = u.shape
    th = t // 2
    cm, sm = _dft_mats(t)
    u_rev = jnp.concatenate([u[:, :1], jnp.flip(u[:, th + 1:], axis=1)], axis=1)
    u_mid = u[:, th:th + 1]
    tf = _pick(t, (512, 256, 128))
    kern = functools.partial(_fourier_kernel, scale=1.0 / math.sqrt(t * FFT_GROUP), tf=tf)
    return pl.pallas_call(
        kern, grid=(t // tf, nb),
        in_specs=[pl.BlockSpec((tf, th), lambda f, b: (f, 0)), pl.BlockSpec((tf, th), lambda f, b: (f, 0)),
                  pl.BlockSpec((1, th, w), lambda f, b: (b, 0, 0)), pl.BlockSpec((1, th, w), lambda f, b: (b, 0, 0)),
                  pl.BlockSpec((1, 1, w), lambda f, b: (b, 0, 0)),
                  pl.BlockSpec((w, w), lambda f, b: (0, 0)), pl.BlockSpec((w, w), lambda f, b: (0, 0))],
        out_specs=pl.BlockSpec((1, tf, w), lambda f, b: (b, f, 0)),
        out_shape=jax.ShapeDtypeStruct((nb, t, w), F32),
        compiler_params=_cp(("arbitrary", "arbitrary")), name="fourier",
    )(cm, sm, u, u_rev, u_mid, chan_c, chan_s)


def _mla_prep_kernel(cq_ref, ckv_ref, misc_ref, cos_ref, sin_ref, gq_ref, gkv_ref, wq_ref, wkv_ref,
                     q_ref, k_ref, v_ref, *, q_scale):
    def rms(x, g):
        return x * lax.rsqrt(jnp.mean(x * x, axis=-1, keepdims=True) + NORM_EPS) * g

    cos = cos_ref[...]
    sin = sin_ref[...]
    qa = _mm(rms(cq_ref[...], gq_ref[...]), wq_ref[...])
    kva = _mm(rms(ckv_ref[...], gkv_ref[...]), wkv_ref[...])
    misc = misc_ref[...]
    k_rope = misc[:, :MLA_ROPE] * cos + misc[:, MLA_ROPE:] * sin
    zpad = jnp.zeros((cos.shape[0], MLA_QK_PAD - MLA_NOPE - MLA_ROPE), F32)
    for hd in range(MLA_HEADS):
        qh = qa[:, hd * 256:(hd + 1) * 256]
        q_rope = qh[:, MLA_NOPE:MLA_NOPE + MLA_ROPE] * cos + qh[:, MLA_NOPE + MLA_ROPE:] * sin
        q_ref[0, hd] = (jnp.concatenate([qh[:, :MLA_NOPE], q_rope, zpad], axis=1) * q_scale).astype(BF16)
        kh = kva[:, hd * 256:hd * 256 + MLA_NOPE]
        k_ref[0, hd] = jnp.concatenate([kh, k_rope, zpad], axis=1).astype(BF16)
        v_ref[0, hd] = kva[:, hd * 256 + MLA_NOPE:(hd + 1) * 256].astype(BF16)


def mla_prep(zin, cos_t, sin_t, p, geo):
    tt, tps, nb, s_len = geo["tt"], geo["tiles_per_seq"], geo["batch"], geo["s_len"]
    row = lambda b, j: b * tps + j
    full = lambda shape: pl.BlockSpec(shape, lambda b, j: (0,) * len(shape))
    ctx_t = geo["ctx_tiles"]
    hs = lambda w: pl.BlockSpec((1, MLA_HEADS, tt, w),
                                lambda b, j: (b, 0, jnp.where(j < ctx_t, j + (tps - ctx_t), j - ctx_t), 0))
    kern = functools.partial(_mla_prep_kernel, q_scale=(MLA_NOPE + MLA_ROPE) ** -0.5)
    return pl.pallas_call(
        kern, grid=(nb, tps),
        in_specs=[pl.BlockSpec((tt, MLA_Q_RANK), lambda b, j: (row(b, j), COL_CQ // MLA_Q_RANK)),
                  pl.BlockSpec((tt, MLA_KV_RANK), lambda b, j: (row(b, j), COL_CKV // MLA_KV_RANK)),
                  pl.BlockSpec((tt, 128), lambda b, j: (row(b, j), COL_MISC // 128)),
                  pl.BlockSpec((tt, MLA_ROPE), lambda b, j: (j, 0)), pl.BlockSpec((tt, MLA_ROPE), lambda b, j: (j, 0)),
                  full((1, MLA_Q_RANK)), full((1, MLA_KV_RANK)),
                  full((MLA_Q_RANK, MLA_HEADS * 256)), full((MLA_KV_RANK, MLA_HEADS * 256))],
        out_specs=[hs(MLA_QK_PAD), hs(MLA_QK_PAD), hs(MLA_V)],
        out_shape=[jax.ShapeDtypeStruct((nb, MLA_HEADS, s_len, MLA_QK_PAD), BF16),
                   jax.ShapeDtypeStruct((nb, MLA_HEADS, s_len, MLA_QK_PAD), BF16),
                   jax.ShapeDtypeStruct((nb, MLA_HEADS, s_len, MLA_V), BF16)],
        compiler_params=_cp(("arbitrary", "arbitrary")), name="mla_prep",
    )(zin, zin, zin, cos_t, sin_t, p["mla_q_norm_g"].reshape(1, -1), p["mla_kv_norm_g"].reshape(1, -1),
      p["mla_wq"], p["mla_wkv"])


def _attn_kernel(q_ref, k_ref, v_ref, o_ref):
    s = lax.dot_general(q_ref[0, 0], k_ref[0, 0], (((1,), (1,)), ((), ())), preferred_element_type=F32)
    pr = jnp.exp(s - jnp.max(s, axis=-1, keepdims=True))
    den = jnp.sum(pr, axis=-1, keepdims=True)
    o_ref[0] = jnp.dot(pr.astype(BF16), v_ref[0, 0], preferred_element_type=F32) / den


def mla_attention(q, k, v, geo):
    nb, nh, s_len, _ = q.shape
    n_ctx = geo["ctx"]
    n_lat = s_len - n_ctx
    assert n_lat % n_ctx == 0

    def call(tq, q_block0, n_q, key_rows, key_block):
        return pl.pallas_call(
            _attn_kernel, grid=(nb, nh, n_q // tq),
            in_specs=[pl.BlockSpec((1, 1, tq, MLA_QK_PAD), lambda b, h, i: (b, h, q_block0 + i, 0)),
                      pl.BlockSpec((1, 1, key_rows, MLA_QK_PAD), lambda b, h, i: (b, h, key_block, 0)),
                      pl.BlockSpec((1, 1, key_rows, MLA_V), lambda b, h, i: (b, h, key_block, 0))],
            out_specs=pl.BlockSpec((1, tq, MLA_V), lambda b, h, i: (b, i, h)),
            out_shape=jax.ShapeDtypeStruct((nb, n_q, nh * MLA_V), F32),
            compiler_params=_cp(("arbitrary", "arbitrary", "arbitrary")), name="mla_attention",
        )(q, k, v)

    tq_lat = _pick(n_lat, (512, 256, 128))
    tq_ctx = _pick(n_ctx, (256, 128, 64))
    o_lat = call(tq_lat, 0, n_lat, s_len, 0)
    o_ctx = call(tq_ctx, n_lat // tq_ctx, n_ctx, n_ctx, n_lat // n_ctx)
    return o_lat, o_ctx


def _merge_kernel(hn_ref, hf_ref, hb_ref, lg_ref, of_ref, ob_ref, z_ref, gg_ref, yc_ref, yd_ref, wg_ref, wb_ref,
                  wo_ref, o_ref, y_s, acc_s, *, n_j):
    j = pl.program_id(1)

    @pl.when(j == 0)
    def _():
        lg = lg_ref[...]
        gelu = 0.5 * lg * (1.0 + jnp.tanh(math.sqrt(2.0 / math.pi) * (lg + 0.044715 * (lg * lg * lg))))
        y_s[0] = (gelu * (hf_ref[...] + hb_ref[...])).astype(BF16)
        for hd in range(GDN_HEADS):
            sl = slice(hd * GDN_DV, (hd + 1) * GDN_DV)
            o = of_ref[:, sl] + ob_ref[:, sl]
            o = o * lax.rsqrt(jnp.mean(o * o, axis=-1, keepdims=True) + NORM_EPS) * gg_ref[...]
            z = z_ref[:, sl]
            y_s[1, :, sl] = (o * (z * jax.nn.sigmoid(z))).astype(BF16)
        y_s[2] = yc_ref[...].astype(BF16)
        y_s[3] = yd_ref[...].astype(BF16)
        acc_s[...] = jnp.zeros_like(acc_s)

    hn = hn_ref[...]
    merged = None
    for i in range(N_BRANCH):
        gate = jax.nn.sigmoid(jnp.dot(hn, wg_ref[i], preferred_element_type=F32))
        term = gate * jnp.dot(y_s[i], wb_ref[i], preferred_element_type=F32)
        merged = term if merged is None else merged + term
    acc_s[...] += jnp.dot(merged.astype(BF16), wo_ref[...], preferred_element_type=F32)

    @pl.when(j == n_j - 1)
    def _():
        o_ref[...] = acc_s[...]


def merge(hn, zin, hf, hb, of, ob, yc, yd, p):
    m, d = hn.shape
    tm = _pick(m, (512, 256, 128))
    tn = 256
    n_j = d // tn
    w = BRANCH_WIDTH
    tok = lambda a_w, cb: pl.BlockSpec((tm, a_w), lambda i, j: (i, cb))
    kern = functools.partial(_merge_kernel, n_j=n_j)
    return pl.pallas_call(
        kern, grid=(m // tm, n_j),
        in_specs=[tok(d, 0), tok(w, 0), tok(w, 0), tok(w, COL_LG // w), tok(w, 0), tok(w, 0), tok(w, COL_Z // w),
                  pl.BlockSpec((1, GDN_DV), lambda i, j: (0, 0)), tok(w, 0), tok(w, 0),
                  pl.BlockSpec((N_BRANCH, d, tn), lambda i, j: (0, 0, j)),
                  pl.BlockSpec((N_BRANCH, w, tn), lambda i, j: (0, 0, j)),
                  pl.BlockSpec((tn, d), lambda i, j: (j, 0))],
        out_specs=pl.BlockSpec((tm, d), lambda i, j: (i, 0)),
        out_shape=jax.ShapeDtypeStruct((m, d), F32),
        scratch_shapes=[pltpu.VMEM((N_BRANCH, tm, w), BF16), pltpu.VMEM((tm, d), F32)],
        compiler_params=_cp(("arbitrary", "arbitrary")), name="merge",
    )(hn, hf, hb, zin, of, ob, zin, p["gdn_norm_g"].reshape(1, -1), yc, yd, p["w_gate"], p["w_branch"], p["w_out"])


def _moe_kernel(be_ref, nv_ref, x_ref, w1_ref, b1_ref, w2_ref, b2_ref, o_ref, w1_s, w2_s):
    i = pl.program_id(0)
    new_expert = jnp.logical_or(i == 0, be_ref[i] != be_ref[jnp.maximum(i - 1, 0)])

    @pl.when(new_expert)
    def _():
        w1_s[...] = w1_ref[0, 0].astype(BF16)
        w2_s[...] = w2_ref[0, 0].astype(BF16)

    @pl.when(i < nv_ref[0])
    def _():
        h = jnp.dot(x_ref[...], w1_s[...], preferred_element_type=F32) + b1_ref[0, 0]
        h_glu = jnp.minimum(h[:, :D_EXPERT], SWIGLU_LIMIT)
        h_lin = jnp.clip(h[:, D_EXPERT:], -SWIGLU_LIMIT, SWIGLU_LIMIT)
        act = h_glu * jax.nn.sigmoid(SWIGLU_ALPHA * h_glu) * (h_lin + 1.0)
        y = jnp.dot(act.astype(BF16), w2_s[...], preferred_element_type=F32) + b2_ref[0, 0]
        o_ref[...] = y.astype(o_ref.dtype)

    @pl.when(i >= nv_ref[0])
    def _():
        o_ref[...] = jnp.zeros_like(o_ref)


def moe_experts(xg, block_e, n_valid, layer, w1, b1, w2, b2):
    p_rows, d = xg.shape
    bm = MOE_ROWS
    n_blocks = p_rows // bm
    depth, ne, _, dh = w1.shape
    grid_spec = pltpu.PrefetchScalarGridSpec(
        num_scalar_prefetch=2, grid=(n_blocks,),
        in_specs=[pl.BlockSpec((bm, d), lambda i, be, nv: (i, 0)),
                  pl.BlockSpec((1, 1, d, dh), lambda i, be, nv: (layer, be[i], 0, 0)),
                  pl.BlockSpec((1, 1, 1, dh), lambda i, be, nv: (layer, be[i], 0, 0)),
                  pl.BlockSpec((1, 1, D_EXPERT, d), lambda i, be, nv: (layer, be[i], 0, 0)),
                  pl.BlockSpec((1, 1, 1, d), lambda i, be, nv: (layer, be[i], 0, 0))],
        out_specs=pl.BlockSpec((bm, d), lambda i, be, nv: (i, 0)),
        scratch_shapes=[pltpu.VMEM((d, dh), BF16), pltpu.VMEM((D_EXPERT, d), BF16)])
    return pl.pallas_call(
        _moe_kernel, grid_spec=grid_spec, out_shape=jax.ShapeDtypeStruct((p_rows, d), BF16),
        compiler_params=_cp(("arbitrary",)), name="moe_experts",
    )(block_e, n_valid, xg, w1, b1.reshape(depth, ne, 1, dh), w2, b2.reshape(depth, ne, 1, d))


def moe_ffn(hn, idx, rank, counts, layer, w1, b1, w2, b2):
    n, d = hn.shape
    bm = MOE_ROWS
    nk = n * TOP_K
    counts = counts.astype(jnp.int32)
    padded = (counts + bm - 1) // bm * bm
    pend = jnp.cumsum(padded)
    pstart = pend - padded
    experts = jnp.arange(N_EXPERTS, dtype=jnp.int32)
    pos = rank + jnp.sum(jnp.where(idx[..., None] == experts, pstart, 0), axis=-1)
    n_blocks = (nk + N_EXPERTS * (bm - 1) + bm - 1) // bm
    p_rows = n_blocks * bm
    real = (idx.reshape(nk) << KEY_EXPERT_SHIFT) + jnp.arange(nk, dtype=jnp.int32)
    fill = jnp.arange(bm - 1, dtype=jnp.int32)
    pad_keys = jnp.where(fill[None, :] < (padded - counts)[:, None],
                         (experts[:, None] << KEY_EXPERT_SHIFT) + (1 << KEY_PAD_BIT) + fill[None, :],
                         jnp.iinfo(jnp.int32).max)
    tail = jnp.full((p_rows - nk - N_EXPERTS * (bm - 1),), jnp.iinfo(jnp.int32).max, jnp.int32)
    keys = lax.sort(jnp.concatenate([real, pad_keys.reshape(-1), tail]))
    is_pad = (keys >> KEY_PAD_BIT) & 1
    buf_tok = jnp.where(is_pad == 1, 0, (keys & ((1 << KEY_PAD_BIT) - 1)) // TOP_K)
    block_start = jnp.arange(n_blocks, dtype=jnp.int32) * bm
    block_e = jnp.minimum(jnp.sum((pend[None, :] <= block_start[:, None]).astype(jnp.int32), axis=1), N_EXPERTS - 1)
    n_valid = (pend[-1:] // bm).astype(jnp.int32)
    y = moe_experts(hn[buf_tok], block_e, n_valid, layer, w1, b1, w2, b2)
    return y[pos.T.reshape(nk)]


def _rope_rot(w):
    h = MLA_ROPE // 4
    a, b, c, d = w[..., :h], w[..., h:2 * h], w[..., 2 * h:3 * h], w[..., 3 * h:]
    return jnp.concatenate([-b, a, -d, c], axis=-1)


def _prep_layer(w_in_all, layer, mla_w_uq, mla_w_ukv):
    d = w_in_all.shape[1]
    o = np.cumsum([0, 512, 512, GDN_QKV, 512, 8, 8, 512, 512, 256, 64, N_BRANCH * d])
    lu, lg, qkv, z, be, al, fu, cq, ckv, kr, mg = [w_in_all[layer, :, o[i]:o[i + 1]] for i in range(11)]
    pad = jnp.zeros((d, 128 - 16), w_in_all.dtype)
    w_main = jnp.concatenate([qkv, lu, lg, z, fu, cq, ckv, kr, _rope_rot(kr), be, al, pad], axis=1).astype(BF16)
    w_gate = jnp.transpose(mg.reshape(d, N_BRANCH, d), (1, 0, 2)).astype(BF16)
    uq = mla_w_uq.reshape(MLA_Q_RANK, MLA_HEADS, MLA_NOPE + MLA_ROPE)
    wq = jnp.concatenate([uq, _rope_rot(uq[..., MLA_NOPE:])], axis=-1).reshape(MLA_Q_RANK, MLA_HEADS * 256)
    return w_main, w_gate, wq.astype(BF16), mla_w_ukv.astype(BF16)


def _rope_tables(ctx, n_lat):
    half = MLA_ROPE // 4
    inv = jnp.power(ROPE_BASE, -jnp.arange(half, dtype=F32) / half)
    t = jnp.arange(n_lat)
    row = (t // GRID_W).astype(F32)[:, None] * inv
    col = (t % GRID_W).astype(F32)[:, None] * inv
    cos = jnp.concatenate([jnp.cos(row), jnp.cos(row), jnp.cos(col), jnp.cos(col)], axis=1)
    sin = jnp.concatenate([jnp.sin(row), jnp.sin(row), jnp.sin(col), jnp.sin(col)], axis=1)
    cos = jnp.concatenate([jnp.ones((ctx, MLA_ROPE), F32), cos], axis=0)
    sin = jnp.concatenate([jnp.zeros((ctx, MLA_ROPE), F32), sin], axis=0)
    return cos, sin


def _chan_dft():
    idx = np.arange(FFT_GROUP)
    ang = 2.0 * np.pi * ((idx[:, None] * idx[None, :]) % FFT_GROUP) / FFT_GROUP
    eye = np.eye(FFT_GROUPS)
    return (jnp.asarray(np.kron(eye, np.cos(ang)), BF16), jnp.asarray(np.kron(eye, np.sin(ang)), BF16))


def kernel(x, c, ctx, c_ctx, w_ada, b_ada, norm1_g, norm2_g, w_in, lru_conv_w, lru_conv_b, lru_w_a, lru_b_a, lru_w_x, lru_b_x, lru_lambda, gdn_conv_w, gdn_conv_b, gdn_a_log, gdn_dt_bias, gdn_norm_g, mla_q_norm_g, mla_kv_norm_g, mla_w_uq, mla_w_ukv, w_branch, w_out, router_w, router_b, exp_w1, exp_b1, exp_w2, exp_b2, final_norm_g):
    nb, n_lat, d = x.shape
    n_ctx = ctx.shape[1]
    depth = w_ada.shape[0]
    s_len = n_ctx + n_lat
    m = nb * s_len
    tt = _pick(math.gcd(n_ctx, n_lat), (256, 128, 64))
    geo = dict(tt=tt, tiles_per_seq=s_len // tt, ctx_tiles=n_ctx // tt, batch=nb, s_len=s_len, ctx=n_ctx)

    mod_rows = 8 * ((nb + 1 + 7) // 8)
    cmat = jnp.zeros((mod_rows, d), F32).at[:nb].set(c).at[nb].set(c_ctx)
    mod_all = ada_table(cmat, w_ada, b_ada).reshape(depth, mod_rows * N_MOD, 1, d)

    cos_t, sin_t = _rope_tables(n_ctx, n_lat)
    chan_c, chan_s = _chan_dft()
    h = jnp.concatenate([ctx, x], axis=1).reshape(m, d)

    delta, prev_mod, gate_k = None, None, None
    for l in range(depth):
        w_main, w_gate, wq, wkv = _prep_layer(w_in, l, mla_w_uq[l], mla_w_ukv[l])
        p = dict(lru_conv_w=lru_conv_w[l], lru_conv_b=lru_conv_b[l], lru_w_a=lru_w_a[l].astype(BF16),
                 lru_b_a=lru_b_a[l], lru_w_x=lru_w_x[l].astype(BF16), lru_b_x=lru_b_x[l], lru_lambda=lru_lambda[l],
                 gdn_conv_w=gdn_conv_w[l], gdn_conv_b=gdn_conv_b[l], gdn_a_log=gdn_a_log[l],
                 gdn_dt_bias=gdn_dt_bias[l], gdn_norm_g=gdn_norm_g[l], mla_q_norm_g=mla_q_norm_g[l],
                 mla_kv_norm_g=mla_kv_norm_g[l], mla_wq=wq, mla_wkv=wkv, w_gate=w_gate,
                 w_branch=w_branch[l].astype(BF16), w_out=w_out[l].astype(BF16))
        mod = mod_all[l]
        h, hn = norm_mod(h, norm1_g[l], geo, mod=mod, shift_k=0, scale_k=1, delta=delta, gate_k=gate_k,
                         prev_mod=prev_mod)
        zin = matmul(hn, w_main)
        hf, hb = lru_scan(zin, p, geo)
        qkvn, bg = gdn_prep(zin, p, geo)
        of, ob = gdn_scan(qkvn, bg, geo)
        fu = zin[:, COL_FU:COL_FU + FFT_WIDTH].astype(BF16).reshape(nb, s_len, FFT_WIDTH)
        yc = jnp.concatenate([fourier(fu[:, :n_ctx], chan_c, chan_s), fourier(fu[:, n_ctx:], chan_c, chan_s)],
                             axis=1).reshape(m, FFT_WIDTH)
        q, k, v = mla_prep(zin, cos_t, sin_t, p, geo)
        yd_lat, yd_ctx = mla_attention(q, k, v, geo)
        yd = jnp.concatenate([yd_ctx, yd_lat], axis=1).reshape(m, MLA_HEADS * MLA_V)
        mixed = merge(hn, zin, hf, hb, of, ob, yc, yd, p)
        h, hn2, idx, gates, rank, counts = norm_mod(
            h, norm2_g[l], geo, mod=mod, shift_k=3, scale_k=4, delta=("dense", mixed), gate_k=2, prev_mod=mod,
            router=(router_w[l], router_b[l]))
        ygath = moe_ffn(hn2, idx, rank, counts[0], l, exp_w1, exp_b1, exp_w2, exp_b2)
        delta, prev_mod, gate_k = ("moe", ygath, gates), mod, 5
    (out,) = norm_mod(h, final_norm_g, geo, delta=delta, gate_k=5, prev_mod=prev_mod, emit_h=False,
                      out_dtype=F32, lat_only=True)
    return out.reshape(nb, n_lat, d)
```

```python
import functools
import math

import jax
import jax.numpy as jnp
import numpy as np
from jax import lax
from jax.experimental import pallas as pl
from jax.experimental.pallas import tpu as pltpu

F32 = jnp.float32
BF16 = jnp.bfloat16

GRID_W = 64
NORM_EPS = 1e-6
N_MOD = 6
LRU_WIDTH = 512
LRU_BLOCKS = 4
LRU_BLOCK = 128
LRU_C = 8.0
GDN_HEADS = 4
GDN_DK = 128
GDN_DV = 128
GDN_CHUNK = 64
GDN_QKV = GDN_HEADS * (2 * GDN_DK + GDN_DV)
FFT_GROUPS = 4
FFT_GROUP = 128
FFT_WIDTH = 512
MLA_HEADS = 4
MLA_Q_RANK = 512
MLA_KV_RANK = 256
MLA_NOPE = 128
MLA_ROPE = 64
MLA_V = 128
MLA_QK_PAD = 256
ROPE_BASE = 10000.0
N_BRANCH = 4
BRANCH_WIDTH = 512
N_EXPERTS = 32
TOP_K = 4
D_EXPERT = 512
SWIGLU_LIMIT = 7.0
SWIGLU_ALPHA = 1.702
MOE_ROWS = 512
KEY_EXPERT_SHIFT = 20
KEY_PAD_BIT = 19

COL_QKV = 0
COL_LU = 1536
COL_LG = 2048
COL_Z = 2560
COL_FU = 3072
COL_CQ = 3584
COL_CKV = 4096
COL_MISC = 4352
COL_BA = 4480
ZIN_W = 4608

VMEM_LIMIT = 56 * 1024 * 1024
HALO = 8


def _cp(sem, vmem=VMEM_LIMIT):
    return pltpu.CompilerParams(dimension_semantics=sem, vmem_limit_bytes=vmem)


def _pick(n, cands):
    for c in cands:
        if n % c == 0:
            return c
    raise ValueError(f"no tile for {n} in {cands}")


def _mm(a, b):
    return jnp.dot(a.astype(BF16), b.astype(BF16), preferred_element_type=F32)


def _mm_nt(a, b):
    return lax.dot_general(a.astype(BF16), b.astype(BF16), (((1,), (1,)), ((), ())), preferred_element_type=F32)


def _mm_tn(a, b):
    return lax.dot_general(a.astype(BF16), b.astype(BF16), (((0,), (0,)), ((), ())), preferred_element_type=F32)


def _mm_f32(a, b):
    return jnp.dot(a, b, preferred_element_type=F32, precision=lax.Precision.HIGHEST)


def _softplus(y):
    return jnp.maximum(y, 0.0) + jnp.log1p(jnp.exp(-jnp.abs(y)))


def _ada_kernel(c_ref, w_ref, b_ref, o_ref):
    cv = c_ref[...]
    s = cv * jax.nn.sigmoid(cv)
    o_ref[0] = _mm(s, w_ref[0]) + b_ref[0]


def ada_table(cmat, w_ada, b_ada):
    depth, d, n = w_ada.shape
    rows = cmat.shape[0]
    tn = _pick(n, (1024, 512, 256, 128))
    return pl.pallas_call(
        _ada_kernel,
        grid=(depth, n // tn),
        in_specs=[pl.BlockSpec((rows, d), lambda l, j: (0, 0)),
                  pl.BlockSpec((1, d, tn), lambda l, j: (l, 0, j)),
                  pl.BlockSpec((1, 1, tn), lambda l, j: (l, 0, j))],
        out_specs=pl.BlockSpec((1, rows, tn), lambda l, j: (l, 0, j)),
        out_shape=jax.ShapeDtypeStruct((depth, rows, n), F32),
        compiler_params=_cp(("arbitrary", "arbitrary")),
        name="ada_table",
    )(cmat, w_ada, b_ada.reshape(depth, 1, n))


def _norm_kernel(*refs, delta_kind, has_mod, has_router, emit_h):
    it = iter(refs)
    h_ref = next(it)
    if delta_kind == "dense":
        d_ref = next(it)
        gate_ref = next(it)
    elif delta_kind == "moe":
        yg_ref = next(it)
        gate_ref = next(it)
    g_ref = next(it)
    if has_mod:
        shift_ref = next(it)
        scale_ref = next(it)
    if has_router:
        rw_ref = next(it)
        rb_ref = next(it)
    if emit_h:
        ho_ref = next(it)
    y_ref = next(it)
    if has_router:
        idx_ref = next(it)
        gt_ref = next(it)
        rank_ref = next(it)
        cnt_ref = next(it)

    h = h_ref[...]
    d = h.shape[1]
    if delta_kind == "dense":
        h = h + gate_ref[0] * d_ref[...]
    elif delta_kind == "moe":
        tt = h.shape[0]
        rows = lax.broadcasted_iota(jnp.int32, (tt, TOP_K * tt), 0)
        cols = lax.broadcasted_iota(jnp.int32, (tt, TOP_K * tt), 1)
        pick = (jnp.right_shift(cols, TOP_K.bit_length() - 1) == rows).astype(BF16)
        h = h + gate_ref[0] * jnp.dot(pick, yg_ref[...], preferred_element_type=F32)
    if emit_h:
        ho_ref[...] = h
    y = h * lax.rsqrt(jnp.mean(h * h, axis=-1, keepdims=True) + NORM_EPS) * g_ref[...]
    if has_mod:
        y = y * (1.0 + scale_ref[0]) + shift_ref[0]
    y_ref[...] = y.astype(y_ref.dtype)
    if has_router:
        @pl.when(pl.program_id(0) == 0)
        def _():
            cnt_ref[...] = jnp.zeros_like(cnt_ref)

        logit = _mm_f32(y, rw_ref[...]) + rb_ref[...]
        tt, ne = logit.shape
        lane = lax.broadcasted_iota(jnp.int32, (tt, ne), 1).astype(F32)
        vals, hots = [], []
        for kk in range(TOP_K):
            top = jnp.max(logit, axis=-1, keepdims=True)
            arg = jnp.min(jnp.where(logit == top, lane, float(ne)), axis=-1, keepdims=True)
            hot = lane == arg
            idx_ref[:, kk:kk + 1] = arg.astype(jnp.int32)
            vals.append(top)
            hots.append(hot)
            logit = jnp.where(hot, -jnp.inf, logit)
        exps = [jnp.exp(v - vals[0]) for v in vals]
        den = exps[0] + exps[1] + exps[2] + exps[3]
        for kk in range(TOP_K):
            gt_ref[:, kk:kk + 1] = exps[kk] / den
        hot_all = hots[0].astype(F32) + hots[1].astype(F32) + hots[2].astype(F32) + hots[3].astype(F32)
        ii = lax.broadcasted_iota(jnp.int32, (tt, tt), 0)
        jj = lax.broadcasted_iota(jnp.int32, (tt, tt), 1)
        before = _mm((jj < ii).astype(F32), hot_all) + cnt_ref[...]
        for kk in range(TOP_K):
            rank_ref[:, kk:kk + 1] = jnp.sum(jnp.where(hots[kk], before, 0.0), axis=-1, keepdims=True).astype(jnp.int32)
        cnt_ref[...] += jnp.sum(hot_all, axis=0, keepdims=True)


def norm_mod(h, g, geo, *, mod=None, shift_k=None, scale_k=None, delta=None, gate_k=None, prev_mod=None,
             router=None, emit_h=True, out_dtype=None, lat_only=False):
    m, d = h.shape
    out_dtype = BF16 if out_dtype is None else out_dtype
    tt, tps, ctx_t, nb = geo["tt"], geo["tiles_per_seq"], geo["ctx_tiles"], geo["batch"]
    if lat_only:
        lat_t = tps - ctx_t
        grid = (nb * lat_t,)
        in_row = lambda i: (i // lat_t) * tps + ctx_t + i % lat_t
        mod_row = lambda i: i // lat_t
        out_rows = nb * lat_t * tt
    else:
        grid = (m // tt,)
        in_row = lambda i: i
        mod_row = lambda i: jnp.where(i % tps < ctx_t, nb, i // tps)
        out_rows = m
    tok = lambda w: pl.BlockSpec((tt, w), lambda i: (in_row(i), 0))
    out_tok = lambda w: pl.BlockSpec((tt, w), lambda i: (i, 0))

    def mod_spec(k):
        return pl.BlockSpec((1, 1, d), lambda i: (mod_row(i) * N_MOD + k, 0, 0))

    args, specs = [h], [tok(d)]
    delta_kind = None if delta is None else delta[0]
    if delta_kind == "dense":
        args += [delta[1], prev_mod]
        specs += [tok(d), mod_spec(gate_k)]
    elif delta_kind == "moe":
        args += [delta[1], prev_mod]
        specs += [pl.BlockSpec((TOP_K * tt, d), lambda i: (in_row(i), 0)), mod_spec(gate_k)]
    args.append(g.reshape(1, d))
    specs.append(pl.BlockSpec((1, d), lambda i: (0, 0)))
    if mod is not None:
        args += [mod, mod]
        specs += [mod_spec(shift_k), mod_spec(scale_k)]
    if router is not None:
        rw, rb = router
        ne = rw.shape[1]
        args += [rw, rb.reshape(1, ne)]
        specs += [pl.BlockSpec(rw.shape, lambda i: (0, 0)), pl.BlockSpec((1, ne), lambda i: (0, 0))]
    out_shape, out_specs = [], []
    if emit_h:
        out_shape.append(jax.ShapeDtypeStruct((out_rows, d), F32))
        out_specs.append(out_tok(d))
    out_shape.append(jax.ShapeDtypeStruct((out_rows, d), out_dtype))
    out_specs.append(out_tok(d))
    if router is not None:
        out_shape += [jax.ShapeDtypeStruct((out_rows, TOP_K), jnp.int32), jax.ShapeDtypeStruct((out_rows, TOP_K), F32),
                      jax.ShapeDtypeStruct((out_rows, TOP_K), jnp.int32), jax.ShapeDtypeStruct((1, ne), F32)]
        out_specs += [out_tok(TOP_K), out_tok(TOP_K), out_tok(TOP_K), pl.BlockSpec((1, ne), lambda i: (0, 0))]
    kern = functools.partial(_norm_kernel, delta_kind=delta_kind, has_mod=mod is not None,
                             has_router=router is not None, emit_h=emit_h)
    return pl.pallas_call(kern, grid=grid, in_specs=specs, out_specs=out_specs, out_shape=out_shape,
                          compiler_params=_cp(("arbitrary",)), name="norm_mod")(*args)


def _matmul_kernel(x_ref, w_ref, o_ref):
    o_ref[...] = jnp.dot(x_ref[...], w_ref[...], preferred_element_type=F32).astype(o_ref.dtype)


def matmul(x, w, out_dtype=F32):
    m, k = x.shape
    n = w.shape[1]
    tm = _pick(m, (1024, 512, 256, 128))
    tn = _pick(n, (768, 512, 256, 128))
    return pl.pallas_call(
        _matmul_kernel,
        grid=(m // tm, n // tn),
        in_specs=[pl.BlockSpec((tm, k), lambda i, j: (i, 0)), pl.BlockSpec((k, tn), lambda i, j: (0, j))],
        out_specs=pl.BlockSpec((tm, tn), lambda i, j: (i, j)),
        out_shape=jax.ShapeDtypeStruct((m, n), out_dtype),
        compiler_params=_cp(("arbitrary", "arbitrary")),
        name="in_proj",
    )(x, w)


def _conv4(x, xp, xn, prev_ok, next_ok, cw_ref, cb_ref, ext_ref):
    tt = x.shape[0]
    ext_ref[0:HALO, :] = xp * prev_ok
    ext_ref[HALO:HALO + tt, :] = x
    ext_ref[HALO + tt:2 * HALO + tt, :] = xn * next_ok
    y = cb_ref[...] + cw_ref[2:3, :] * x
    y = y + cw_ref[0:1, :] * ext_ref[HALO - 2:HALO - 2 + tt, :]
    y = y + cw_ref[1:2, :] * ext_ref[HALO - 1:HALO - 1 + tt, :]
    y = y + cw_ref[3:4, :] * ext_ref[HALO + 1:HALO + 1 + tt, :]
    return y


def _seg_edges(tile, ctx_t, n_t):
    first = jnp.logical_or(tile == 0, tile == ctx_t)
    last = jnp.logical_or(tile == ctx_t - 1, tile == n_t - 1)
    return jnp.where(first, 0.0, 1.0).astype(F32), jnp.where(last, 0.0, 1.0).astype(F32)


def _rev_tile(j, ctx_t, n_t):
    return jnp.where(j < ctx_t, ctx_t - 1 - j, n_t - 1 - (j - ctx_t))


def _halo_specs(width, col_block, tile_of, tt, tps, m):
    r8 = tt // HALO
    last8 = m // HALO - 1

    def cur(b, j):
        return (b * tps + tile_of(j), col_block)

    def prev(b, j):
        return (jnp.maximum((b * tps + tile_of(j)) * r8 - 1, 0), col_block)

    def nxt(b, j):
        return (jnp.minimum((b * tps + tile_of(j) + 1) * r8, last8), col_block)

    return [pl.BlockSpec((tt, width), cur), pl.BlockSpec((HALO, width), prev), pl.BlockSpec((HALO, width), nxt)]


def _lru_kernel(xf_ref, xfp_ref, xfn_ref, xb_ref, xbp_ref, xbn_ref, cw_ref, cb_ref, wa_ref, ba_ref, wx_ref, bx_ref,
                lam_ref, hf_ref, hb_ref, ext_s, a_s, b_s, carry_s, *, tt, ctx_t, n_t):
    j = pl.program_id(1)

    @pl.when(j == 0)
    def _():
        carry_s[...] = jnp.zeros_like(carry_s)

    rows = lax.broadcasted_iota(jnp.int32, (HALO, LRU_WIDTH), 0)
    for d, (x_ref, xp_ref, xn_ref, out_ref) in enumerate(((xf_ref, xfp_ref, xfn_ref, hf_ref),
                                                          (xb_ref, xbp_ref, xbn_ref, hb_ref))):
        tile = j if d == 0 else _rev_tile(j, ctx_t, n_t)
        prev_ok, next_ok = _seg_edges(tile, ctx_t, n_t)
        xc = _conv4(x_ref[...], xp_ref[...], xn_ref[...], prev_ok, next_ok, cw_ref, cb_ref, ext_s)
        sp = _softplus(-lam_ref[d:d + 1, :])
        for g in range(LRU_BLOCKS):
            sl = slice(g * LRU_BLOCK, (g + 1) * LRU_BLOCK)
            xg = xc[:, sl]
            r = jax.nn.sigmoid(_mm(xg, wa_ref[d, g]) + ba_ref[d:d + 1, sl])
            ig = jax.nn.sigmoid(_mm(xg, wx_ref[d, g]) + bx_ref[d:d + 1, sl])
            log_a = -LRU_C * r * sp[:, sl]
            a = jnp.exp(log_a)
            a_s[d, :, sl] = a
            b_s[d, :, sl] = jnp.sqrt(-jnp.tanh(log_a) * (a * a + 1.0)) * (ig * xg)

        n_grp = tt // HALO

        def group(gi, carry, d=d, out_ref=out_ref):
            g0 = gi if d == 0 else n_grp - 1 - gi
            off = pl.multiple_of(g0 * HALO, HALO)
            av = a_s[d, pl.ds(off, HALO), :]
            bv = b_s[d, pl.ds(off, HALO), :]
            for s in (1, 2, 4):
                shift = s if d == 0 else HALO - s
                a_sh = pltpu.roll(av, shift, 0)
                b_sh = pltpu.roll(bv, shift, 0)
                ok = (rows >= s) if d == 0 else (rows < HALO - s)
                bv = jnp.where(ok, av * b_sh + bv, bv)
                av = jnp.where(ok, av * a_sh, av)
            hv = bv + av * carry
            out_ref[pl.ds(off, HALO), :] = hv
            return hv[HALO - 1:HALO, :] if d == 0 else hv[0:1, :]

        carry_s[d:d + 1, :] = lax.fori_loop(0, n_grp, group, carry_s[d:d + 1, :])


def lru_scan(zin, p, geo):
    m = zin.shape[0]
    tt, tps, ctx_t, nb = geo["tt"], geo["tiles_per_seq"], geo["ctx_tiles"], geo["batch"]
    cb = COL_LU // LRU_WIDTH
    fwd = _halo_specs(LRU_WIDTH, cb, lambda j: j, tt, tps, m)
    bwd = _halo_specs(LRU_WIDTH, cb, lambda j: _rev_tile(j, ctx_t, tps), tt, tps, m)
    full = lambda shape: pl.BlockSpec(shape, lambda b, j: (0,) * len(shape))
    w_specs = [full((4, LRU_WIDTH)), full((1, LRU_WIDTH)), full((2, LRU_BLOCKS, LRU_BLOCK, LRU_BLOCK)),
               full((2, LRU_WIDTH)), full((2, LRU_BLOCKS, LRU_BLOCK, LRU_BLOCK)), full((2, LRU_WIDTH)),
               full((2, LRU_WIDTH))]
    out_f = pl.BlockSpec((tt, LRU_WIDTH), lambda b, j: (b * tps + j, 0))
    out_b = pl.BlockSpec((tt, LRU_WIDTH), lambda b, j: (b * tps + _rev_tile(j, ctx_t, tps), 0))
    kern = functools.partial(_lru_kernel, tt=tt, ctx_t=ctx_t, n_t=tps)
    return pl.pallas_call(
        kern, grid=(nb, tps), in_specs=fwd + bwd + w_specs, out_specs=[out_f, out_b],
        out_shape=[jax.ShapeDtypeStruct((m, LRU_WIDTH), F32)] * 2,
        scratch_shapes=[pltpu.VMEM((tt + 2 * HALO, LRU_WIDTH), F32), pltpu.VMEM((2, tt, LRU_WIDTH), F32),
                        pltpu.VMEM((2, tt, LRU_WIDTH), F32), pltpu.VMEM((2, LRU_WIDTH), F32)],
        compiler_params=_cp(("arbitrary", "arbitrary")), name="lru_scan",
    )(zin, zin, zin, zin, zin, zin, p["lru_conv_w"], p["lru_conv_b"].reshape(1, -1), p["lru_w_a"], p["lru_b_a"],
      p["lru_w_x"], p["lru_b_x"], p["lru_lambda"])


def _gdn_prep_kernel(x_ref, xp_ref, xn_ref, ba_ref, cw_ref, cb_ref, nega_ref, dtb_ref, qkv_ref, bg_ref, ext_s,
                     *, ctx_t, n_t):
    tile = pl.program_id(1)
    prev_ok, next_ok = _seg_edges(tile, ctx_t, n_t)
    xc = _conv4(x_ref[...], xp_ref[...], xn_ref[...], prev_ok, next_ok, cw_ref, cb_ref, ext_s)
    xc = xc * jax.nn.sigmoid(xc)
    nq = GDN_HEADS * GDN_DK
    for hd in range(GDN_HEADS):
        for base, scale in ((0, GDN_DK ** -0.5), (nq, 1.0)):
            sl = slice(base + hd * GDN_DK, base + (hd + 1) * GDN_DK)
            v = xc[:, sl]
            qkv_ref[:, sl] = v * (lax.rsqrt(jnp.sum(v * v, axis=-1, keepdims=True) + NORM_EPS) * scale)
    qkv_ref[:, 2 * nq:] = xc[:, 2 * nq:]
    ba = ba_ref[...]
    cols = lax.broadcasted_iota(jnp.int32, ba.shape, 1)
    gval = nega_ref[...] * _softplus(ba + dtb_ref[...])
    bg_ref[...] = jnp.where(cols < 2 * GDN_HEADS, jax.nn.sigmoid(ba), gval)


def gdn_prep(zin, p, geo):
    m = zin.shape[0]
    tt, tps, ctx_t, nb = geo["tt"], geo["tiles_per_seq"], geo["ctx_tiles"], geo["batch"]
    specs = _halo_specs(GDN_QKV, COL_QKV // GDN_QKV, lambda j: j, tt, tps, m)
    specs.append(pl.BlockSpec((tt, 128), lambda b, j: (b * tps + j, COL_BA // 128)))
    full = lambda shape: pl.BlockSpec(shape, lambda b, j: (0,) * len(shape))
    specs += [full((4, GDN_QKV)), full((1, GDN_QKV)), full((1, 128)), full((1, 128))]
    pad = jnp.zeros((1, 128), F32)
    nega = pad.at[0, 8:16].set(-jnp.exp(p["gdn_a_log"].reshape(-1)))
    dtb = pad.at[0, 8:16].set(p["gdn_dt_bias"].reshape(-1))
    kern = functools.partial(_gdn_prep_kernel, ctx_t=ctx_t, n_t=tps)
    return pl.pallas_call(
        kern, grid=(nb, tps), in_specs=specs,
        out_specs=[pl.BlockSpec((tt, GDN_QKV), lambda b, j: (b * tps + j, 0)),
                   pl.BlockSpec((tt, 128), lambda b, j: (b * tps + j, 0))],
        out_shape=[jax.ShapeDtypeStruct((m, GDN_QKV), F32), jax.ShapeDtypeStruct((m, 128), F32)],
        scratch_shapes=[pltpu.VMEM((tt + 2 * HALO, GDN_QKV), F32)],
        compiler_params=_cp(("arbitrary", "arbitrary")), name="gdn_prep",
    )(zin, zin, zin, zin, p["gdn_conv_w"], p["gdn_conv_b"].reshape(1, -1), nega, dtb)


def _gdn_kernel(xf_ref, bgf_ref, xb_ref, bgb_ref, of_ref, ob_ref, s_s, *, gb):
    j = pl.program_id(1)

    @pl.when(j == 0)
    def _():
        s_s[...] = jnp.zeros_like(s_s)

    c = GDN_CHUNK
    nq = GDN_HEADS * GDN_DK
    ii = lax.broadcasted_iota(jnp.int32, (c, c), 0)
    jj = lax.broadcasted_iota(jnp.int32, (c, c), 1)
    incl = ((ii >= jj), (ii <= jj))
    strict = ((ii > jj), (ii < jj))
    blk = jnp.right_shift(ii, 4) == jnp.right_shift(jj, 4)
    eye = (ii == jj).astype(F32)

    chains = []
    for g in range(gb):
        for d, (x_ref, bg_ref) in enumerate(((xf_ref, bgf_ref), (xb_ref, bgb_ref))):
            bg = bg_ref[g]
            gam_all = _mm_f32(incl[d].astype(F32), bg)
            gam_t = gam_all.T
            for hd in range(GDN_HEADS):
                cb = d * GDN_HEADS + hd
                cg = 2 * GDN_HEADS + cb
                chains.append(dict(
                    g=g, d=d, hd=hd,
                    q=x_ref[g, :, hd * GDN_DK:(hd + 1) * GDN_DK],
                    k=x_ref[g, :, nq + hd * GDN_DK:nq + (hd + 1) * GDN_DK],
                    v=x_ref[g, :, 2 * nq + hd * GDN_DV:2 * nq + (hd + 1) * GDN_DV],
                    beta=bg[:, cb:cb + 1], gam=gam_all[:, cg:cg + 1], gam_row=gam_t[cg:cg + 1, :]))

    def put(name, fn):
        vals = [fn(ch) for ch in chains]
        for ch, val in zip(chains, vals):
            ch[name] = val

    put("dec_incl", lambda ch: jnp.where(
        incl[ch["d"]], jnp.exp(jnp.where(incl[ch["d"]], ch["gam"] - ch["gam_row"], 0.0)), 0.0))
    put("dec_strict", lambda ch: jnp.where(strict[ch["d"]], ch["dec_incl"], 0.0))
    put("g_last", lambda ch: ch["gam"][0:1, :] if ch["d"] == 1 else ch["gam"][c - 1:c, :])
    put("eg", lambda ch: jnp.exp(ch["gam"]))
    put("kb", lambda ch: ch["k"] * ch["beta"])
    put("a", lambda ch: _mm_nt(ch["kb"], ch["k"]) * ch["dec_strict"])
    put("rhs", lambda ch: jnp.concatenate([ch["v"] * ch["beta"], ch["kb"] * ch["eg"]], axis=1))
    put("a_d", lambda ch: jnp.where(blk, ch["a"], 0.0))
    put("a_o", lambda ch: ch["a"] - ch["a_d"])
    put("a2", lambda ch: _mm(ch["a_d"], ch["a_d"]))
    put("a4", lambda ch: _mm(ch["a2"], ch["a2"]))
    put("a8", lambda ch: _mm(ch["a4"], ch["a4"]))
    put("dinv", lambda ch: eye - ch["a_d"])
    put("dinv", lambda ch: ch["dinv"] + _mm(ch["dinv"], ch["a2"]))
    put("dinv", lambda ch: ch["dinv"] + _mm(ch["dinv"], ch["a4"]))
    put("dinv", lambda ch: ch["dinv"] + _mm(ch["dinv"], ch["a8"]))
    put("n1", lambda ch: _mm(ch["dinv"], ch["a_o"]))
    put("n2", lambda ch: _mm(ch["n1"], ch["n1"]))
    put("x", lambda ch: _mm(ch["dinv"], ch["rhs"]))
    put("x", lambda ch: ch["x"] + _mm(ch["n2"], ch["x"]))
    put("x", lambda ch: ch["x"] - _mm(ch["n1"], ch["x"]))
    put("qk", lambda ch: _mm_nt(ch["q"], ch["k"]) * ch["dec_incl"])
    put("s", lambda ch: s_s[ch["g"], ch["d"], ch["hd"]])
    put("v_new", lambda ch: ch["x"][:, :GDN_DV] - _mm(ch["x"][:, GDN_DV:], ch["s"]))
    put("o", lambda ch: _mm(ch["q"] * ch["eg"], ch["s"]) + _mm(ch["qk"], ch["v_new"]))
    put("s_new", lambda ch: ch["s"] * jnp.exp(ch["g_last"])
        + _mm_tn(ch["k"] * jnp.exp(ch["g_last"] - ch["gam"]), ch["v_new"]))
    for ch in chains:
        o_ref = of_ref if ch["d"] == 0 else ob_ref
        o_ref[ch["g"], :, ch["hd"] * GDN_DV:(ch["hd"] + 1) * GDN_DV] = ch["o"]
        s_s[ch["g"], ch["d"], ch["hd"]] = ch["s_new"]


def gdn_scan(qkvn, bg, geo):
    m = qkvn.shape[0]
    c = GDN_CHUNK
    nb, s_len, ctx = geo["batch"], geo["s_len"], geo["ctx"]
    gb = 2 if nb % 2 == 0 else 1
    n_c, ctx_c = s_len // c, ctx // c
    fwd = lambda b, j: (b, j, 0)
    bwd = lambda b, j: (b, _rev_tile(j, ctx_c, n_c), 0)
    w = GDN_HEADS * GDN_DV
    kern = functools.partial(_gdn_kernel, gb=gb)
    qkv3 = qkvn.reshape(nb, s_len, GDN_QKV)
    bg3 = bg.reshape(nb, s_len, 128)
    of, ob = pl.pallas_call(
        kern, grid=(nb // gb, n_c),
        in_specs=[pl.BlockSpec((gb, c, GDN_QKV), fwd), pl.BlockSpec((gb, c, 128), fwd),
                  pl.BlockSpec((gb, c, GDN_QKV), bwd), pl.BlockSpec((gb, c, 128), bwd)],
        out_specs=[pl.BlockSpec((gb, c, w), fwd), pl.BlockSpec((gb, c, w), bwd)],
        out_shape=[jax.ShapeDtypeStruct((nb, s_len, w), F32)] * 2,
        scratch_shapes=[pltpu.VMEM((gb, 2, GDN_HEADS, GDN_DK, GDN_DV), F32)],
        compiler_params=_cp(("arbitrary", "arbitrary")), name="gdn_scan",
    )(qkv3, bg3, qkv3, bg3)
    return of.reshape(m, w), ob.reshape(m, w)


def _fourier_kernel(c_ref, s_ref, u_ref, cc_ref, sc_ref, o_ref, *, scale):
    u = u_ref[0]
    a = jnp.dot(c_ref[...], u, preferred_element_type=F32)
    b = jnp.dot(s_ref[...], u, preferred_element_type=F32)
    o = _mm(a, cc_ref[...]) - _mm(b, sc_ref[...])
    o_ref[0] = o * scale


def _dft_mats(t):
    idx = jnp.arange(t, dtype=jnp.int32)
    ang = ((idx[:, None] * idx[None, :]) % t).astype(F32) * (2.0 * math.pi / t)
    return jnp.cos(ang).astype(BF16), jnp.sin(ang).astype(BF16)


def fourier(u, chan_c, chan_s):
    nb, t, w = u.shape
    cm, sm = _dft_mats(t)
    tf = _pick(t, (512, 256, 128))
    kern = functools.partial(_fourier_kernel, scale=1.0 / math.sqrt(t * FFT_GROUP))
    return pl.pallas_call(
        kern, grid=(t // tf, nb),
        in_specs=[pl.BlockSpec((tf, t), lambda f, b: (f, 0)), pl.BlockSpec((tf, t), lambda f, b: (f, 0)),
                  pl.BlockSpec((1, t, w), lambda f, b: (b, 0, 0)),
                  pl.BlockSpec((w, w), lambda f, b: (0, 0)), pl.BlockSpec((w, w), lambda f, b: (0, 0))],
        out_specs=pl.BlockSpec((1, tf, w), lambda f, b: (b, f, 0)),
        out_shape=jax.ShapeDtypeStruct((nb, t, w), F32),
        compiler_params=_cp(("arbitrary", "arbitrary")), name="fourier",
    )(cm, sm, u, chan_c, chan_s)


def _mla_prep_kernel(cq_ref, ckv_ref, misc_ref, cos_ref, sin_ref, gq_ref, gkv_ref, wq_ref, wkv_ref,
                     q_ref, k_ref, v_ref, *, q_scale):
    def rms(x, g):
        return x * lax.rsqrt(jnp.mean(x * x, axis=-1, keepdims=True) + NORM_EPS) * g

    cos = cos_ref[...]
    sin = sin_ref[...]
    qa = _mm(rms(cq_ref[...], gq_ref[...]), wq_ref[...])
    kva = _mm(rms(ckv_ref[...], gkv_ref[...]), wkv_ref[...])
    misc = misc_ref[...]
    k_rope = misc[:, :MLA_ROPE] * cos + misc[:, MLA_ROPE:] * sin
    zpad = jnp.zeros((cos.shape[0], MLA_QK_PAD - MLA_NOPE - MLA_ROPE), F32)
    for hd in range(MLA_HEADS):
        qh = qa[:, hd * 256:(hd + 1) * 256]
        q_rope = qh[:, MLA_NOPE:MLA_NOPE + MLA_ROPE] * cos + qh[:, MLA_NOPE + MLA_ROPE:] * sin
        q_ref[0, hd] = (jnp.concatenate([qh[:, :MLA_NOPE], q_rope, zpad], axis=1) * q_scale).astype(BF16)
        kh = kva[:, hd * 256:hd * 256 + MLA_NOPE]
        k_ref[0, hd] = jnp.concatenate([kh, k_rope, zpad], axis=1).astype(BF16)
        v_ref[0, hd] = kva[:, hd * 256 + MLA_NOPE:(hd + 1) * 256].astype(BF16)


def mla_prep(zin, cos_t, sin_t, p, geo):
    tt, tps, nb, s_len = geo["tt"], geo["tiles_per_seq"], geo["batch"], geo["s_len"]
    row = lambda b, j: b * tps + j
    full = lambda shape: pl.BlockSpec(shape, lambda b, j: (0,) * len(shape))
    ctx_t = geo["ctx_tiles"]
    hs = lambda w: pl.BlockSpec((1, MLA_HEADS, tt, w),
                                lambda b, j: (b, 0, jnp.where(j < ctx_t, j + (tps - ctx_t), j - ctx_t), 0))
    kern = functools.partial(_mla_prep_kernel, q_scale=(MLA_NOPE + MLA_ROPE) ** -0.5)
    return pl.pallas_call(
        kern, grid=(nb, tps),
        in_specs=[pl.BlockSpec((tt, MLA_Q_RANK), lambda b, j: (row(b, j), COL_CQ // MLA_Q_RANK)),
                  pl.BlockSpec((tt, MLA_KV_RANK), lambda b, j: (row(b, j), COL_CKV // MLA_KV_RANK)),
                  pl.BlockSpec((tt, 128), lambda b, j: (row(b, j), COL_MISC // 128)),
                  pl.BlockSpec((tt, MLA_ROPE), lambda b, j: (j, 0)), pl.BlockSpec((tt, MLA_ROPE), lambda b, j: (j, 0)),
                  full((1, MLA_Q_RANK)), full((1, MLA_KV_RANK)),
                  full((MLA_Q_RANK, MLA_HEADS * 256)), full((MLA_KV_RANK, MLA_HEADS * 256))],
        out_specs=[hs(MLA_QK_PAD), hs(MLA_QK_PAD), hs(MLA_V)],
        out_shape=[jax.ShapeDtypeStruct((nb, MLA_HEADS, s_len, MLA_QK_PAD), BF16),
                   jax.ShapeDtypeStruct((nb, MLA_HEADS, s_len, MLA_QK_PAD), BF16),
                   jax.ShapeDtypeStruct((nb, MLA_HEADS, s_len, MLA_V), BF16)],
        compiler_params=_cp(("arbitrary", "arbitrary")), name="mla_prep",
    )(zin, zin, zin, cos_t, sin_t, p["mla_q_norm_g"].reshape(1, -1), p["mla_kv_norm_g"].reshape(1, -1),
      p["mla_wq"], p["mla_wkv"])


def _attn_kernel(q_ref, k_ref, v_ref, o_ref):
    s = lax.dot_general(q_ref[0, 0], k_ref[0, 0], (((1,), (1,)), ((), ())), preferred_element_type=F32)
    pr = jnp.exp(s - jnp.max(s, axis=-1, keepdims=True))
    den = jnp.sum(pr, axis=-1, keepdims=True)
    o_ref[0] = jnp.dot(pr.astype(BF16), v_ref[0, 0], preferred_element_type=F32) / den


def mla_attention(q, k, v, geo):
    nb, nh, s_len, _ = q.shape
    n_ctx = geo["ctx"]
    n_lat = s_len - n_ctx
    assert n_lat % n_ctx == 0

    def call(tq, q_block0, n_q, key_rows, key_block):
        return pl.pallas_call(
            _attn_kernel, grid=(nb, nh, n_q // tq),
            in_specs=[pl.BlockSpec((1, 1, tq, MLA_QK_PAD), lambda b, h, i: (b, h, q_block0 + i, 0)),
                      pl.BlockSpec((1, 1, key_rows, MLA_QK_PAD), lambda b, h, i: (b, h, key_block, 0)),
                      pl.BlockSpec((1, 1, key_rows, MLA_V), lambda b, h, i: (b, h, key_block, 0))],
            out_specs=pl.BlockSpec((1, tq, MLA_V), lambda b, h, i: (b, i, h)),
            out_shape=jax.ShapeDtypeStruct((nb, n_q, nh * MLA_V), F32),
            compiler_params=_cp(("arbitrary", "arbitrary", "arbitrary")), name="mla_attention",
        )(q, k, v)

    tq_lat = _pick(n_lat, (256, 128))
    tq_ctx = _pick(n_ctx, (256, 128, 64))
    o_lat = call(tq_lat, 0, n_lat, s_len, 0)
    o_ctx = call(tq_ctx, n_lat // tq_ctx, n_ctx, n_ctx, n_lat // n_ctx)
    return o_lat, o_ctx


def _merge_kernel(hn_ref, hf_ref, hb_ref, lg_ref, of_ref, ob_ref, z_ref, gg_ref, yc_ref, yd_ref, wg_ref, wb_ref,
                  wo_ref, o_ref, y_s, acc_s, *, n_j):
    j = pl.program_id(1)

    @pl.when(j == 0)
    def _():
        lg = lg_ref[...]
        gelu = 0.5 * lg * (1.0 + jnp.tanh(math.sqrt(2.0 / math.pi) * (lg + 0.044715 * (lg * lg * lg))))
        y_s[0] = (gelu * (hf_ref[...] + hb_ref[...])).astype(BF16)
        for hd in range(GDN_HEADS):
            sl = slice(hd * GDN_DV, (hd + 1) * GDN_DV)
            o = of_ref[:, sl] + ob_ref[:, sl]
            o = o * lax.rsqrt(jnp.mean(o * o, axis=-1, keepdims=True) + NORM_EPS) * gg_ref[...]
            z = z_ref[:, sl]
            y_s[1, :, sl] = (o * (z * jax.nn.sigmoid(z))).astype(BF16)
        y_s[2] = yc_ref[...].astype(BF16)
        y_s[3] = yd_ref[...].astype(BF16)
        acc_s[...] = jnp.zeros_like(acc_s)

    hn = hn_ref[...]
    merged = None
    for i in range(N_BRANCH):
        gate = jax.nn.sigmoid(jnp.dot(hn, wg_ref[i], preferred_element_type=F32))
        term = gate * jnp.dot(y_s[i], wb_ref[i], preferred_element_type=F32)
        merged = term if merged is None else merged + term
    acc_s[...] += jnp.dot(merged.astype(BF16), wo_ref[...], preferred_element_type=F32)

    @pl.when(j == n_j - 1)
    def _():
        o_ref[...] = acc_s[...]


def merge(hn, zin, hf, hb, of, ob, yc, yd, p):
    m, d = hn.shape
    tm = _pick(m, (512, 256, 128))
    tn = 256
    n_j = d // tn
    w = BRANCH_WIDTH
    tok = lambda a_w, cb: pl.BlockSpec((tm, a_w), lambda i, j: (i, cb))
    kern = functools.partial(_merge_kernel, n_j=n_j)
    return pl.pallas_call(
        kern, grid=(m // tm, n_j),
        in_specs=[tok(d, 0), tok(w, 0), tok(w, 0), tok(w, COL_LG // w), tok(w, 0), tok(w, 0), tok(w, COL_Z // w),
                  pl.BlockSpec((1, GDN_DV), lambda i, j: (0, 0)), tok(w, 0), tok(w, 0),
                  pl.BlockSpec((N_BRANCH, d, tn), lambda i, j: (0, 0, j)),
                  pl.BlockSpec((N_BRANCH, w, tn), lambda i, j: (0, 0, j)),
                  pl.BlockSpec((tn, d), lambda i, j: (j, 0))],
        out_specs=pl.BlockSpec((tm, d), lambda i, j: (i, 0)),
        out_shape=jax.ShapeDtypeStruct((m, d), F32),
        scratch_shapes=[pltpu.VMEM((N_BRANCH, tm, w), BF16), pltpu.VMEM((tm, d), F32)],
        compiler_params=_cp(("arbitrary", "arbitrary")), name="merge",
    )(hn, hf, hb, zin, of, ob, zin, p["gdn_norm_g"].reshape(1, -1), yc, yd, p["w_gate"], p["w_branch"], p["w_out"])


def _moe_kernel(be_ref, nv_ref, x_ref, g_ref, w1_ref, b1_ref, w2_ref, b2_ref, o_ref, w1_s, w2_s):
    i = pl.program_id(0)
    new_expert = jnp.logical_or(i == 0, be_ref[i] != be_ref[jnp.maximum(i - 1, 0)])

    @pl.when(new_expert)
    def _():
        w1_s[...] = w1_ref[0, 0].astype(BF16)
        w2_s[...] = w2_ref[0, 0].astype(BF16)

    @pl.when(i < nv_ref[0])
    def _():
        h = jnp.dot(x_ref[...], w1_s[...], preferred_element_type=F32) + b1_ref[0, 0]
        h_glu = jnp.minimum(h[:, :D_EXPERT], SWIGLU_LIMIT)
        h_lin = jnp.clip(h[:, D_EXPERT:], -SWIGLU_LIMIT, SWIGLU_LIMIT)
        act = h_glu * jax.nn.sigmoid(SWIGLU_ALPHA * h_glu) * (h_lin + 1.0)
        y = jnp.dot(act.astype(BF16), w2_s[...], preferred_element_type=F32) + b2_ref[0, 0]
        o_ref[...] = (y * g_ref[...]).astype(o_ref.dtype)

    @pl.when(i >= nv_ref[0])
    def _():
        o_ref[...] = jnp.zeros_like(o_ref)


def moe_experts(xg, gate, block_e, n_valid, layer, w1, b1, w2, b2):
    p_rows, d = xg.shape
    bm = MOE_ROWS
    n_blocks = p_rows // bm
    depth, ne, _, dh = w1.shape
    grid_spec = pltpu.PrefetchScalarGridSpec(
        num_scalar_prefetch=2, grid=(n_blocks,),
        in_specs=[pl.BlockSpec((bm, d), lambda i, be, nv: (i, 0)),
                  pl.BlockSpec((bm, 1), lambda i, be, nv: (i, 0)),
                  pl.BlockSpec((1, 1, d, dh), lambda i, be, nv: (layer, be[i], 0, 0)),
                  pl.BlockSpec((1, 1, 1, dh), lambda i, be, nv: (layer, be[i], 0, 0)),
                  pl.BlockSpec((1, 1, D_EXPERT, d), lambda i, be, nv: (layer, be[i], 0, 0)),
                  pl.BlockSpec((1, 1, 1, d), lambda i, be, nv: (layer, be[i], 0, 0))],
        out_specs=pl.BlockSpec((bm, d), lambda i, be, nv: (i, 0)),
        scratch_shapes=[pltpu.VMEM((d, dh), BF16), pltpu.VMEM((D_EXPERT, d), BF16)])
    return pl.pallas_call(
        _moe_kernel, grid_spec=grid_spec, out_shape=jax.ShapeDtypeStruct((p_rows, d), BF16),
        compiler_params=_cp(("arbitrary",)), name="moe_experts",
    )(block_e, n_valid, xg, gate.reshape(p_rows, 1), w1, b1.reshape(depth, ne, 1, dh), w2, b2.reshape(depth, ne, 1, d))


def moe_ffn(hn, idx, gates, rank, counts, layer, w1, b1, w2, b2):
    n, d = hn.shape
    bm = MOE_ROWS
    nk = n * TOP_K
    counts = counts.astype(jnp.int32)
    padded = (counts + bm - 1) // bm * bm
    pend = jnp.cumsum(padded)
    pstart = pend - padded
    experts = jnp.arange(N_EXPERTS, dtype=jnp.int32)
    pos = rank + jnp.sum(jnp.where(idx[..., None] == experts, pstart, 0), axis=-1)
    n_blocks = (nk + N_EXPERTS * (bm - 1) + bm - 1) // bm
    p_rows = n_blocks * bm
    real = (idx.reshape(nk) << KEY_EXPERT_SHIFT) + jnp.arange(nk, dtype=jnp.int32)
    fill = jnp.arange(bm - 1, dtype=jnp.int32)
    pad_keys = jnp.where(fill[None, :] < (padded - counts)[:, None],
                         (experts[:, None] << KEY_EXPERT_SHIFT) + (1 << KEY_PAD_BIT) + fill[None, :],
                         jnp.iinfo(jnp.int32).max)
    n_fill = p_rows - nk
    tail = jnp.full((n_fill - N_EXPERTS * (bm - 1),), jnp.iinfo(jnp.int32).max, jnp.int32)
    keys, buf_gate = lax.sort((jnp.concatenate([real, pad_keys.reshape(-1), tail]),
                               jnp.concatenate([gates.reshape(nk), jnp.zeros((n_fill,), F32)])), num_keys=1)
    is_pad = (keys >> KEY_PAD_BIT) & 1
    buf_tok = jnp.where(is_pad == 1, 0, (keys & ((1 << KEY_PAD_BIT) - 1)) // TOP_K)
    block_start = jnp.arange(n_blocks, dtype=jnp.int32) * bm
    block_e = jnp.minimum(jnp.sum((pend[None, :] <= block_start[:, None]).astype(jnp.int32), axis=1), N_EXPERTS - 1)
    n_valid = (pend[-1:] // bm).astype(jnp.int32)
    y = moe_experts(hn[buf_tok], buf_gate, block_e, n_valid, layer, w1, b1, w2, b2)
    return y[pos.reshape(nk)]


def _rope_rot(w):
    h = MLA_ROPE // 4
    a, b, c, d = w[..., :h], w[..., h:2 * h], w[..., 2 * h:3 * h], w[..., 3 * h:]
    return jnp.concatenate([-b, a, -d, c], axis=-1)


def _prep_layer(w_in_all, layer, mla_w_uq, mla_w_ukv):
    d = w_in_all.shape[1]
    o = np.cumsum([0, 512, 512, GDN_QKV, 512, 8, 8, 512, 512, 256, 64, N_BRANCH * d])
    lu, lg, qkv, z, be, al, fu, cq, ckv, kr, mg = [w_in_all[layer, :, o[i]:o[i + 1]] for i in range(11)]
    pad = jnp.zeros((d, 128 - 16), w_in_all.dtype)
    w_main = jnp.concatenate([qkv, lu, lg, z, fu, cq, ckv, kr, _rope_rot(kr), be, al, pad], axis=1).astype(BF16)
    w_gate = jnp.transpose(mg.reshape(d, N_BRANCH, d), (1, 0, 2)).astype(BF16)
    uq = mla_w_uq.reshape(MLA_Q_RANK, MLA_HEADS, MLA_NOPE + MLA_ROPE)
    wq = jnp.concatenate([uq, _rope_rot(uq[..., MLA_NOPE:])], axis=-1).reshape(MLA_Q_RANK, MLA_HEADS * 256)
    return w_main, w_gate, wq.astype(BF16), mla_w_ukv.astype(BF16)


def _rope_tables(ctx, n_lat):
    half = MLA_ROPE // 4
    inv = jnp.power(ROPE_BASE, -jnp.arange(half, dtype=F32) / half)
    t = jnp.arange(n_lat)
    row = (t // GRID_W).astype(F32)[:, None] * inv
    col = (t % GRID_W).astype(F32)[:, None] * inv
    cos = jnp.concatenate([jnp.cos(row), jnp.cos(row), jnp.cos(col), jnp.cos(col)], axis=1)
    sin = jnp.concatenate([jnp.sin(row), jnp.sin(row), jnp.sin(col), jnp.sin(col)], axis=1)
    cos = jnp.concatenate([jnp.ones((ctx, MLA_ROPE), F32), cos], axis=0)
    sin = jnp.concatenate([jnp.zeros((ctx, MLA_ROPE), F32), sin], axis=0)
    return cos, sin


def _chan_dft():
    idx = np.arange(FFT_GROUP)
    ang = 2.0 * np.pi * ((idx[:, None] * idx[None, :]) % FFT_GROUP) / FFT_GROUP
    eye = np.eye(FFT_GROUPS)
    return (jnp.asarray(np.kron(eye, np.cos(ang)), BF16), jnp.asarray(np.kron(eye, np.sin(ang)), BF16))


def kernel(x, c, ctx, c_ctx, w_ada, b_ada, norm1_g, norm2_g, w_in, lru_conv_w, lru_conv_b, lru_w_a, lru_b_a, lru_w_x, lru_b_x, lru_lambda, gdn_conv_w, gdn_conv_b, gdn_a_log, gdn_dt_bias, gdn_norm_g, mla_q_norm_g, mla_kv_norm_g, mla_w_uq, mla_w_ukv, w_branch, w_out, router_w, router_b, exp_w1, exp_b1, exp_w2, exp_b2, final_norm_g):
    nb, n_lat, d = x.shape
    n_ctx = ctx.shape[1]
    depth = w_ada.shape[0]
    s_len = n_ctx + n_lat
    m = nb * s_len
    tt = _pick(math.gcd(n_ctx, n_lat), (256, 128, 64))
    geo = dict(tt=tt, tiles_per_seq=s_len // tt, ctx_tiles=n_ctx // tt, batch=nb, s_len=s_len, ctx=n_ctx)

    mod_rows = 8 * ((nb + 1 + 7) // 8)
    cmat = jnp.zeros((mod_rows, d), F32).at[:nb].set(c).at[nb].set(c_ctx)
    mod_all = ada_table(cmat, w_ada, b_ada).reshape(depth, mod_rows * N_MOD, 1, d)

    cos_t, sin_t = _rope_tables(n_ctx, n_lat)
    chan_c, chan_s = _chan_dft()
    h = jnp.concatenate([ctx, x], axis=1).reshape(m, d)

    delta, prev_mod, gate_k = None, None, None
    for l in range(depth):
        w_main, w_gate, wq, wkv = _prep_layer(w_in, l, mla_w_uq[l], mla_w_ukv[l])
        p = dict(lru_conv_w=lru_conv_w[l], lru_conv_b=lru_conv_b[l], lru_w_a=lru_w_a[l].astype(BF16),
                 lru_b_a=lru_b_a[l], lru_w_x=lru_w_x[l].astype(BF16), lru_b_x=lru_b_x[l], lru_lambda=lru_lambda[l],
                 gdn_conv_w=gdn_conv_w[l], gdn_conv_b=gdn_conv_b[l], gdn_a_log=gdn_a_log[l],
                 gdn_dt_bias=gdn_dt_bias[l], gdn_norm_g=gdn_norm_g[l], mla_q_norm_g=mla_q_norm_g[l],
                 mla_kv_norm_g=mla_kv_norm_g[l], mla_wq=wq, mla_wkv=wkv, w_gate=w_gate,
                 w_branch=w_branch[l].astype(BF16), w_out=w_out[l].astype(BF16))
        mod = mod_all[l]
        h, hn = norm_mod(h, norm1_g[l], geo, mod=mod, shift_k=0, scale_k=1, delta=delta, gate_k=gate_k,
                         prev_mod=prev_mod)
        zin = matmul(hn, w_main)
        hf, hb = lru_scan(zin, p, geo)
        qkvn, bg = gdn_prep(zin, p, geo)
        of, ob = gdn_scan(qkvn, bg, geo)
        fu = zin[:, COL_FU:COL_FU + FFT_WIDTH].astype(BF16).reshape(nb, s_len, FFT_WIDTH)
        yc = jnp.concatenate([fourier(fu[:, :n_ctx], chan_c, chan_s), fourier(fu[:, n_ctx:], chan_c, chan_s)],
                             axis=1).reshape(m, FFT_WIDTH)
        q, k, v = mla_prep(zin, cos_t, sin_t, p, geo)
        yd_lat, yd_ctx = mla_attention(q, k, v, geo)
        yd = jnp.concatenate([yd_ctx, yd_lat], axis=1).reshape(m, MLA_HEADS * MLA_V)
        mixed = merge(hn, zin, hf, hb, of, ob, yc, yd, p)
        h, hn2, idx, gates, rank, counts = norm_mod(
            h, norm2_g[l], geo, mod=mod, shift_k=3, scale_k=4, delta=("dense", mixed), gate_k=2, prev_mod=mod,
            router=(router_w[l], router_b[l]))
        ygath = moe_ffn(hn2, idx, gates, rank, counts[0], l, exp_w1, exp_b1, exp_w2, exp_b2)
        delta, prev_mod, gate_k = ("moe", ygath), mod, 5
    (out,) = norm_mod(h, final_norm_g, geo, delta=delta, gate_k=5, prev_mod=prev_mod, emit_h=False,
                      out_dtype=F32, lat_only=True)
    return out.reshape(nb, n_lat, d)
```

```python
import functools
import math

import jax
import jax.numpy as jnp
import numpy as np
from jax import lax
from jax.experimental import pallas as pl
from jax.experimental.pallas import tpu as pltpu

F32 = jnp.float32
BF16 = jnp.bfloat16

GRID_W = 64
NORM_EPS = 1e-6
N_MOD = 6
LRU_WIDTH = 512
LRU_BLOCKS = 4
LRU_BLOCK = 128
LRU_C = 8.0
GDN_HEADS = 4
GDN_DK = 128
GDN_DV = 128
GDN_CHUNK = 64
GDN_QKV = GDN_HEADS * (2 * GDN_DK + GDN_DV)
FFT_GROUPS = 4
FFT_GROUP = 128
FFT_WIDTH = 512
MLA_HEADS = 4
MLA_Q_RANK = 512
MLA_KV_RANK = 256
MLA_NOPE = 128
MLA_ROPE = 64
MLA_V = 128
MLA_QK_PAD = 256
ROPE_BASE = 10000.0
N_BRANCH = 4
BRANCH_WIDTH = 512
N_EXPERTS = 32
TOP_K = 4
D_EXPERT = 512
SWIGLU_LIMIT = 7.0
SWIGLU_ALPHA = 1.702
MOE_ROWS = 512
KEY_EXPERT_SHIFT = 20
KEY_PAD_BIT = 19

COL_QKV = 0
COL_LU = 1536
COL_LG = 2048
COL_Z = 2560
COL_FU = 3072
COL_CQ = 3584
COL_CKV = 4096
COL_MISC = 4352
COL_BA = 4480
ZIN_W = 4608

VMEM_LIMIT = 56 * 1024 * 1024
HALO = 8


def _cp(sem, vmem=VMEM_LIMIT):
    return pltpu.CompilerParams(dimension_semantics=sem, vmem_limit_bytes=vmem)


def _pick(n, cands):
    for c in cands:
        if n % c == 0:
            return c
    raise ValueError(f"no tile for {n} in {cands}")


def _mm(a, b):
    return jnp.dot(a.astype(BF16), b.astype(BF16), preferred_element_type=F32)


def _mm_nt(a, b):
    return lax.dot_general(a.astype(BF16), b.astype(BF16), (((1,), (1,)), ((), ())), preferred_element_type=F32)


def _mm_tn(a, b):
    return lax.dot_general(a.astype(BF16), b.astype(BF16), (((0,), (0,)), ((), ())), preferred_element_type=F32)


def _mm_f32(a, b):
    return jnp.dot(a, b, preferred_element_type=F32, precision=lax.Precision.HIGHEST)


def _softplus(y):
    return jnp.maximum(y, 0.0) + jnp.log1p(jnp.exp(-jnp.abs(y)))


def _ada_kernel(c_ref, w_ref, b_ref, o_ref):
    cv = c_ref[...]
    s = cv * jax.nn.sigmoid(cv)
    o_ref[0] = _mm(s, w_ref[0]) + b_ref[0]


def ada_table(cmat, w_ada, b_ada):
    depth, d, n = w_ada.shape
    rows = cmat.shape[0]
    tn = _pick(n, (1024, 512, 256, 128))
    return pl.pallas_call(
        _ada_kernel,
        grid=(depth, n // tn),
        in_specs=[pl.BlockSpec((rows, d), lambda l, j: (0, 0)),
                  pl.BlockSpec((1, d, tn), lambda l, j: (l, 0, j)),
                  pl.BlockSpec((1, 1, tn), lambda l, j: (l, 0, j))],
        out_specs=pl.BlockSpec((1, rows, tn), lambda l, j: (l, 0, j)),
        out_shape=jax.ShapeDtypeStruct((depth, rows, n), F32),
        compiler_params=_cp(("arbitrary", "arbitrary")),
        name="ada_table",
    )(cmat, w_ada, b_ada.reshape(depth, 1, n))


def _norm_kernel(*refs, has_delta, has_mod, has_router, emit_h):
    it = iter(refs)
    h_ref = next(it)
    if has_delta:
        d_ref = next(it)
        gate_ref = next(it)
    g_ref = next(it)
    if has_mod:
        shift_ref = next(it)
        scale_ref = next(it)
    if has_router:
        rw_ref = next(it)
        rb_ref = next(it)
    if emit_h:
        ho_ref = next(it)
    y_ref = next(it)
    if has_router:
        idx_ref = next(it)
        gt_ref = next(it)
        rank_ref = next(it)
        cnt_ref = next(it)

    h = h_ref[...]
    if has_delta:
        h = h + gate_ref[0] * d_ref[...]
    if emit_h:
        ho_ref[...] = h
    y = h * lax.rsqrt(jnp.mean(h * h, axis=-1, keepdims=True) + NORM_EPS) * g_ref[...]
    if has_mod:
        y = y * (1.0 + scale_ref[0]) + shift_ref[0]
    y_ref[...] = y.astype(y_ref.dtype)
    if has_router:
        @pl.when(pl.program_id(0) == 0)
        def _():
            cnt_ref[...] = jnp.zeros_like(cnt_ref)

        logit = _mm_f32(y, rw_ref[...]) + rb_ref[...]
        tt, ne = logit.shape
        lane = lax.broadcasted_iota(jnp.int32, (tt, ne), 1).astype(F32)
        vals, hots = [], []
        for kk in range(TOP_K):
            top = jnp.max(logit, axis=-1, keepdims=True)
            arg = jnp.min(jnp.where(logit == top, lane, float(ne)), axis=-1, keepdims=True)
            hot = lane == arg
            idx_ref[:, kk:kk + 1] = arg.astype(jnp.int32)
            vals.append(top)
            hots.append(hot)
            logit = jnp.where(hot, -jnp.inf, logit)
        exps = [jnp.exp(v - vals[0]) for v in vals]
        den = exps[0] + exps[1] + exps[2] + exps[3]
        for kk in range(TOP_K):
            gt_ref[:, kk:kk + 1] = exps[kk] / den
        hot_all = hots[0].astype(F32) + hots[1].astype(F32) + hots[2].astype(F32) + hots[3].astype(F32)
        ii = lax.broadcasted_iota(jnp.int32, (tt, tt), 0)
        jj = lax.broadcasted_iota(jnp.int32, (tt, tt), 1)
        before = _mm((jj < ii).astype(F32), hot_all) + cnt_ref[...]
        for kk in range(TOP_K):
            rank_ref[:, kk:kk + 1] = jnp.sum(jnp.where(hots[kk], before, 0.0), axis=-1, keepdims=True).astype(jnp.int32)
        cnt_ref[...] += jnp.sum(hot_all, axis=0, keepdims=True)


def norm_mod(h, g, geo, *, mod=None, shift_k=None, scale_k=None, delta=None, gate_k=None, prev_mod=None,
             router=None, emit_h=True, out_dtype=None, lat_only=False):
    m, d = h.shape
    out_dtype = BF16 if out_dtype is None else out_dtype
    tt, tps, ctx_t, nb = geo["tt"], geo["tiles_per_seq"], geo["ctx_tiles"], geo["batch"]
    if lat_only:
        lat_t = tps - ctx_t
        grid = (nb * lat_t,)
        in_row = lambda i: (i // lat_t) * tps + ctx_t + i % lat_t
        mod_row = lambda i: i // lat_t
        out_rows = nb * lat_t * tt
    else:
        grid = (m // tt,)
        in_row = lambda i: i
        mod_row = lambda i: jnp.where(i % tps < ctx_t, nb, i // tps)
        out_rows = m
    tok = lambda w: pl.BlockSpec((tt, w), lambda i: (in_row(i), 0))
    out_tok = lambda w: pl.BlockSpec((tt, w), lambda i: (i, 0))

    def mod_spec(k):
        return pl.BlockSpec((1, 1, d), lambda i: (mod_row(i) * N_MOD + k, 0, 0))

    args, specs = [h], [tok(d)]
    if delta is not None:
        args += [delta, prev_mod]
        specs += [tok(d), mod_spec(gate_k)]
    args.append(g.reshape(1, d))
    specs.append(pl.BlockSpec((1, d), lambda i: (0, 0)))
    if mod is not None:
        args += [mod, mod]
        specs += [mod_spec(shift_k), mod_spec(scale_k)]
    if router is not None:
        rw, rb = router
        ne = rw.shape[1]
        args += [rw, rb.reshape(1, ne)]
        specs += [pl.BlockSpec(rw.shape, lambda i: (0, 0)), pl.BlockSpec((1, ne), lambda i: (0, 0))]
    out_shape, out_specs = [], []
    if emit_h:
        out_shape.append(jax.ShapeDtypeStruct((out_rows, d), F32))
        out_specs.append(out_tok(d))
    out_shape.append(jax.ShapeDtypeStruct((out_rows, d), out_dtype))
    out_specs.append(out_tok(d))
    if router is not None:
        out_shape += [jax.ShapeDtypeStruct((out_rows, TOP_K), jnp.int32), jax.ShapeDtypeStruct((out_rows, TOP_K), F32),
                      jax.ShapeDtypeStruct((out_rows, TOP_K), jnp.int32), jax.ShapeDtypeStruct((1, ne), F32)]
        out_specs += [out_tok(TOP_K), out_tok(TOP_K), out_tok(TOP_K), pl.BlockSpec((1, ne), lambda i: (0, 0))]
    kern = functools.partial(_norm_kernel, has_delta=delta is not None, has_mod=mod is not None,
                             has_router=router is not None, emit_h=emit_h)
    return pl.pallas_call(kern, grid=grid, in_specs=specs, out_specs=out_specs, out_shape=out_shape,
                          compiler_params=_cp(("arbitrary",)), name="norm_mod")(*args)


def _matmul_kernel(x_ref, w_ref, o_ref):
    o_ref[...] = jnp.dot(x_ref[...], w_ref[...], preferred_element_type=F32).astype(o_ref.dtype)


def matmul(x, w, out_dtype=F32):
    m, k = x.shape
    n = w.shape[1]
    tm = _pick(m, (1024, 512, 256, 128))
    tn = _pick(n, (768, 512, 256, 128))
    return pl.pallas_call(
        _matmul_kernel,
        grid=(m // tm, n // tn),
        in_specs=[pl.BlockSpec((tm, k), lambda i, j: (i, 0)), pl.BlockSpec((k, tn), lambda i, j: (0, j))],
        out_specs=pl.BlockSpec((tm, tn), lambda i, j: (i, j)),
        out_shape=jax.ShapeDtypeStruct((m, n), out_dtype),
        compiler_params=_cp(("arbitrary", "arbitrary")),
        name="in_proj",
    )(x, w)


def _conv4(x, xp, xn, prev_ok, next_ok, cw_ref, cb_ref, ext_ref):
    tt = x.shape[0]
    ext_ref[0:HALO, :] = xp * prev_ok
    ext_ref[HALO:HALO + tt, :] = x
    ext_ref[HALO + tt:2 * HALO + tt, :] = xn * next_ok
    y = cb_ref[...] + cw_ref[2:3, :] * x
    y = y + cw_ref[0:1, :] * ext_ref[HALO - 2:HALO - 2 + tt, :]
    y = y + cw_ref[1:2, :] * ext_ref[HALO - 1:HALO - 1 + tt, :]
    y = y + cw_ref[3:4, :] * ext_ref[HALO + 1:HALO + 1 + tt, :]
    return y


def _seg_edges(tile, ctx_t, n_t):
    first = jnp.logical_or(tile == 0, tile == ctx_t)
    last = jnp.logical_or(tile == ctx_t - 1, tile == n_t - 1)
    return jnp.where(first, 0.0, 1.0).astype(F32), jnp.where(last, 0.0, 1.0).astype(F32)


def _rev_tile(j, ctx_t, n_t):
    return jnp.where(j < ctx_t, ctx_t - 1 - j, n_t - 1 - (j - ctx_t))


def _halo_specs(width, col_block, tile_of, tt, tps, m):
    r8 = tt // HALO
    last8 = m // HALO - 1

    def cur(b, j):
        return (b * tps + tile_of(j), col_block)

    def prev(b, j):
        return (jnp.maximum((b * tps + tile_of(j)) * r8 - 1, 0), col_block)

    def nxt(b, j):
        return (jnp.minimum((b * tps + tile_of(j) + 1) * r8, last8), col_block)

    return [pl.BlockSpec((tt, width), cur), pl.BlockSpec((HALO, width), prev), pl.BlockSpec((HALO, width), nxt)]


def _lru_kernel(xf_ref, xfp_ref, xfn_ref, xb_ref, xbp_ref, xbn_ref, cw_ref, cb_ref, wa_ref, ba_ref, wx_ref, bx_ref,
                lam_ref, hf_ref, hb_ref, ext_s, a_s, b_s, carry_s, *, tt, ctx_t, n_t):
    j = pl.program_id(1)

    @pl.when(j == 0)
    def _():
        carry_s[...] = jnp.zeros_like(carry_s)

    rows = lax.broadcasted_iota(jnp.int32, (HALO, LRU_WIDTH), 0)
    for d, (x_ref, xp_ref, xn_ref, out_ref) in enumerate(((xf_ref, xfp_ref, xfn_ref, hf_ref),
                                                          (xb_ref, xbp_ref, xbn_ref, hb_ref))):
        tile = j if d == 0 else _rev_tile(j, ctx_t, n_t)
        prev_ok, next_ok = _seg_edges(tile, ctx_t, n_t)
        xc = _conv4(x_ref[...], xp_ref[...], xn_ref[...], prev_ok, next_ok, cw_ref, cb_ref, ext_s)
        sp = _softplus(-lam_ref[d:d + 1, :])
        for g in range(LRU_BLOCKS):
            sl = slice(g * LRU_BLOCK, (g + 1) * LRU_BLOCK)
            xg = xc[:, sl]
            r = jax.nn.sigmoid(_mm(xg, wa_ref[d, g]) + ba_ref[d:d + 1, sl])
            ig = jax.nn.sigmoid(_mm(xg, wx_ref[d, g]) + bx_ref[d:d + 1, sl])
            log_a = -LRU_C * r * sp[:, sl]
            a = jnp.exp(log_a)
            a_s[d, :, sl] = a
            b_s[d, :, sl] = jnp.sqrt(-jnp.tanh(log_a) * (a * a + 1.0)) * (ig * xg)

        n_grp = tt // HALO

        def group(gi, carry, d=d, out_ref=out_ref):
            g0 = gi if d == 0 else n_grp - 1 - gi
            off = pl.multiple_of(g0 * HALO, HALO)
            av = a_s[d, pl.ds(off, HALO), :]
            bv = b_s[d, pl.ds(off, HALO), :]
            for s in (1, 2, 4):
                shift = s if d == 0 else HALO - s
                a_sh = pltpu.roll(av, shift, 0)
                b_sh = pltpu.roll(bv, shift, 0)
                ok = (rows >= s) if d == 0 else (rows < HALO - s)
                bv = jnp.where(ok, av * b_sh + bv, bv)
                av = jnp.where(ok, av * a_sh, av)
            hv = bv + av * carry
            out_ref[pl.ds(off, HALO), :] = hv
            return hv[HALO - 1:HALO, :] if d == 0 else hv[0:1, :]

        carry_s[d:d + 1, :] = lax.fori_loop(0, n_grp, group, carry_s[d:d + 1, :])


def lru_scan(zin, p, geo):
    m = zin.shape[0]
    tt, tps, ctx_t, nb = geo["tt"], geo["tiles_per_seq"], geo["ctx_tiles"], geo["batch"]
    cb = COL_LU // LRU_WIDTH
    fwd = _halo_specs(LRU_WIDTH, cb, lambda j: j, tt, tps, m)
    bwd = _halo_specs(LRU_WIDTH, cb, lambda j: _rev_tile(j, ctx_t, tps), tt, tps, m)
    full = lambda shape: pl.BlockSpec(shape, lambda b, j: (0,) * len(shape))
    w_specs = [full((4, LRU_WIDTH)), full((1, LRU_WIDTH)), full((2, LRU_BLOCKS, LRU_BLOCK, LRU_BLOCK)),
               full((2, LRU_WIDTH)), full((2, LRU_BLOCKS, LRU_BLOCK, LRU_BLOCK)), full((2, LRU_WIDTH)),
               full((2, LRU_WIDTH))]
    out_f = pl.BlockSpec((tt, LRU_WIDTH), lambda b, j: (b * tps + j, 0))
    out_b = pl.BlockSpec((tt, LRU_WIDTH), lambda b, j: (b * tps + _rev_tile(j, ctx_t, tps), 0))
    kern = functools.partial(_lru_kernel, tt=tt, ctx_t=ctx_t, n_t=tps)
    return pl.pallas_call(
        kern, grid=(nb, tps), in_specs=fwd + bwd + w_specs, out_specs=[out_f, out_b],
        out_shape=[jax.ShapeDtypeStruct((m, LRU_WIDTH), F32)] * 2,
        scratch_shapes=[pltpu.VMEM((tt + 2 * HALO, LRU_WIDTH), F32), pltpu.VMEM((2, tt, LRU_WIDTH), F32),
                        pltpu.VMEM((2, tt, LRU_WIDTH), F32), pltpu.VMEM((2, LRU_WIDTH), F32)],
        compiler_params=_cp(("arbitrary", "arbitrary")), name="lru_scan",
    )(zin, zin, zin, zin, zin, zin, p["lru_conv_w"], p["lru_conv_b"].reshape(1, -1), p["lru_w_a"], p["lru_b_a"],
      p["lru_w_x"], p["lru_b_x"], p["lru_lambda"])


def _gdn_prep_kernel(x_ref, xp_ref, xn_ref, ba_ref, cw_ref, cb_ref, nega_ref, dtb_ref, qkv_ref, bg_ref, ext_s,
                     *, ctx_t, n_t):
    tile = pl.program_id(1)
    prev_ok, next_ok = _seg_edges(tile, ctx_t, n_t)
    xc = _conv4(x_ref[...], xp_ref[...], xn_ref[...], prev_ok, next_ok, cw_ref, cb_ref, ext_s)
    xc = xc * jax.nn.sigmoid(xc)
    nq = GDN_HEADS * GDN_DK
    for hd in range(GDN_HEADS):
        for base, scale in ((0, GDN_DK ** -0.5), (nq, 1.0)):
            sl = slice(base + hd * GDN_DK, base + (hd + 1) * GDN_DK)
            v = xc[:, sl]
            qkv_ref[:, sl] = v * (lax.rsqrt(jnp.sum(v * v, axis=-1, keepdims=True) + NORM_EPS) * scale)
    qkv_ref[:, 2 * nq:] = xc[:, 2 * nq:]
    ba = ba_ref[...]
    cols = lax.broadcasted_iota(jnp.int32, ba.shape, 1)
    gval = nega_ref[...] * _softplus(ba + dtb_ref[...])
    bg_ref[...] = jnp.where(cols < 2 * GDN_HEADS, jax.nn.sigmoid(ba), gval)


def gdn_prep(zin, p, geo):
    m = zin.shape[0]
    tt, tps, ctx_t, nb = geo["tt"], geo["tiles_per_seq"], geo["ctx_tiles"], geo["batch"]
    specs = _halo_specs(GDN_QKV, COL_QKV // GDN_QKV, lambda j: j, tt, tps, m)
    specs.append(pl.BlockSpec((tt, 128), lambda b, j: (b * tps + j, COL_BA // 128)))
    full = lambda shape: pl.BlockSpec(shape, lambda b, j: (0,) * len(shape))
    specs += [full((4, GDN_QKV)), full((1, GDN_QKV)), full((1, 128)), full((1, 128))]
    pad = jnp.zeros((1, 128), F32)
    nega = pad.at[0, 8:16].set(-jnp.exp(p["gdn_a_log"].reshape(-1)))
    dtb = pad.at[0, 8:16].set(p["gdn_dt_bias"].reshape(-1))
    kern = functools.partial(_gdn_prep_kernel, ctx_t=ctx_t, n_t=tps)
    return pl.pallas_call(
        kern, grid=(nb, tps), in_specs=specs,
        out_specs=[pl.BlockSpec((tt, GDN_QKV), lambda b, j: (b * tps + j, 0)),
                   pl.BlockSpec((tt, 128), lambda b, j: (b * tps + j, 0))],
        out_shape=[jax.ShapeDtypeStruct((m, GDN_QKV), F32), jax.ShapeDtypeStruct((m, 128), F32)],
        scratch_shapes=[pltpu.VMEM((tt + 2 * HALO, GDN_QKV), F32)],
        compiler_params=_cp(("arbitrary", "arbitrary")), name="gdn_prep",
    )(zin, zin, zin, zin, p["gdn_conv_w"], p["gdn_conv_b"].reshape(1, -1), nega, dtb)


def _gdn_kernel(xf_ref, bgf_ref, xb_ref, bgb_ref, of_ref, ob_ref, s_s, *, gb):
    j = pl.program_id(1)

    @pl.when(j == 0)
    def _():
        s_s[...] = jnp.zeros_like(s_s)

    c = GDN_CHUNK
    nq = GDN_HEADS * GDN_DK
    ii = lax.broadcasted_iota(jnp.int32, (c, c), 0)
    jj = lax.broadcasted_iota(jnp.int32, (c, c), 1)
    incl = ((ii >= jj), (ii <= jj))
    strict = ((ii > jj), (ii < jj))
    blk = jnp.right_shift(ii, 4) == jnp.right_shift(jj, 4)
    eye = (ii == jj).astype(F32)

    chains = []
    for g in range(gb):
        for d, (x_ref, bg_ref) in enumerate(((xf_ref, bgf_ref), (xb_ref, bgb_ref))):
            bg = bg_ref[g]
            gam_all = _mm_f32(incl[d].astype(F32), bg)
            gam_t = gam_all.T
            for hd in range(GDN_HEADS):
                cb = d * GDN_HEADS + hd
                cg = 2 * GDN_HEADS + cb
                chains.append(dict(
                    g=g, d=d, hd=hd,
                    q=x_ref[g, :, hd * GDN_DK:(hd + 1) * GDN_DK],
                    k=x_ref[g, :, nq + hd * GDN_DK:nq + (hd + 1) * GDN_DK],
                    v=x_ref[g, :, 2 * nq + hd * GDN_DV:2 * nq + (hd + 1) * GDN_DV],
                    beta=bg[:, cb:cb + 1], gam=gam_all[:, cg:cg + 1], gam_row=gam_t[cg:cg + 1, :]))

    def put(name, fn):
        vals = [fn(ch) for ch in chains]
        for ch, val in zip(chains, vals):
            ch[name] = val

    put("dec_incl", lambda ch: jnp.where(
        incl[ch["d"]], jnp.exp(jnp.where(incl[ch["d"]], ch["gam"] - ch["gam_row"], 0.0)), 0.0))
    put("dec_strict", lambda ch: jnp.where(strict[ch["d"]], ch["dec_incl"], 0.0))
    put("g_last", lambda ch: ch["gam"][0:1, :] if ch["d"] == 1 else ch["gam"][c - 1:c, :])
    put("eg", lambda ch: jnp.exp(ch["gam"]))
    put("kb", lambda ch: ch["k"] * ch["beta"])
    put("a", lambda ch: _mm_nt(ch["kb"], ch["k"]) * ch["dec_strict"])
    put("rhs", lambda ch: jnp.concatenate([ch["v"] * ch["beta"], ch["kb"] * ch["eg"]], axis=1))
    put("a_d", lambda ch: jnp.where(blk, ch["a"], 0.0))
    put("a_o", lambda ch: ch["a"] - ch["a_d"])
    put("a2", lambda ch: _mm(ch["a_d"], ch["a_d"]))
    put("a4", lambda ch: _mm(ch["a2"], ch["a2"]))
    put("a8", lambda ch: _mm(ch["a4"], ch["a4"]))
    put("dinv", lambda ch: eye - ch["a_d"])
    put("dinv", lambda ch: ch["dinv"] + _mm(ch["dinv"], ch["a2"]))
    put("dinv", lambda ch: ch["dinv"] + _mm(ch["dinv"], ch["a4"]))
    put("dinv", lambda ch: ch["dinv"] + _mm(ch["dinv"], ch["a8"]))
    put("n1", lambda ch: _mm(ch["dinv"], ch["a_o"]))
    put("n2", lambda ch: _mm(ch["n1"], ch["n1"]))
    put("x", lambda ch: _mm(ch["dinv"], ch["rhs"]))
    put("x", lambda ch: ch["x"] + _mm(ch["n2"], ch["x"]))
    put("x", lambda ch: ch["x"] - _mm(ch["n1"], ch["x"]))
    put("qk", lambda ch: _mm_nt(ch["q"], ch["k"]) * ch["dec_incl"])
    put("s", lambda ch: s_s[ch["g"], ch["d"], ch["hd"]])
    put("v_new", lambda ch: ch["x"][:, :GDN_DV] - _mm(ch["x"][:, GDN_DV:], ch["s"]))
    put("o", lambda ch: _mm(ch["q"] * ch["eg"], ch["s"]) + _mm(ch["qk"], ch["v_new"]))
    put("s_new", lambda ch: ch["s"] * jnp.exp(ch["g_last"])
        + _mm_tn(ch["k"] * jnp.exp(ch["g_last"] - ch["gam"]), ch["v_new"]))
    for ch in chains:
        o_ref = of_ref if ch["d"] == 0 else ob_ref
        o_ref[ch["g"], :, ch["hd"] * GDN_DV:(ch["hd"] + 1) * GDN_DV] = ch["o"]
        s_s[ch["g"], ch["d"], ch["hd"]] = ch["s_new"]


def gdn_scan(qkvn, bg, geo):
    m = qkvn.shape[0]
    c = GDN_CHUNK
    nb, s_len, ctx = geo["batch"], geo["s_len"], geo["ctx"]
    gb = 2 if nb % 2 == 0 else 1
    n_c, ctx_c = s_len // c, ctx // c
    fwd = lambda b, j: (b, j, 0)
    bwd = lambda b, j: (b, _rev_tile(j, ctx_c, n_c), 0)
    w = GDN_HEADS * GDN_DV
    kern = functools.partial(_gdn_kernel, gb=gb)
    qkv3 = qkvn.reshape(nb, s_len, GDN_QKV)
    bg3 = bg.reshape(nb, s_len, 128)
    of, ob = pl.pallas_call(
        kern, grid=(nb // gb, n_c),
        in_specs=[pl.BlockSpec((gb, c, GDN_QKV), fwd), pl.BlockSpec((gb, c, 128), fwd),
                  pl.BlockSpec((gb, c, GDN_QKV), bwd), pl.BlockSpec((gb, c, 128), bwd)],
        out_specs=[pl.BlockSpec((gb, c, w), fwd), pl.BlockSpec((gb, c, w), bwd)],
        out_shape=[jax.ShapeDtypeStruct((nb, s_len, w), F32)] * 2,
        scratch_shapes=[pltpu.VMEM((gb, 2, GDN_HEADS, GDN_DK, GDN_DV), F32)],
        compiler_params=_cp(("arbitrary", "arbitrary")), name="gdn_scan",
    )(qkv3, bg3, qkv3, bg3)
    return of.reshape(m, w), ob.reshape(m, w)


def _fourier_kernel(c_ref, s_ref, u_ref, cc_ref, sc_ref, o_ref, *, scale):
    u = u_ref[0]
    a = jnp.dot(c_ref[...], u, preferred_element_type=F32)
    b = jnp.dot(s_ref[...], u, preferred_element_type=F32)
    o = _mm(a, cc_ref[...]) - _mm(b, sc_ref[...])
    o_ref[0] = o * scale


def _dft_mats(t):
    idx = jnp.arange(t, dtype=jnp.int32)
    ang = ((idx[:, None] * idx[None, :]) % t).astype(F32) * (2.0 * math.pi / t)
    return jnp.cos(ang).astype(BF16), jnp.sin(ang).astype(BF16)


def fourier(u, chan_c, chan_s):
    nb, t, w = u.shape
    cm, sm = _dft_mats(t)
    tf = _pick(t, (512, 256, 128))
    kern = functools.partial(_fourier_kernel, scale=1.0 / math.sqrt(t * FFT_GROUP))
    return pl.pallas_call(
        kern, grid=(t // tf, nb),
        in_specs=[pl.BlockSpec((tf, t), lambda f, b: (f, 0)), pl.BlockSpec((tf, t), lambda f, b: (f, 0)),
                  pl.BlockSpec((1, t, w), lambda f, b: (b, 0, 0)),
                  pl.BlockSpec((w, w), lambda f, b: (0, 0)), pl.BlockSpec((w, w), lambda f, b: (0, 0))],
        out_specs=pl.BlockSpec((1, tf, w), lambda f, b: (b, f, 0)),
        out_shape=jax.ShapeDtypeStruct((nb, t, w), F32),
        compiler_params=_cp(("arbitrary", "arbitrary")), name="fourier",
    )(cm, sm, u, chan_c, chan_s)


def _mla_prep_kernel(cq_ref, ckv_ref, misc_ref, cos_ref, sin_ref, gq_ref, gkv_ref, wq_ref, wkv_ref,
                     q_ref, k_ref, v_ref, *, q_scale):
    def rms(x, g):
        return x * lax.rsqrt(jnp.mean(x * x, axis=-1, keepdims=True) + NORM_EPS) * g

    cos = cos_ref[...]
    sin = sin_ref[...]
    qa = _mm(rms(cq_ref[...], gq_ref[...]), wq_ref[...])
    kva = _mm(rms(ckv_ref[...], gkv_ref[...]), wkv_ref[...])
    misc = misc_ref[...]
    k_rope = misc[:, :MLA_ROPE] * cos + misc[:, MLA_ROPE:] * sin
    zpad = jnp.zeros((cos.shape[0], MLA_QK_PAD - MLA_NOPE - MLA_ROPE), F32)
    for hd in range(MLA_HEADS):
        qh = qa[:, hd * 256:(hd + 1) * 256]
        q_rope = qh[:, MLA_NOPE:MLA_NOPE + MLA_ROPE] * cos + qh[:, MLA_NOPE + MLA_ROPE:] * sin
        q_ref[0, hd] = (jnp.concatenate([qh[:, :MLA_NOPE], q_rope, zpad], axis=1) * q_scale).astype(BF16)
        kh = kva[:, hd * 256:hd * 256 + MLA_NOPE]
        k_ref[0, hd] = jnp.concatenate([kh, k_rope, zpad], axis=1).astype(BF16)
        v_ref[0, hd] = kva[:, hd * 256 + MLA_NOPE:(hd + 1) * 256].astype(BF16)


def mla_prep(zin, cos_t, sin_t, p, geo):
    tt, tps, nb, s_len = geo["tt"], geo["tiles_per_seq"], geo["batch"], geo["s_len"]
    row = lambda b, j: b * tps + j
    full = lambda shape: pl.BlockSpec(shape, lambda b, j: (0,) * len(shape))
    ctx_t = geo["ctx_tiles"]
    hs = lambda w: pl.BlockSpec((1, MLA_HEADS, tt, w),
                                lambda b, j: (b, 0, jnp.where(j < ctx_t, j + (tps - ctx_t), j - ctx_t), 0))
    kern = functools.partial(_mla_prep_kernel, q_scale=(MLA_NOPE + MLA_ROPE) ** -0.5)
    return pl.pallas_call(
        kern, grid=(nb, tps),
        in_specs=[pl.BlockSpec((tt, MLA_Q_RANK), lambda b, j: (row(b, j), COL_CQ // MLA_Q_RANK)),
                  pl.BlockSpec((tt, MLA_KV_RANK), lambda b, j: (row(b, j), COL_CKV // MLA_KV_RANK)),
                  pl.BlockSpec((tt, 128), lambda b, j: (row(b, j), COL_MISC // 128)),
                  pl.BlockSpec((tt, MLA_ROPE), lambda b, j: (j, 0)), pl.BlockSpec((tt, MLA_ROPE), lambda b, j: (j, 0)),
                  full((1, MLA_Q_RANK)), full((1, MLA_KV_RANK)),
                  full((MLA_Q_RANK, MLA_HEADS * 256)), full((MLA_KV_RANK, MLA_HEADS * 256))],
        out_specs=[hs(MLA_QK_PAD), hs(MLA_QK_PAD), hs(MLA_V)],
        out_shape=[jax.ShapeDtypeStruct((nb, MLA_HEADS, s_len, MLA_QK_PAD), BF16),
                   jax.ShapeDtypeStruct((nb, MLA_HEADS, s_len, MLA_QK_PAD), BF16),
                   jax.ShapeDtypeStruct((nb, MLA_HEADS, s_len, MLA_V), BF16)],
        compiler_params=_cp(("arbitrary", "arbitrary")), name="mla_prep",
    )(zin, zin, zin, cos_t, sin_t, p["mla_q_norm_g"].reshape(1, -1), p["mla_kv_norm_g"].reshape(1, -1),
      p["mla_wq"], p["mla_wkv"])


def _attn_kernel(q_ref, k_ref, v_ref, o_ref):
    s = lax.dot_general(q_ref[0, 0], k_ref[0, 0], (((1,), (1,)), ((), ())), preferred_element_type=F32)
    pr = jnp.exp(s - jnp.max(s, axis=-1, keepdims=True))
    den = jnp.sum(pr, axis=-1, keepdims=True)
    o_ref[0] = jnp.dot(pr.astype(BF16), v_ref[0, 0], preferred_element_type=F32) / den


def mla_attention(q, k, v, geo):
    nb, nh, s_len, _ = q.shape
    n_ctx = geo["ctx"]
    n_lat = s_len - n_ctx
    assert n_lat % n_ctx == 0

    def call(tq, q_block0, n_q, key_rows, key_block):
        return pl.pallas_call(
            _attn_kernel, grid=(nb, nh, n_q // tq),
            in_specs=[pl.BlockSpec((1, 1, tq, MLA_QK_PAD), lambda b, h, i: (b, h, q_block0 + i, 0)),
                      pl.BlockSpec((1, 1, key_rows, MLA_QK_PAD), lambda b, h, i: (b, h, key_block, 0)),
                      pl.BlockSpec((1, 1, key_rows, MLA_V), lambda b, h, i: (b, h, key_block, 0))],
            out_specs=pl.BlockSpec((1, tq, MLA_V), lambda b, h, i: (b, i, h)),
            out_shape=jax.ShapeDtypeStruct((nb, n_q, nh * MLA_V), F32),
            compiler_params=_cp(("arbitrary", "arbitrary", "arbitrary")), name="mla_attention",
        )(q, k, v)

    tq_lat = _pick(n_lat, (256, 128))
    tq_ctx = _pick(n_ctx, (256, 128, 64))
    o_lat = call(tq_lat, 0, n_lat, s_len, 0)
    o_ctx = call(tq_ctx, n_lat // tq_ctx, n_ctx, n_ctx, n_lat // n_ctx)
    return o_lat, o_ctx


def _merge_kernel(hn_ref, hf_ref, hb_ref, lg_ref, of_ref, ob_ref, z_ref, gg_ref, yc_ref, yd_ref, wg_ref, wb_ref,
                  wo_ref, o_ref, y_s, acc_s, *, n_j):
    j = pl.program_id(1)

    @pl.when(j == 0)
    def _():
        lg = lg_ref[...]
        gelu = 0.5 * lg * (1.0 + jnp.tanh(math.sqrt(2.0 / math.pi) * (lg + 0.044715 * (lg * lg * lg))))
        y_s[0] = (gelu * (hf_ref[...] + hb_ref[...])).astype(BF16)
        for hd in range(GDN_HEADS):
            sl = slice(hd * GDN_DV, (hd + 1) * GDN_DV)
            o = of_ref[:, sl] + ob_ref[:, sl]
            o = o * lax.rsqrt(jnp.mean(o * o, axis=-1, keepdims=True) + NORM_EPS) * gg_ref[...]
            z = z_ref[:, sl]
            y_s[1, :, sl] = (o * (z * jax.nn.sigmoid(z))).astype(BF16)
        y_s[2] = yc_ref[...].astype(BF16)
        y_s[3] = yd_ref[...].astype(BF16)
        acc_s[...] = jnp.zeros_like(acc_s)

    hn = hn_ref[...]
    merged = None
    for i in range(N_BRANCH):
        gate = jax.nn.sigmoid(jnp.dot(hn, wg_ref[i], preferred_element_type=F32))
        term = gate * jnp.dot(y_s[i], wb_ref[i], preferred_element_type=F32)
        merged = term if merged is None else merged + term
    acc_s[...] += jnp.dot(merged.astype(BF16), wo_ref[...], preferred_element_type=F32)

    @pl.when(j == n_j - 1)
    def _():
        o_ref[...] = acc_s[...]


def merge(hn, zin, hf, hb, of, ob, yc, yd, p):
    m, d = hn.shape
    tm = _pick(m, (512, 256, 128))
    tn = 256
    n_j = d // tn
    w = BRANCH_WIDTH
    tok = lambda a_w, cb: pl.BlockSpec((tm, a_w), lambda i, j: (i, cb))
    kern = functools.partial(_merge_kernel, n_j=n_j)
    return pl.pallas_call(
        kern, grid=(m // tm, n_j),
        in_specs=[tok(d, 0), tok(w, 0), tok(w, 0), tok(w, COL_LG // w), tok(w, 0), tok(w, 0), tok(w, COL_Z // w),
                  pl.BlockSpec((1, GDN_DV), lambda i, j: (0, 0)), tok(w, 0), tok(w, 0),
                  pl.BlockSpec((N_BRANCH, d, tn), lambda i, j: (0, 0, j)),
                  pl.BlockSpec((N_BRANCH, w, tn), lambda i, j: (0, 0, j)),
                  pl.BlockSpec((tn, d), lambda i, j: (j, 0))],
        out_specs=pl.BlockSpec((tm, d), lambda i, j: (i, 0)),
        out_shape=jax.ShapeDtypeStruct((m, d), F32),
        scratch_shapes=[pltpu.VMEM((N_BRANCH, tm, w), BF16), pltpu.VMEM((tm, d), F32)],
        compiler_params=_cp(("arbitrary", "arbitrary")), name="merge",
    )(hn, hf, hb, zin, of, ob, zin, p["gdn_norm_g"].reshape(1, -1), yc, yd, p["w_gate"], p["w_branch"], p["w_out"])


def _moe_kernel(be_ref, nv_ref, x_ref, g_ref, w1_ref, b1_ref, w2_ref, b2_ref, o_ref, w1_s, w2_s):
    i = pl.program_id(0)
    new_expert = jnp.logical_or(i == 0, be_ref[i] != be_ref[jnp.maximum(i - 1, 0)])

    @pl.when(new_expert)
    def _():
        w1_s[...] = w1_ref[0, 0].astype(BF16)
        w2_s[...] = w2_ref[0, 0].astype(BF16)

    @pl.when(i < nv_ref[0])
    def _():
        h = jnp.dot(x_ref[...], w1_s[...], preferred_element_type=F32) + b1_ref[0, 0]
        h_glu = jnp.minimum(h[:, :D_EXPERT], SWIGLU_LIMIT)
        h_lin = jnp.clip(h[:, D_EXPERT:], -SWIGLU_LIMIT, SWIGLU_LIMIT)
        act = h_glu * jax.nn.sigmoid(SWIGLU_ALPHA * h_glu) * (h_lin + 1.0)
        y = jnp.dot(act.astype(BF16), w2_s[...], preferred_element_type=F32) + b2_ref[0, 0]
        o_ref[...] = (y * g_ref[...]).astype(o_ref.dtype)

    @pl.when(i >= nv_ref[0])
    def _():
        o_ref[...] = jnp.zeros_like(o_ref)


def moe_experts(xg, gate, block_e, n_valid, layer, w1, b1, w2, b2):
    p_rows, d = xg.shape
    bm = MOE_ROWS
    n_blocks = p_rows // bm
    depth, ne, _, dh = w1.shape
    grid_spec = pltpu.PrefetchScalarGridSpec(
        num_scalar_prefetch=2, grid=(n_blocks,),
        in_specs=[pl.BlockSpec((bm, d), lambda i, be, nv: (i, 0)),
                  pl.BlockSpec((bm, 1), lambda i, be, nv: (i, 0)),
                  pl.BlockSpec((1, 1, d, dh), lambda i, be, nv: (layer, be[i], 0, 0)),
                  pl.BlockSpec((1, 1, 1, dh), lambda i, be, nv: (layer, be[i], 0, 0)),
                  pl.BlockSpec((1, 1, D_EXPERT, d), lambda i, be, nv: (layer, be[i], 0, 0)),
                  pl.BlockSpec((1, 1, 1, d), lambda i, be, nv: (layer, be[i], 0, 0))],
        out_specs=pl.BlockSpec((bm, d), lambda i, be, nv: (i, 0)),
        scratch_shapes=[pltpu.VMEM((d, dh), BF16), pltpu.VMEM((D_EXPERT, d), BF16)])
    return pl.pallas_call(
        _moe_kernel, grid_spec=grid_spec, out_shape=jax.ShapeDtypeStruct((p_rows, d), F32),
        compiler_params=_cp(("arbitrary",)), name="moe_experts",
    )(block_e, n_valid, xg, gate.reshape(p_rows, 1), w1, b1.reshape(depth, ne, 1, dh), w2, b2.reshape(depth, ne, 1, d))


def moe_ffn(hn, idx, gates, rank, counts, layer, w1, b1, w2, b2):
    n, d = hn.shape
    bm = MOE_ROWS
    nk = n * TOP_K
    counts = counts.astype(jnp.int32)
    padded = (counts + bm - 1) // bm * bm
    pend = jnp.cumsum(padded)
    pstart = pend - padded
    experts = jnp.arange(N_EXPERTS, dtype=jnp.int32)
    pos = rank + jnp.sum(jnp.where(idx[..., None] == experts, pstart, 0), axis=-1)
    n_blocks = (nk + N_EXPERTS * (bm - 1) + bm - 1) // bm
    p_rows = n_blocks * bm
    real = (idx.reshape(nk) << KEY_EXPERT_SHIFT) + jnp.arange(nk, dtype=jnp.int32)
    fill = jnp.arange(bm - 1, dtype=jnp.int32)
    pad_keys = jnp.where(fill[None, :] < (padded - counts)[:, None],
                         (experts[:, None] << KEY_EXPERT_SHIFT) + (1 << KEY_PAD_BIT) + fill[None, :],
                         jnp.iinfo(jnp.int32).max)
    n_fill = p_rows - nk
    tail = jnp.full((n_fill - N_EXPERTS * (bm - 1),), jnp.iinfo(jnp.int32).max, jnp.int32)
    keys, buf_gate = lax.sort((jnp.concatenate([real, pad_keys.reshape(-1), tail]),
                               jnp.concatenate([gates.reshape(nk), jnp.zeros((n_fill,), F32)])), num_keys=1)
    is_pad = (keys >> KEY_PAD_BIT) & 1
    buf_tok = jnp.where(is_pad == 1, 0, (keys & ((1 << KEY_PAD_BIT) - 1)) // TOP_K)
    block_start = jnp.arange(n_blocks, dtype=jnp.int32) * bm
    block_e = jnp.minimum(jnp.sum((pend[None, :] <= block_start[:, None]).astype(jnp.int32), axis=1), N_EXPERTS - 1)
    n_valid = (pend[-1:] // bm).astype(jnp.int32)
    y = moe_experts(hn[buf_tok], buf_gate, block_e, n_valid, layer, w1, b1, w2, b2)
    out = y[pos[:, 0]]
    for kk in range(1, TOP_K):
        out = out + y[pos[:, kk]]
    return out


def _rope_rot(w):
    h = MLA_ROPE // 4
    a, b, c, d = w[..., :h], w[..., h:2 * h], w[..., 2 * h:3 * h], w[..., 3 * h:]
    return jnp.concatenate([-b, a, -d, c], axis=-1)


def _prep_layer(w_in, mla_w_uq, mla_w_ukv):
    d = w_in.shape[0]
    o = np.cumsum([0, 512, 512, GDN_QKV, 512, 8, 8, 512, 512, 256, 64, N_BRANCH * d])
    lu, lg, qkv, z, be, al, fu, cq, ckv, kr, mg = [w_in[:, o[i]:o[i + 1]] for i in range(11)]
    pad = jnp.zeros((d, 128 - 16), w_in.dtype)
    w_main = jnp.concatenate([qkv, lu, lg, z, fu, cq, ckv, kr, _rope_rot(kr), be, al, pad], axis=1).astype(BF16)
    w_gate = jnp.transpose(mg.reshape(d, N_BRANCH, d), (1, 0, 2)).astype(BF16)
    uq = mla_w_uq.reshape(MLA_Q_RANK, MLA_HEADS, MLA_NOPE + MLA_ROPE)
    wq = jnp.concatenate([uq, _rope_rot(uq[..., MLA_NOPE:])], axis=-1).reshape(MLA_Q_RANK, MLA_HEADS * 256)
    return w_main, w_gate, wq.astype(BF16), mla_w_ukv.astype(BF16)


def _rope_tables(ctx, n_lat):
    half = MLA_ROPE // 4
    inv = jnp.power(ROPE_BASE, -jnp.arange(half, dtype=F32) / half)
    t = jnp.arange(n_lat)
    row = (t // GRID_W).astype(F32)[:, None] * inv
    col = (t % GRID_W).astype(F32)[:, None] * inv
    cos = jnp.concatenate([jnp.cos(row), jnp.cos(row), jnp.cos(col), jnp.cos(col)], axis=1)
    sin = jnp.concatenate([jnp.sin(row), jnp.sin(row), jnp.sin(col), jnp.sin(col)], axis=1)
    cos = jnp.concatenate([jnp.ones((ctx, MLA_ROPE), F32), cos], axis=0)
    sin = jnp.concatenate([jnp.zeros((ctx, MLA_ROPE), F32), sin], axis=0)
    return cos, sin


def _chan_dft():
    idx = np.arange(FFT_GROUP)
    ang = 2.0 * np.pi * ((idx[:, None] * idx[None, :]) % FFT_GROUP) / FFT_GROUP
    eye = np.eye(FFT_GROUPS)
    return (jnp.asarray(np.kron(eye, np.cos(ang)), BF16), jnp.asarray(np.kron(eye, np.sin(ang)), BF16))


def kernel(x, c, ctx, c_ctx, w_ada, b_ada, norm1_g, norm2_g, w_in, lru_conv_w, lru_conv_b, lru_w_a, lru_b_a, lru_w_x, lru_b_x, lru_lambda, gdn_conv_w, gdn_conv_b, gdn_a_log, gdn_dt_bias, gdn_norm_g, mla_q_norm_g, mla_kv_norm_g, mla_w_uq, mla_w_ukv, w_branch, w_out, router_w, router_b, exp_w1, exp_b1, exp_w2, exp_b2, final_norm_g):
    nb, n_lat, d = x.shape
    n_ctx = ctx.shape[1]
    depth = w_ada.shape[0]
    s_len = n_ctx + n_lat
    m = nb * s_len
    tt = _pick(math.gcd(n_ctx, n_lat), (256, 128, 64))
    geo = dict(tt=tt, tiles_per_seq=s_len // tt, ctx_tiles=n_ctx // tt, batch=nb, s_len=s_len, ctx=n_ctx)

    mod_rows = 8 * ((nb + 1 + 7) // 8)
    cmat = jnp.zeros((mod_rows, d), F32).at[:nb].set(c).at[nb].set(c_ctx)
    mod_all = ada_table(cmat, w_ada, b_ada).reshape(depth, mod_rows * N_MOD, 1, d)

    cos_t, sin_t = _rope_tables(n_ctx, n_lat)
    chan_c, chan_s = _chan_dft()
    h = jnp.concatenate([ctx, x], axis=1).reshape(m, d)

    delta, prev_mod, gate_k = None, None, None
    for l in range(depth):
        w_main, w_gate, wq, wkv = _prep_layer(w_in[l], mla_w_uq[l], mla_w_ukv[l])
        p = dict(lru_conv_w=lru_conv_w[l], lru_conv_b=lru_conv_b[l], lru_w_a=lru_w_a[l].astype(BF16),
                 lru_b_a=lru_b_a[l], lru_w_x=lru_w_x[l].astype(BF16), lru_b_x=lru_b_x[l], lru_lambda=lru_lambda[l],
                 gdn_conv_w=gdn_conv_w[l], gdn_conv_b=gdn_conv_b[l], gdn_a_log=gdn_a_log[l],
                 gdn_dt_bias=gdn_dt_bias[l], gdn_norm_g=gdn_norm_g[l], mla_q_norm_g=mla_q_norm_g[l],
                 mla_kv_norm_g=mla_kv_norm_g[l], mla_wq=wq, mla_wkv=wkv, w_gate=w_gate,
                 w_branch=w_branch[l].astype(BF16), w_out=w_out[l].astype(BF16))
        mod = mod_all[l]
        h, hn = norm_mod(h, norm1_g[l], geo, mod=mod, shift_k=0, scale_k=1, delta=delta, gate_k=gate_k,
                         prev_mod=prev_mod)
        zin = matmul(hn, w_main)
        hf, hb = lru_scan(zin, p, geo)
        qkvn, bg = gdn_prep(zin, p, geo)
        of, ob = gdn_scan(qkvn, bg, geo)
        fu = zin[:, COL_FU:COL_FU + FFT_WIDTH].astype(BF16).reshape(nb, s_len, FFT_WIDTH)
        yc = jnp.concatenate([fourier(fu[:, :n_ctx], chan_c, chan_s), fourier(fu[:, n_ctx:], chan_c, chan_s)],
                             axis=1).reshape(m, FFT_WIDTH)
        q, k, v = mla_prep(zin, cos_t, sin_t, p, geo)
        yd_lat, yd_ctx = mla_attention(q, k, v, geo)
        yd = jnp.concatenate([yd_ctx, yd_lat], axis=1).reshape(m, MLA_HEADS * MLA_V)
        mixed = merge(hn, zin, hf, hb, of, ob, yc, yd, p)
        h, hn2, idx, gates, rank, counts = norm_mod(
            h, norm2_g[l], geo, mod=mod, shift_k=3, scale_k=4, delta=mixed, gate_k=2, prev_mod=mod,
            router=(router_w[l], router_b[l]))
        ffn = moe_ffn(hn2, idx, gates, rank, counts[0], l, exp_w1, exp_b1, exp_w2, exp_b2)
        delta, prev_mod, gate_k = ffn, mod, 5
    (out,) = norm_mod(h, final_norm_g, geo, delta=delta, gate_k=5, prev_mod=prev_mod, emit_h=False,
                      out_dtype=F32, lat_only=True)
    return out.reshape(nb, n_lat, d)
```

```python
import functools
import math

import jax
import jax.numpy as jnp
import numpy as np
from jax import lax
from jax.experimental import pallas as pl
from jax.experimental.pallas import tpu as pltpu

F32 = jnp.float32
BF16 = jnp.bfloat16

GRID_W = 64
NORM_EPS = 1e-6
N_MOD = 6
LRU_WIDTH = 512
LRU_BLOCKS = 4
LRU_BLOCK = 128
LRU_C = 8.0
GDN_HEADS = 4
GDN_DK = 128
GDN_DV = 128
GDN_CHUNK = 64
GDN_QKV = GDN_HEADS * (2 * GDN_DK + GDN_DV)
FFT_GROUPS = 4
FFT_GROUP = 128
FFT_WIDTH = 512
MLA_HEADS = 4
MLA_Q_RANK = 512
MLA_KV_RANK = 256
MLA_NOPE = 128
MLA_ROPE = 64
MLA_V = 128
MLA_QK_PAD = 256
ROPE_BASE = 10000.0
N_BRANCH = 4
BRANCH_WIDTH = 512
N_EXPERTS = 32
TOP_K = 4
D_EXPERT = 512
SWIGLU_LIMIT = 7.0
SWIGLU_ALPHA = 1.702
MOE_ROWS = 512
KEY_EXPERT_SHIFT = 20
KEY_PAD_BIT = 19

COL_QKV = 0
COL_LU = 1536
COL_LG = 2048
COL_Z = 2560
COL_FU = 3072
COL_CQ = 3584
COL_CKV = 4096
COL_MISC = 4352
COL_BA = 4480
ZIN_W = 4608

VMEM_LIMIT = 56 * 1024 * 1024
HALO = 8


def _cp(sem, vmem=VMEM_LIMIT):
    return pltpu.CompilerParams(dimension_semantics=sem, vmem_limit_bytes=vmem)


def _pick(n, cands):
    for c in cands:
        if n % c == 0:
            return c
    raise ValueError(f"no tile for {n} in {cands}")


def _mm(a, b):
    return jnp.dot(a.astype(BF16), b.astype(BF16), preferred_element_type=F32)


def _mm_nt(a, b):
    return lax.dot_general(a.astype(BF16), b.astype(BF16), (((1,), (1,)), ((), ())), preferred_element_type=F32)


def _mm_tn(a, b):
    return lax.dot_general(a.astype(BF16), b.astype(BF16), (((0,), (0,)), ((), ())), preferred_element_type=F32)


def _mm_f32(a, b):
    return jnp.dot(a, b, preferred_element_type=F32, precision=lax.Precision.HIGHEST)


def _softplus(y):
    return jnp.maximum(y, 0.0) + jnp.log1p(jnp.exp(-jnp.abs(y)))


def _ada_kernel(c_ref, w_ref, b_ref, o_ref):
    cv = c_ref[...]
    s = cv * jax.nn.sigmoid(cv)
    o_ref[0] = _mm(s, w_ref[0]) + b_ref[0]


def ada_table(cmat, w_ada, b_ada):
    depth, d, n = w_ada.shape
    rows = cmat.shape[0]
    tn = _pick(n, (1024, 512, 256, 128))
    return pl.pallas_call(
        _ada_kernel,
        grid=(depth, n // tn),
        in_specs=[pl.BlockSpec((rows, d), lambda l, j: (0, 0)),
                  pl.BlockSpec((1, d, tn), lambda l, j: (l, 0, j)),
                  pl.BlockSpec((1, 1, tn), lambda l, j: (l, 0, j))],
        out_specs=pl.BlockSpec((1, rows, tn), lambda l, j: (l, 0, j)),
        out_shape=jax.ShapeDtypeStruct((depth, rows, n), F32),
        compiler_params=_cp(("arbitrary", "arbitrary")),
        name="ada_table",
    )(cmat, w_ada, b_ada.reshape(depth, 1, n))


def _norm_kernel(*refs, has_delta, has_mod, has_router, emit_h):
    it = iter(refs)
    h_ref = next(it)
    if has_delta:
        d_ref = next(it)
        gate_ref = next(it)
    g_ref = next(it)
    if has_mod:
        shift_ref = next(it)
        scale_ref = next(it)
    if has_router:
        rw_ref = next(it)
        rb_ref = next(it)
    if emit_h:
        ho_ref = next(it)
    y_ref = next(it)
    if has_router:
        idx_ref = next(it)
        gt_ref = next(it)
        rank_ref = next(it)
        cnt_ref = next(it)

    h = h_ref[...]
    if has_delta:
        h = h + gate_ref[0] * d_ref[...]
    if emit_h:
        ho_ref[...] = h
    y = h * lax.rsqrt(jnp.mean(h * h, axis=-1, keepdims=True) + NORM_EPS) * g_ref[...]
    if has_mod:
        y = y * (1.0 + scale_ref[0]) + shift_ref[0]
    y_ref[...] = y.astype(y_ref.dtype)
    if has_router:
        @pl.when(pl.program_id(0) == 0)
        def _():
            cnt_ref[...] = jnp.zeros_like(cnt_ref)

        logit = _mm_f32(y, rw_ref[...]) + rb_ref[...]
        tt, ne = logit.shape
        lane = lax.broadcasted_iota(jnp.int32, (tt, ne), 1).astype(F32)
        vals, hots = [], []
        for kk in range(TOP_K):
            top = jnp.max(logit, axis=-1, keepdims=True)
            arg = jnp.min(jnp.where(logit == top, lane, float(ne)), axis=-1, keepdims=True)
            hot = lane == arg
            idx_ref[:, kk:kk + 1] = arg.astype(jnp.int32)
            vals.append(top)
            hots.append(hot)
            logit = jnp.where(hot, -jnp.inf, logit)
        exps = [jnp.exp(v - vals[0]) for v in vals]
        den = exps[0] + exps[1] + exps[2] + exps[3]
        for kk in range(TOP_K):
            gt_ref[:, kk:kk + 1] = exps[kk] / den
        hot_all = hots[0].astype(F32) + hots[1].astype(F32) + hots[2].astype(F32) + hots[3].astype(F32)
        ii = lax.broadcasted_iota(jnp.int32, (tt, tt), 0)
        jj = lax.broadcasted_iota(jnp.int32, (tt, tt), 1)
        before = _mm((jj < ii).astype(F32), hot_all) + cnt_ref[...]
        for kk in range(TOP_K):
            rank_ref[:, kk:kk + 1] = jnp.sum(jnp.where(hots[kk], before, 0.0), axis=-1, keepdims=True).astype(jnp.int32)
        cnt_ref[...] += jnp.sum(hot_all, axis=0, keepdims=True)


def norm_mod(h, g, geo, *, mod=None, shift_k=None, scale_k=None, delta=None, gate_k=None, prev_mod=None,
             router=None, emit_h=True, out_dtype=None, lat_only=False):
    m, d = h.shape
    out_dtype = BF16 if out_dtype is None else out_dtype
    tt, tps, ctx_t, nb = geo["tt"], geo["tiles_per_seq"], geo["ctx_tiles"], geo["batch"]
    if lat_only:
        lat_t = tps - ctx_t
        grid = (nb * lat_t,)
        in_row = lambda i: (i // lat_t) * tps + ctx_t + i % lat_t
        mod_row = lambda i: i // lat_t
        out_rows = nb * lat_t * tt
    else:
        grid = (m // tt,)
        in_row = lambda i: i
        mod_row = lambda i: jnp.where(i % tps < ctx_t, nb, i // tps)
        out_rows = m
    tok = lambda w: pl.BlockSpec((tt, w), lambda i: (in_row(i), 0))
    out_tok = lambda w: pl.BlockSpec((tt, w), lambda i: (i, 0))

    def mod_spec(k):
        return pl.BlockSpec((1, 1, d), lambda i: (mod_row(i) * N_MOD + k, 0, 0))

    args, specs = [h], [tok(d)]
    if delta is not None:
        args += [delta, prev_mod]
        specs += [tok(d), mod_spec(gate_k)]
    args.append(g.reshape(1, d))
    specs.append(pl.BlockSpec((1, d), lambda i: (0, 0)))
    if mod is not None:
        args += [mod, mod]
        specs += [mod_spec(shift_k), mod_spec(scale_k)]
    if router is not None:
        rw, rb = router
        ne = rw.shape[1]
        args += [rw, rb.reshape(1, ne)]
        specs += [pl.BlockSpec(rw.shape, lambda i: (0, 0)), pl.BlockSpec((1, ne), lambda i: (0, 0))]
    out_shape, out_specs = [], []
    if emit_h:
        out_shape.append(jax.ShapeDtypeStruct((out_rows, d), F32))
        out_specs.append(out_tok(d))
    out_shape.append(jax.ShapeDtypeStruct((out_rows, d), out_dtype))
    out_specs.append(out_tok(d))
    if router is not None:
        out_shape += [jax.ShapeDtypeStruct((out_rows, TOP_K), jnp.int32), jax.ShapeDtypeStruct((out_rows, TOP_K), F32),
                      jax.ShapeDtypeStruct((out_rows, TOP_K), jnp.int32), jax.ShapeDtypeStruct((1, ne), F32)]
        out_specs += [out_tok(TOP_K), out_tok(TOP_K), out_tok(TOP_K), pl.BlockSpec((1, ne), lambda i: (0, 0))]
    kern = functools.partial(_norm_kernel, has_delta=delta is not None, has_mod=mod is not None,
                             has_router=router is not None, emit_h=emit_h)
    return pl.pallas_call(kern, grid=grid, in_specs=specs, out_specs=out_specs, out_shape=out_shape,
                          compiler_params=_cp(("arbitrary",)), name="norm_mod")(*args)


def _matmul_kernel(x_ref, w_ref, o_ref):
    o_ref[...] = jnp.dot(x_ref[...], w_ref[...], preferred_element_type=F32).astype(o_ref.dtype)


def matmul(x, w, out_dtype=F32):
    m, k = x.shape
    n = w.shape[1]
    tm = _pick(m, (1024, 512, 256, 128))
    tn = _pick(n, (768, 512, 256, 128))
    return pl.pallas_call(
        _matmul_kernel,
        grid=(m // tm, n // tn),
        in_specs=[pl.BlockSpec((tm, k), lambda i, j: (i, 0)), pl.BlockSpec((k, tn), lambda i, j: (0, j))],
        out_specs=pl.BlockSpec((tm, tn), lambda i, j: (i, j)),
        out_shape=jax.ShapeDtypeStruct((m, n), out_dtype),
        compiler_params=_cp(("arbitrary", "arbitrary")),
        name="in_proj",
    )(x, w)


def _conv4(x, xp, xn, prev_ok, next_ok, cw_ref, cb_ref, ext_ref):
    tt = x.shape[0]
    ext_ref[0:HALO, :] = xp * prev_ok
    ext_ref[HALO:HALO + tt, :] = x
    ext_ref[HALO + tt:2 * HALO + tt, :] = xn * next_ok
    y = cb_ref[...] + cw_ref[2:3, :] * x
    y = y + cw_ref[0:1, :] * ext_ref[HALO - 2:HALO - 2 + tt, :]
    y = y + cw_ref[1:2, :] * ext_ref[HALO - 1:HALO - 1 + tt, :]
    y = y + cw_ref[3:4, :] * ext_ref[HALO + 1:HALO + 1 + tt, :]
    return y


def _seg_edges(tile, ctx_t, n_t):
    first = jnp.logical_or(tile == 0, tile == ctx_t)
    last = jnp.logical_or(tile == ctx_t - 1, tile == n_t - 1)
    return jnp.where(first, 0.0, 1.0).astype(F32), jnp.where(last, 0.0, 1.0).astype(F32)


def _rev_tile(j, ctx_t, n_t):
    return jnp.where(j < ctx_t, ctx_t - 1 - j, n_t - 1 - (j - ctx_t))


def _halo_specs(width, col_block, tile_of, tt, tps, m):
    r8 = tt // HALO
    last8 = m // HALO - 1

    def cur(b, j):
        return (b * tps + tile_of(j), col_block)

    def prev(b, j):
        return (jnp.maximum((b * tps + tile_of(j)) * r8 - 1, 0), col_block)

    def nxt(b, j):
        return (jnp.minimum((b * tps + tile_of(j) + 1) * r8, last8), col_block)

    return [pl.BlockSpec((tt, width), cur), pl.BlockSpec((HALO, width), prev), pl.BlockSpec((HALO, width), nxt)]


def _lru_kernel(xf_ref, xfp_ref, xfn_ref, xb_ref, xbp_ref, xbn_ref, cw_ref, cb_ref, wa_ref, ba_ref, wx_ref, bx_ref,
                lam_ref, hf_ref, hb_ref, ext_s, a_s, b_s, carry_s, *, tt, ctx_t, n_t):
    j = pl.program_id(1)

    @pl.when(j == 0)
    def _():
        carry_s[...] = jnp.zeros_like(carry_s)

    rows = lax.broadcasted_iota(jnp.int32, (HALO, LRU_WIDTH), 0)
    for d, (x_ref, xp_ref, xn_ref, out_ref) in enumerate(((xf_ref, xfp_ref, xfn_ref, hf_ref),
                                                          (xb_ref, xbp_ref, xbn_ref, hb_ref))):
        tile = j if d == 0 else _rev_tile(j, ctx_t, n_t)
        prev_ok, next_ok = _seg_edges(tile, ctx_t, n_t)
        xc = _conv4(x_ref[...], xp_ref[...], xn_ref[...], prev_ok, next_ok, cw_ref, cb_ref, ext_s)
        sp = _softplus(-lam_ref[d:d + 1, :])
        for g in range(LRU_BLOCKS):
            sl = slice(g * LRU_BLOCK, (g + 1) * LRU_BLOCK)
            xg = xc[:, sl]
            r = jax.nn.sigmoid(_mm(xg, wa_ref[d, g]) + ba_ref[d:d + 1, sl])
            ig = jax.nn.sigmoid(_mm(xg, wx_ref[d, g]) + bx_ref[d:d + 1, sl])
            log_a = -LRU_C * r * sp[:, sl]
            a = jnp.exp(log_a)
            a_s[d, :, sl] = a
            b_s[d, :, sl] = jnp.sqrt(-jnp.tanh(log_a) * (a * a + 1.0)) * (ig * xg)

        n_grp = tt // HALO

        def group(gi, carry, d=d, out_ref=out_ref):
            g0 = gi if d == 0 else n_grp - 1 - gi
            off = pl.multiple_of(g0 * HALO, HALO)
            av = a_s[d, pl.ds(off, HALO), :]
            bv = b_s[d, pl.ds(off, HALO), :]
            for s in (1, 2, 4):
                shift = s if d == 0 else HALO - s
                a_sh = pltpu.roll(av, shift, 0)
                b_sh = pltpu.roll(bv, shift, 0)
                ok = (rows >= s) if d == 0 else (rows < HALO - s)
                bv = jnp.where(ok, av * b_sh + bv, bv)
                av = jnp.where(ok, av * a_sh, av)
            hv = bv + av * carry
            out_ref[pl.ds(off, HALO), :] = hv
            return hv[HALO - 1:HALO, :] if d == 0 else hv[0:1, :]

        carry_s[d:d + 1, :] = lax.fori_loop(0, n_grp, group, carry_s[d:d + 1, :])


def lru_scan(zin, p, geo):
    m = zin.shape[0]
    tt, tps, ctx_t, nb = geo["tt"], geo["tiles_per_seq"], geo["ctx_tiles"], geo["batch"]
    cb = COL_LU // LRU_WIDTH
    fwd = _halo_specs(LRU_WIDTH, cb, lambda j: j, tt, tps, m)
    bwd = _halo_specs(LRU_WIDTH, cb, lambda j: _rev_tile(j, ctx_t, tps), tt, tps, m)
    full = lambda shape: pl.BlockSpec(shape, lambda b, j: (0,) * len(shape))
    w_specs = [full((4, LRU_WIDTH)), full((1, LRU_WIDTH)), full((2, LRU_BLOCKS, LRU_BLOCK, LRU_BLOCK)),
               full((2, LRU_WIDTH)), full((2, LRU_BLOCKS, LRU_BLOCK, LRU_BLOCK)), full((2, LRU_WIDTH)),
               full((2, LRU_WIDTH))]
    out_f = pl.BlockSpec((tt, LRU_WIDTH), lambda b, j: (b * tps + j, 0))
    out_b = pl.BlockSpec((tt, LRU_WIDTH), lambda b, j: (b * tps + _rev_tile(j, ctx_t, tps), 0))
    kern = functools.partial(_lru_kernel, tt=tt, ctx_t=ctx_t, n_t=tps)
    return pl.pallas_call(
        kern, grid=(nb, tps), in_specs=fwd + bwd + w_specs, out_specs=[out_f, out_b],
        out_shape=[jax.ShapeDtypeStruct((m, LRU_WIDTH), F32)] * 2,
        scratch_shapes=[pltpu.VMEM((tt + 2 * HALO, LRU_WIDTH), F32), pltpu.VMEM((2, tt, LRU_WIDTH), F32),
                        pltpu.VMEM((2, tt, LRU_WIDTH), F32), pltpu.VMEM((2, LRU_WIDTH), F32)],
        compiler_params=_cp(("arbitrary", "arbitrary")), name="lru_scan",
    )(zin, zin, zin, zin, zin, zin, p["lru_conv_w"], p["lru_conv_b"].reshape(1, -1), p["lru_w_a"], p["lru_b_a"],
      p["lru_w_x"], p["lru_b_x"], p["lru_lambda"])


def _gdn_prep_kernel(x_ref, xp_ref, xn_ref, ba_ref, cw_ref, cb_ref, nega_ref, dtb_ref, qkv_ref, bg_ref, ext_s,
                     *, ctx_t, n_t):
    tile = pl.program_id(1)
    prev_ok, next_ok = _seg_edges(tile, ctx_t, n_t)
    xc = _conv4(x_ref[...], xp_ref[...], xn_ref[...], prev_ok, next_ok, cw_ref, cb_ref, ext_s)
    xc = xc * jax.nn.sigmoid(xc)
    nq = GDN_HEADS * GDN_DK
    for hd in range(GDN_HEADS):
        for base, scale in ((0, GDN_DK ** -0.5), (nq, 1.0)):
            sl = slice(base + hd * GDN_DK, base + (hd + 1) * GDN_DK)
            v = xc[:, sl]
            qkv_ref[:, sl] = v * (lax.rsqrt(jnp.sum(v * v, axis=-1, keepdims=True) + NORM_EPS) * scale)
    qkv_ref[:, 2 * nq:] = xc[:, 2 * nq:]
    ba = ba_ref[...]
    cols = lax.broadcasted_iota(jnp.int32, ba.shape, 1)
    gval = nega_ref[...] * _softplus(ba + dtb_ref[...])
    bg_ref[...] = jnp.where(cols < 2 * GDN_HEADS, jax.nn.sigmoid(ba), gval)


def gdn_prep(zin, p, geo):
    m = zin.shape[0]
    tt, tps, ctx_t, nb = geo["tt"], geo["tiles_per_seq"], geo["ctx_tiles"], geo["batch"]
    specs = _halo_specs(GDN_QKV, COL_QKV // GDN_QKV, lambda j: j, tt, tps, m)
    specs.append(pl.BlockSpec((tt, 128), lambda b, j: (b * tps + j, COL_BA // 128)))
    full = lambda shape: pl.BlockSpec(shape, lambda b, j: (0,) * len(shape))
    specs += [full((4, GDN_QKV)), full((1, GDN_QKV)), full((1, 128)), full((1, 128))]
    pad = jnp.zeros((1, 128), F32)
    nega = pad.at[0, 8:16].set(-jnp.exp(p["gdn_a_log"].reshape(-1)))
    dtb = pad.at[0, 8:16].set(p["gdn_dt_bias"].reshape(-1))
    kern = functools.partial(_gdn_prep_kernel, ctx_t=ctx_t, n_t=tps)
    return pl.pallas_call(
        kern, grid=(nb, tps), in_specs=specs,
        out_specs=[pl.BlockSpec((tt, GDN_QKV), lambda b, j: (b * tps + j, 0)),
                   pl.BlockSpec((tt, 128), lambda b, j: (b * tps + j, 0))],
        out_shape=[jax.ShapeDtypeStruct((m, GDN_QKV), F32), jax.ShapeDtypeStruct((m, 128), F32)],
        scratch_shapes=[pltpu.VMEM((tt + 2 * HALO, GDN_QKV), F32)],
        compiler_params=_cp(("arbitrary", "arbitrary")), name="gdn_prep",
    )(zin, zin, zin, zin, p["gdn_conv_w"], p["gdn_conv_b"].reshape(1, -1), nega, dtb)


def _gdn_kernel(xf_ref, bgf_ref, xb_ref, bgb_ref, of_ref, ob_ref, s_s, *, gb):
    j = pl.program_id(1)

    @pl.when(j == 0)
    def _():
        s_s[...] = jnp.zeros_like(s_s)

    c = GDN_CHUNK
    nq = GDN_HEADS * GDN_DK
    ii = lax.broadcasted_iota(jnp.int32, (c, c), 0)
    jj = lax.broadcasted_iota(jnp.int32, (c, c), 1)
    incl = ((ii >= jj), (ii <= jj))
    strict = ((ii > jj), (ii < jj))
    blk = jnp.right_shift(ii, 4) == jnp.right_shift(jj, 4)
    eye = (ii == jj).astype(F32)

    chains = []
    for g in range(gb):
        for d, (x_ref, bg_ref) in enumerate(((xf_ref, bgf_ref), (xb_ref, bgb_ref))):
            bg = bg_ref[g]
            gam_all = _mm_f32(incl[d].astype(F32), bg)
            gam_t = gam_all.T
            for hd in range(GDN_HEADS):
                cb = d * GDN_HEADS + hd
                cg = 2 * GDN_HEADS + cb
                chains.append(dict(
                    g=g, d=d, hd=hd,
                    q=x_ref[g, :, hd * GDN_DK:(hd + 1) * GDN_DK],
                    k=x_ref[g, :, nq + hd * GDN_DK:nq + (hd + 1) * GDN_DK],
                    v=x_ref[g, :, 2 * nq + hd * GDN_DV:2 * nq + (hd + 1) * GDN_DV],
                    beta=bg[:, cb:cb + 1], gam=gam_all[:, cg:cg + 1], gam_row=gam_t[cg:cg + 1, :]))

    def put(name, fn):
        vals = [fn(ch) for ch in chains]
        for ch, val in zip(chains, vals):
            ch[name] = val

    put("dec_incl", lambda ch: jnp.where(
        incl[ch["d"]], jnp.exp(jnp.where(incl[ch["d"]], ch["gam"] - ch["gam_row"], 0.0)), 0.0))
    put("dec_strict", lambda ch: jnp.where(strict[ch["d"]], ch["dec_incl"], 0.0))
    put("g_last", lambda ch: ch["gam"][0:1, :] if ch["d"] == 1 else ch["gam"][c - 1:c, :])
    put("eg", lambda ch: jnp.exp(ch["gam"]))
    put("kb", lambda ch: ch["k"] * ch["beta"])
    put("a", lambda ch: _mm_nt(ch["kb"], ch["k"]) * ch["dec_strict"])
    put("rhs", lambda ch: jnp.concatenate([ch["v"] * ch["beta"], ch["kb"] * ch["eg"]], axis=1))
    put("a_d", lambda ch: jnp.where(blk, ch["a"], 0.0))
    put("a_o", lambda ch: ch["a"] - ch["a_d"])
    put("a2", lambda ch: _mm(ch["a_d"], ch["a_d"]))
    put("a4", lambda ch: _mm(ch["a2"], ch["a2"]))
    put("a8", lambda ch: _mm(ch["a4"], ch["a4"]))
    put("dinv", lambda ch: eye - ch["a_d"])
    put("dinv", lambda ch: ch["dinv"] + _mm(ch["dinv"], ch["a2"]))
    put("dinv", lambda ch: ch["dinv"] + _mm(ch["dinv"], ch["a4"]))
    put("dinv", lambda ch: ch["dinv"] + _mm(ch["dinv"], ch["a8"]))
    put("n1", lambda ch: _mm(ch["dinv"], ch["a_o"]))
    put("n2", lambda ch: _mm(ch["n1"], ch["n1"]))
    put("x", lambda ch: _mm(ch["dinv"], ch["rhs"]))
    put("x", lambda ch: ch["x"] + _mm(ch["n2"], ch["x"]))
    put("x", lambda ch: ch["x"] - _mm(ch["n1"], ch["x"]))
    put("qk", lambda ch: _mm_nt(ch["q"], ch["k"]) * ch["dec_incl"])
    put("s", lambda ch: s_s[ch["g"], ch["d"], ch["hd"]])
    put("v_new", lambda ch: ch["x"][:, :GDN_DV] - _mm(ch["x"][:, GDN_DV:], ch["s"]))
    put("o", lambda ch: _mm(ch["q"] * ch["eg"], ch["s"]) + _mm(ch["qk"], ch["v_new"]))
    put("s_new", lambda ch: ch["s"] * jnp.exp(ch["g_last"])
        + _mm_tn(ch["k"] * jnp.exp(ch["g_last"] - ch["gam"]), ch["v_new"]))
    for ch in chains:
        o_ref = of_ref if ch["d"] == 0 else ob_ref
        o_ref[ch["g"], :, ch["hd"] * GDN_DV:(ch["hd"] + 1) * GDN_DV] = ch["o"]
        s_s[ch["g"], ch["d"], ch["hd"]] = ch["s_new"]


def gdn_scan(qkvn, bg, geo):
    m = qkvn.shape[0]
    c = GDN_CHUNK
    nb, s_len, ctx = geo["batch"], geo["s_len"], geo["ctx"]
    gb = 2 if nb % 2 == 0 else 1
    n_c, ctx_c = s_len // c, ctx // c
    fwd = lambda b, j: (b, j, 0)
    bwd = lambda b, j: (b, _rev_tile(j, ctx_c, n_c), 0)
    w = GDN_HEADS * GDN_DV
    kern = functools.partial(_gdn_kernel, gb=gb)
    qkv3 = qkvn.reshape(nb, s_len, GDN_QKV)
    bg3 = bg.reshape(nb, s_len, 128)
    of, ob = pl.pallas_call(
        kern, grid=(nb // gb, n_c),
        in_specs=[pl.BlockSpec((gb, c, GDN_QKV), fwd), pl.BlockSpec((gb, c, 128), fwd),
                  pl.BlockSpec((gb, c, GDN_QKV), bwd), pl.BlockSpec((gb, c, 128), bwd)],
        out_specs=[pl.BlockSpec((gb, c, w), fwd), pl.BlockSpec((gb, c, w), bwd)],
        out_shape=[jax.ShapeDtypeStruct((nb, s_len, w), F32)] * 2,
        scratch_shapes=[pltpu.VMEM((gb, 2, GDN_HEADS, GDN_DK, GDN_DV), F32)],
        compiler_params=_cp(("arbitrary", "arbitrary")), name="gdn_scan",
    )(qkv3, bg3, qkv3, bg3)
    return of.reshape(m, w), ob.reshape(m, w)


def _fourier_kernel(c_ref, s_ref, u_ref, cc_ref, sc_ref, o_ref, *, scale):
    u = u_ref[0]
    a = jnp.dot(c_ref[...], u, preferred_element_type=F32)
    b = jnp.dot(s_ref[...], u, preferred_element_type=F32)
    o = _mm(a, cc_ref[...]) - _mm(b, sc_ref[...])
    o_ref[0] = o * scale


def _dft_mats(t):
    idx = jnp.arange(t, dtype=jnp.int32)
    ang = ((idx[:, None] * idx[None, :]) % t).astype(F32) * (2.0 * math.pi / t)
    return jnp.cos(ang).astype(BF16), jnp.sin(ang).astype(BF16)


def fourier(u, chan_c, chan_s):
    nb, t, w = u.shape
    cm, sm = _dft_mats(t)
    tf = _pick(t, (512, 256, 128))
    kern = functools.partial(_fourier_kernel, scale=1.0 / math.sqrt(t * FFT_GROUP))
    return pl.pallas_call(
        kern, grid=(t // tf, nb),
        in_specs=[pl.BlockSpec((tf, t), lambda f, b: (f, 0)), pl.BlockSpec((tf, t), lambda f, b: (f, 0)),
                  pl.BlockSpec((1, t, w), lambda f, b: (b, 0, 0)),
                  pl.BlockSpec((w, w), lambda f, b: (0, 0)), pl.BlockSpec((w, w), lambda f, b: (0, 0))],
        out_specs=pl.BlockSpec((1, tf, w), lambda f, b: (b, f, 0)),
        out_shape=jax.ShapeDtypeStruct((nb, t, w), F32),
        compiler_params=_cp(("arbitrary", "arbitrary")), name="fourier",
    )(cm, sm, u, chan_c, chan_s)


def _mla_prep_kernel(cq_ref, ckv_ref, misc_ref, cos_ref, sin_ref, gq_ref, gkv_ref, wq_ref, wkv_ref,
                     q_ref, k_ref, v_ref, *, q_scale):
    def rms(x, g):
        return x * lax.rsqrt(jnp.mean(x * x, axis=-1, keepdims=True) + NORM_EPS) * g

    cos = cos_ref[...]
    sin = sin_ref[...]
    qa = _mm(rms(cq_ref[...], gq_ref[...]), wq_ref[...])
    kva = _mm(rms(ckv_ref[...], gkv_ref[...]), wkv_ref[...])
    misc = misc_ref[...]
    k_rope = misc[:, :MLA_ROPE] * cos + misc[:, MLA_ROPE:] * sin
    zpad = jnp.zeros((cos.shape[0], MLA_QK_PAD - MLA_NOPE - MLA_ROPE), F32)
    for hd in range(MLA_HEADS):
        qh = qa[:, hd * 256:(hd + 1) * 256]
        q_rope = qh[:, MLA_NOPE:MLA_NOPE + MLA_ROPE] * cos + qh[:, MLA_NOPE + MLA_ROPE:] * sin
        q_ref[0, hd] = (jnp.concatenate([qh[:, :MLA_NOPE], q_rope, zpad], axis=1) * q_scale).astype(BF16)
        kh = kva[:, hd * 256:hd * 256 + MLA_NOPE]
        k_ref[0, hd] = jnp.concatenate([kh, k_rope, zpad], axis=1).astype(BF16)
        v_ref[0, hd] = kva[:, hd * 256 + MLA_NOPE:(hd + 1) * 256].astype(BF16)


def mla_prep(zin, cos_t, sin_t, p, geo):
    tt, tps, nb, s_len = geo["tt"], geo["tiles_per_seq"], geo["batch"], geo["s_len"]
    row = lambda b, j: b * tps + j
    full = lambda shape: pl.BlockSpec(shape, lambda b, j: (0,) * len(shape))
    ctx_t = geo["ctx_tiles"]
    hs = lambda w: pl.BlockSpec((1, MLA_HEADS, tt, w),
                                lambda b, j: (b, 0, jnp.where(j < ctx_t, j + (tps - ctx_t), j - ctx_t), 0))
    kern = functools.partial(_mla_prep_kernel, q_scale=(MLA_NOPE + MLA_ROPE) ** -0.5)
    return pl.pallas_call(
        kern, grid=(nb, tps),
        in_specs=[pl.BlockSpec((tt, MLA_Q_RANK), lambda b, j: (row(b, j), COL_CQ // MLA_Q_RANK)),
                  pl.BlockSpec((tt, MLA_KV_RANK), lambda b, j: (row(b, j), COL_CKV // MLA_KV_RANK)),
                  pl.BlockSpec((tt, 128), lambda b, j: (row(b, j), COL_MISC // 128)),
                  pl.BlockSpec((tt, MLA_ROPE), lambda b, j: (j, 0)), pl.BlockSpec((tt, MLA_ROPE), lambda b, j: (j, 0)),
                  full((1, MLA_Q_RANK)), full((1, MLA_KV_RANK)),
                  full((MLA_Q_RANK, MLA_HEADS * 256)), full((MLA_KV_RANK, MLA_HEADS * 256))],
        out_specs=[hs(MLA_QK_PAD), hs(MLA_QK_PAD), hs(MLA_V)],
        out_shape=[jax.ShapeDtypeStruct((nb, MLA_HEADS, s_len, MLA_QK_PAD), BF16),
                   jax.ShapeDtypeStruct((nb, MLA_HEADS, s_len, MLA_QK_PAD), BF16),
                   jax.ShapeDtypeStruct((nb, MLA_HEADS, s_len, MLA_V), BF16)],
        compiler_params=_cp(("arbitrary", "arbitrary")), name="mla_prep",
    )(zin, zin, zin, cos_t, sin_t, p["mla_q_norm_g"].reshape(1, -1), p["mla_kv_norm_g"].reshape(1, -1),
      p["mla_wq"], p["mla_wkv"])


def _attn_kernel(q_ref, k_ref, v_ref, o_ref):
    s = lax.dot_general(q_ref[0, 0], k_ref[0, 0], (((1,), (1,)), ((), ())), preferred_element_type=F32)
    pr = jnp.exp(s - jnp.max(s, axis=-1, keepdims=True))
    den = jnp.sum(pr, axis=-1, keepdims=True)
    o_ref[0] = jnp.dot(pr.astype(BF16), v_ref[0, 0], preferred_element_type=F32) / den


def mla_attention(q, k, v, geo):
    nb, nh, s_len, _ = q.shape
    n_ctx = geo["ctx"]
    n_lat = s_len - n_ctx
    assert n_lat % n_ctx == 0

    def call(tq, q_block0, n_q, key_rows, key_block):
        return pl.pallas_call(
            _attn_kernel, grid=(nb, nh, n_q // tq),
            in_specs=[pl.BlockSpec((1, 1, tq, MLA_QK_PAD), lambda b, h, i: (b, h, q_block0 + i, 0)),
                      pl.BlockSpec((1, 1, key_rows, MLA_QK_PAD), lambda b, h, i: (b, h, key_block, 0)),
                      pl.BlockSpec((1, 1, key_rows, MLA_V), lambda b, h, i: (b, h, key_block, 0))],
            out_specs=pl.BlockSpec((1, tq, MLA_V), lambda b, h, i: (b, i, h)),
            out_shape=jax.ShapeDtypeStruct((nb, n_q, nh * MLA_V), F32),
            compiler_params=_cp(("arbitrary", "arbitrary", "arbitrary")), name="mla_attention",
        )(q, k, v)

    tq_lat = _pick(n_lat, (256, 128))
    tq_ctx = _pick(n_ctx, (256, 128, 64))
    o_lat = call(tq_lat, 0, n_lat, s_len, 0)
    o_ctx = call(tq_ctx, n_lat // tq_ctx, n_ctx, n_ctx, n_lat // n_ctx)
    return o_lat, o_ctx


def _merge_kernel(hn_ref, hf_ref, hb_ref, lg_ref, of_ref, ob_ref, z_ref, gg_ref, yc_ref, yd_ref, wg_ref, wb_ref,
                  wo_ref, o_ref, y_s, acc_s, *, n_j):
    j = pl.program_id(1)

    @pl.when(j == 0)
    def _():
        lg = lg_ref[...]
        gelu = 0.5 * lg * (1.0 + jnp.tanh(math.sqrt(2.0 / math.pi) * (lg + 0.044715 * (lg * lg * lg))))
        y_s[0] = (gelu * (hf_ref[...] + hb_ref[...])).astype(BF16)
        for hd in range(GDN_HEADS):
            sl = slice(hd * GDN_DV, (hd + 1) * GDN_DV)
            o = of_ref[:, sl] + ob_ref[:, sl]
            o = o * lax.rsqrt(jnp.mean(o * o, axis=-1, keepdims=True) + NORM_EPS) * gg_ref[...]
            z = z_ref[:, sl]
            y_s[1, :, sl] = (o * (z * jax.nn.sigmoid(z))).astype(BF16)
        y_s[2] = yc_ref[...].astype(BF16)
        y_s[3] = yd_ref[...].astype(BF16)
        acc_s[...] = jnp.zeros_like(acc_s)

    hn = hn_ref[...]
    merged = None
    for i in range(N_BRANCH):
        gate = jax.nn.sigmoid(jnp.dot(hn, wg_ref[i], preferred_element_type=F32))
        term = gate * jnp.dot(y_s[i], wb_ref[i], preferred_element_type=F32)
        merged = term if merged is None else merged + term
    acc_s[...] += jnp.dot(merged.astype(BF16), wo_ref[...], preferred_element_type=F32)

    @pl.when(j == n_j - 1)
    def _():
        o_ref[...] = acc_s[...]


def merge(hn, zin, hf, hb, of, ob, yc, yd, p):
    m, d = hn.shape
    tm = _pick(m, (512, 256, 128))
    tn = 256
    n_j = d // tn
    w = BRANCH_WIDTH
    tok = lambda a_w, cb: pl.BlockSpec((tm, a_w), lambda i, j: (i, cb))
    kern = functools.partial(_merge_kernel, n_j=n_j)
    return pl.pallas_call(
        kern, grid=(m // tm, n_j),
        in_specs=[tok(d, 0), tok(w, 0), tok(w, 0), tok(w, COL_LG // w), tok(w, 0), tok(w, 0), tok(w, COL_Z // w),
                  pl.BlockSpec((1, GDN_DV), lambda i, j: (0, 0)), tok(w, 0), tok(w, 0),
                  pl.BlockSpec((N_BRANCH, d, tn), lambda i, j: (0, 0, j)),
                  pl.BlockSpec((N_BRANCH, w, tn), lambda i, j: (0, 0, j)),
                  pl.BlockSpec((tn, d), lambda i, j: (j, 0))],
        out_specs=pl.BlockSpec((tm, d), lambda i, j: (i, 0)),
        out_shape=jax.ShapeDtypeStruct((m, d), F32),
        scratch_shapes=[pltpu.VMEM((N_BRANCH, tm, w), BF16), pltpu.VMEM((tm, d), F32)],
        compiler_params=_cp(("arbitrary", "arbitrary")), name="merge",
    )(hn, hf, hb, zin, of, ob, zin, p["gdn_norm_g"].reshape(1, -1), yc, yd, p["w_gate"], p["w_branch"], p["w_out"])


def _moe_kernel(be_ref, nv_ref, x_ref, g_ref, w1_ref, b1_ref, w2_ref, b2_ref, o_ref, w1_s, w2_s):
    i = pl.program_id(0)
    new_expert = jnp.logical_or(i == 0, be_ref[i] != be_ref[jnp.maximum(i - 1, 0)])

    @pl.when(new_expert)
    def _():
        w1_s[...] = w1_ref[0, 0].astype(BF16)
        w2_s[...] = w2_ref[0, 0].astype(BF16)

    @pl.when(i < nv_ref[0])
    def _():
        h = jnp.dot(x_ref[...], w1_s[...], preferred_element_type=F32) + b1_ref[0, 0]
        h_glu = jnp.minimum(h[:, :D_EXPERT], SWIGLU_LIMIT)
        h_lin = jnp.clip(h[:, D_EXPERT:], -SWIGLU_LIMIT, SWIGLU_LIMIT)
        act = h_glu * jax.nn.sigmoid(SWIGLU_ALPHA * h_glu) * (h_lin + 1.0)
        y = jnp.dot(act.astype(BF16), w2_s[...], preferred_element_type=F32) + b2_ref[0, 0]
        o_ref[...] = (y * g_ref[...]).astype(o_ref.dtype)

    @pl.when(i >= nv_ref[0])
    def _():
        o_ref[...] = jnp.zeros_like(o_ref)


def moe_experts(xg, gate, block_e, n_valid, layer, w1, b1, w2, b2):
    p_rows, d = xg.shape
    bm = MOE_ROWS
    n_blocks = p_rows // bm
    depth, ne, _, dh = w1.shape
    grid_spec = pltpu.PrefetchScalarGridSpec(
        num_scalar_prefetch=2, grid=(n_blocks,),
        in_specs=[pl.BlockSpec((bm, d), lambda i, be, nv: (i, 0)),
                  pl.BlockSpec((bm, 1), lambda i, be, nv: (i, 0)),
                  pl.BlockSpec((1, 1, d, dh), lambda i, be, nv: (layer, be[i], 0, 0)),
                  pl.BlockSpec((1, 1, 1, dh), lambda i, be, nv: (layer, be[i], 0, 0)),
                  pl.BlockSpec((1, 1, D_EXPERT, d), lambda i, be, nv: (layer, be[i], 0, 0)),
                  pl.BlockSpec((1, 1, 1, d), lambda i, be, nv: (layer, be[i], 0, 0))],
        out_specs=pl.BlockSpec((bm, d), lambda i, be, nv: (i, 0)),
        scratch_shapes=[pltpu.VMEM((d, dh), BF16), pltpu.VMEM((D_EXPERT, d), BF16)])
    return pl.pallas_call(
        _moe_kernel, grid_spec=grid_spec, out_shape=jax.ShapeDtypeStruct((p_rows, d), F32),
        compiler_params=_cp(("arbitrary",)), name="moe_experts",
    )(block_e, n_valid, xg, gate.reshape(p_rows, 1), w1, b1.reshape(depth, ne, 1, dh), w2, b2.reshape(depth, ne, 1, d))


def moe_ffn(hn, idx, gates, rank, counts, layer, w1, b1, w2, b2):
    n, d = hn.shape
    bm = MOE_ROWS
    nk = n * TOP_K
    counts = counts.astype(jnp.int32)
    padded = (counts + bm - 1) // bm * bm
    pend = jnp.cumsum(padded)
    pstart = pend - padded
    experts = jnp.arange(N_EXPERTS, dtype=jnp.int32)
    pos = rank + jnp.sum(jnp.where(idx[..., None] == experts, pstart, 0), axis=-1)
    n_blocks = (nk + N_EXPERTS * (bm - 1) + bm - 1) // bm
    p_rows = n_blocks * bm
    real = (idx.reshape(nk) << KEY_EXPERT_SHIFT) + jnp.arange(nk, dtype=jnp.int32)
    fill = jnp.arange(bm - 1, dtype=jnp.int32)
    pad_keys = jnp.where(fill[None, :] < (padded - counts)[:, None],
                         (experts[:, None] << KEY_EXPERT_SHIFT) + (1 << KEY_PAD_BIT) + fill[None, :],
                         jnp.iinfo(jnp.int32).max)
    n_fill = p_rows - nk
    tail = jnp.full((n_fill - N_EXPERTS * (bm - 1),), jnp.iinfo(jnp.int32).max, jnp.int32)
    keys, buf_gate = lax.sort((jnp.concatenate([real, pad_keys.reshape(-1), tail]),
                               jnp.concatenate([gates.reshape(nk), jnp.zeros((n_fill,), F32)])), num_keys=1)
    is_pad = (keys >> KEY_PAD_BIT) & 1
    buf_tok = jnp.where(is_pad == 1, jnp.arange(p_rows, dtype=jnp.int32) % n, (keys & ((1 << KEY_PAD_BIT) - 1)) // TOP_K)
    block_start = jnp.arange(n_blocks, dtype=jnp.int32) * bm
    block_e = jnp.minimum(jnp.sum((pend[None, :] <= block_start[:, None]).astype(jnp.int32), axis=1), N_EXPERTS - 1)
    n_valid = (pend[-1:] // bm).astype(jnp.int32)
    y = moe_experts(hn[buf_tok], buf_gate, block_e, n_valid, layer, w1, b1, w2, b2)
    out = y[pos[:, 0]]
    for kk in range(1, TOP_K):
        out = out + y[pos[:, kk]]
    return out


def _rope_rot(w):
    h = MLA_ROPE // 4
    a, b, c, d = w[..., :h], w[..., h:2 * h], w[..., 2 * h:3 * h], w[..., 3 * h:]
    return jnp.concatenate([-b, a, -d, c], axis=-1)


def _prep_weights(w_in, mla_w_uq, mla_w_ukv):
    depth, d, _ = w_in.shape
    o = np.cumsum([0, 512, 512, GDN_QKV, 512, 8, 8, 512, 512, 256, 64, N_BRANCH * d])
    lu, lg, qkv, z, be, al, fu, cq, ckv, kr, mg = [w_in[..., o[i]:o[i + 1]] for i in range(11)]
    pad = jnp.zeros((depth, d, 128 - 16), w_in.dtype)
    w_main = jnp.concatenate([qkv, lu, lg, z, fu, cq, ckv, kr, _rope_rot(kr), be, al, pad], axis=-1).astype(BF16)
    w_gate = jnp.transpose(mg.reshape(depth, d, N_BRANCH, d), (0, 2, 1, 3)).astype(BF16)
    uq = mla_w_uq.reshape(depth, MLA_Q_RANK, MLA_HEADS, MLA_NOPE + MLA_ROPE)
    wq = jnp.concatenate([uq, _rope_rot(uq[..., MLA_NOPE:])], axis=-1).reshape(depth, MLA_Q_RANK, MLA_HEADS * 256)
    return w_main, w_gate, wq.astype(BF16), mla_w_ukv.astype(BF16)


def _rope_tables(ctx, n_lat):
    half = MLA_ROPE // 4
    inv = jnp.power(ROPE_BASE, -jnp.arange(half, dtype=F32) / half)
    t = jnp.arange(n_lat)
    row = (t // GRID_W).astype(F32)[:, None] * inv
    col = (t % GRID_W).astype(F32)[:, None] * inv
    cos = jnp.concatenate([jnp.cos(row), jnp.cos(row), jnp.cos(col), jnp.cos(col)], axis=1)
    sin = jnp.concatenate([jnp.sin(row), jnp.sin(row), jnp.sin(col), jnp.sin(col)], axis=1)
    cos = jnp.concatenate([jnp.ones((ctx, MLA_ROPE), F32), cos], axis=0)
    sin = jnp.concatenate([jnp.zeros((ctx, MLA_ROPE), F32), sin], axis=0)
    return cos, sin


def _chan_dft():
    idx = np.arange(FFT_GROUP)
    ang = 2.0 * np.pi * ((idx[:, None] * idx[None, :]) % FFT_GROUP) / FFT_GROUP
    eye = np.eye(FFT_GROUPS)
    return (jnp.asarray(np.kron(eye, np.cos(ang)), BF16), jnp.asarray(np.kron(eye, np.sin(ang)), BF16))


def kernel(x, c, ctx, c_ctx, w_ada, b_ada, norm1_g, norm2_g, w_in, lru_conv_w, lru_conv_b, lru_w_a, lru_b_a, lru_w_x, lru_b_x, lru_lambda, gdn_conv_w, gdn_conv_b, gdn_a_log, gdn_dt_bias, gdn_norm_g, mla_q_norm_g, mla_kv_norm_g, mla_w_uq, mla_w_ukv, w_branch, w_out, router_w, router_b, exp_w1, exp_b1, exp_w2, exp_b2, final_norm_g):
    nb, n_lat, d = x.shape
    n_ctx = ctx.shape[1]
    depth = w_ada.shape[0]
    s_len = n_ctx + n_lat
    m = nb * s_len
    tt = _pick(math.gcd(n_ctx, n_lat), (256, 128, 64))
    geo = dict(tt=tt, tiles_per_seq=s_len // tt, ctx_tiles=n_ctx // tt, batch=nb, s_len=s_len, ctx=n_ctx)

    mod_rows = 8 * ((nb + 1 + 7) // 8)
    cmat = jnp.zeros((mod_rows, d), F32).at[:nb].set(c).at[nb].set(c_ctx)
    mod_all = ada_table(cmat, w_ada, b_ada).reshape(depth, mod_rows * N_MOD, 1, d)

    cos_t, sin_t = _rope_tables(n_ctx, n_lat)
    chan_c, chan_s = _chan_dft()
    h = jnp.concatenate([ctx, x], axis=1).reshape(m, d)

    w_main_all, w_gate_all, wq_all, wkv_all = _prep_weights(w_in, mla_w_uq, mla_w_ukv)
    w_branch_all, w_out_all = w_branch.astype(BF16), w_out.astype(BF16)
    delta, prev_mod, gate_k = None, None, None
    for l in range(depth):
        w_main, w_gate, wq, wkv = w_main_all[l], w_gate_all[l], wq_all[l], wkv_all[l]
        p = dict(lru_conv_w=lru_conv_w[l], lru_conv_b=lru_conv_b[l], lru_w_a=lru_w_a[l].astype(BF16),
                 lru_b_a=lru_b_a[l], lru_w_x=lru_w_x[l].astype(BF16), lru_b_x=lru_b_x[l], lru_lambda=lru_lambda[l],
                 gdn_conv_w=gdn_conv_w[l], gdn_conv_b=gdn_conv_b[l], gdn_a_log=gdn_a_log[l],
                 gdn_dt_bias=gdn_dt_bias[l], gdn_norm_g=gdn_norm_g[l], mla_q_norm_g=mla_q_norm_g[l],
                 mla_kv_norm_g=mla_kv_norm_g[l], mla_wq=wq, mla_wkv=wkv, w_gate=w_gate,
                 w_branch=w_branch_all[l], w_out=w_out_all[l])
        mod = mod_all[l]
        h, hn = norm_mod(h, norm1_g[l], geo, mod=mod, shift_k=0, scale_k=1, delta=delta, gate_k=gate_k,
                         prev_mod=prev_mod)
        zin = matmul(hn, w_main)
        hf, hb = lru_scan(zin, p, geo)
        qkvn, bg = gdn_prep(zin, p, geo)
        of, ob = gdn_scan(qkvn, bg, geo)
        fu = zin[:, COL_FU:COL_FU + FFT_WIDTH].astype(BF16).reshape(nb, s_len, FFT_WIDTH)
        yc = jnp.concatenate([fourier(fu[:, :n_ctx], chan_c, chan_s), fourier(fu[:, n_ctx:], chan_c, chan_s)],
                             axis=1).reshape(m, FFT_WIDTH)
        q, k, v = mla_prep(zin, cos_t, sin_t, p, geo)
        yd_lat, yd_ctx = mla_attention(q, k, v, geo)
        yd = jnp.concatenate([yd_ctx, yd_lat], axis=1).reshape(m, MLA_HEADS * MLA_V)
        mixed = merge(hn, zin, hf, hb, of, ob, yc, yd, p)
        h, hn2, idx, gates, rank, counts = norm_mod(
            h, norm2_g[l], geo, mod=mod, shift_k=3, scale_k=4, delta=mixed, gate_k=2, prev_mod=mod,
            router=(router_w[l], router_b[l]))
        ffn = moe_ffn(hn2, idx, gates, rank, counts[0], l, exp_w1, exp_b1, exp_w2, exp_b2)
        delta, prev_mod, gate_k = ffn, mod, 5
    (out,) = norm_mod(h, final_norm_g, geo, delta=delta, gate_k=5, prev_mod=prev_mod, emit_h=False,
                      out_dtype=F32, lat_only=True)
    return out.reshape(nb, n_lat, d)
```

```python
import functools
import math

import jax
import jax.numpy as jnp
import numpy as np
from jax import lax
from jax.experimental import pallas as pl
from jax.experimental.pallas import tpu as pltpu

F32 = jnp.float32
BF16 = jnp.bfloat16

GRID_W = 64
NORM_EPS = 1e-6
N_MOD = 6
LRU_WIDTH = 512
LRU_BLOCKS = 4
LRU_BLOCK = 128
LRU_C = 8.0
GDN_HEADS = 4
GDN_DK = 128
GDN_DV = 128
GDN_CHUNK = 64
GDN_QKV = GDN_HEADS * (2 * GDN_DK + GDN_DV)
FFT_GROUPS = 4
FFT_GROUP = 128
FFT_WIDTH = 512
MLA_HEADS = 4
MLA_Q_RANK = 512
MLA_KV_RANK = 256
MLA_NOPE = 128
MLA_ROPE = 64
MLA_V = 128
MLA_QK_PAD = 256
ROPE_BASE = 10000.0
N_BRANCH = 4
BRANCH_WIDTH = 512
N_EXPERTS = 32
TOP_K = 4
D_EXPERT = 512
SWIGLU_LIMIT = 7.0
SWIGLU_ALPHA = 1.702
MOE_ROWS = 512
KEY_EXPERT_SHIFT = 20
KEY_PAD_BIT = 19

COL_QKV = 0
COL_LU = 1536
COL_LG = 2048
COL_Z = 2560
COL_FU = 3072
COL_CQ = 3584
COL_CKV = 4096
COL_MISC = 4352
COL_BA = 4480
ZIN_W = 4608

VMEM_LIMIT = 56 * 1024 * 1024
HALO = 8


def _cp(sem, vmem=VMEM_LIMIT):
    return pltpu.CompilerParams(dimension_semantics=sem, vmem_limit_bytes=vmem)


def _pick(n, cands):
    for c in cands:
        if n % c == 0:
            return c
    raise ValueError(f"no tile for {n} in {cands}")


def _mm(a, b):
    return jnp.dot(a.astype(BF16), b.astype(BF16), preferred_element_type=F32)


def _mm_nt(a, b):
    return lax.dot_general(a.astype(BF16), b.astype(BF16), (((1,), (1,)), ((), ())), preferred_element_type=F32)


def _mm_tn(a, b):
    return lax.dot_general(a.astype(BF16), b.astype(BF16), (((0,), (0,)), ((), ())), preferred_element_type=F32)


def _mm_f32(a, b):
    return jnp.dot(a, b, preferred_element_type=F32, precision=lax.Precision.HIGHEST)


def _softplus(y):
    return jnp.maximum(y, 0.0) + jnp.log1p(jnp.exp(-jnp.abs(y)))


def _ada_kernel(c_ref, w_ref, b_ref, o_ref):
    cv = c_ref[...]
    s = cv * jax.nn.sigmoid(cv)
    o_ref[0] = _mm(s, w_ref[0]) + b_ref[0]


def ada_table(cmat, w_ada, b_ada):
    depth, d, n = w_ada.shape
    rows = cmat.shape[0]
    tn = _pick(n, (1024, 512, 256, 128))
    return pl.pallas_call(
        _ada_kernel,
        grid=(depth, n // tn),
        in_specs=[pl.BlockSpec((rows, d), lambda l, j: (0, 0)),
                  pl.BlockSpec((1, d, tn), lambda l, j: (l, 0, j)),
                  pl.BlockSpec((1, 1, tn), lambda l, j: (l, 0, j))],
        out_specs=pl.BlockSpec((1, rows, tn), lambda l, j: (l, 0, j)),
        out_shape=jax.ShapeDtypeStruct((depth, rows, n), F32),
        compiler_params=_cp(("arbitrary", "arbitrary")),
        name="ada_table",
    )(cmat, w_ada, b_ada.reshape(depth, 1, n))


def _norm_kernel(*refs, has_delta, delta_is_rows, has_mod, has_router, emit_h):
    it = iter(refs)
    h_ref = next(it)
    if has_delta:
        d_ref = next(it)
        gate_ref = next(it)
    g_ref = next(it)
    if has_mod:
        shift_ref = next(it)
        scale_ref = next(it)
    if has_router:
        rw_ref = next(it)
        rb_ref = next(it)
    if emit_h:
        ho_ref = next(it)
    y_ref = next(it)
    if has_router:
        idx_ref = next(it)
        gt_ref = next(it)
        rank_ref = next(it)
        cnt_ref = next(it)

    h = h_ref[...]
    if has_delta and delta_is_rows:
        tt = h.shape[0]
        rows = lax.broadcasted_iota(jnp.int32, (tt, TOP_K * tt), 0)
        cols = lax.broadcasted_iota(jnp.int32, (tt, TOP_K * tt), 1)
        pick = (jnp.right_shift(cols, TOP_K.bit_length() - 1) == rows).astype(BF16)
        h = h + gate_ref[0] * jnp.dot(pick, d_ref[...], preferred_element_type=F32)
    elif has_delta:
        h = h + gate_ref[0] * d_ref[...].astype(F32)
    if emit_h:
        ho_ref[...] = h
    y = h * lax.rsqrt(jnp.mean(h * h, axis=-1, keepdims=True) + NORM_EPS) * g_ref[...]
    if has_mod:
        y = y * (1.0 + scale_ref[0]) + shift_ref[0]
    y_ref[...] = y.astype(y_ref.dtype)
    if has_router:
        @pl.when(pl.program_id(0) == 0)
        def _():
            cnt_ref[...] = jnp.zeros_like(cnt_ref)

        logit = _mm_f32(y, rw_ref[...]) + rb_ref[...]
        tt, ne = logit.shape
        lane = lax.broadcasted_iota(jnp.int32, (tt, ne), 1).astype(F32)
        vals, hots = [], []
        for kk in range(TOP_K):
            top = jnp.max(logit, axis=-1, keepdims=True)
            arg = jnp.min(jnp.where(logit == top, lane, float(ne)), axis=-1, keepdims=True)
            hot = lane == arg
            idx_ref[:, kk:kk + 1] = arg.astype(jnp.int32)
            vals.append(top)
            hots.append(hot)
            logit = jnp.where(hot, -jnp.inf, logit)
        exps = [jnp.exp(v - vals[0]) for v in vals]
        den = exps[0] + exps[1] + exps[2] + exps[3]
        for kk in range(TOP_K):
            gt_ref[:, kk:kk + 1] = exps[kk] / den
        hot_all = hots[0].astype(F32) + hots[1].astype(F32) + hots[2].astype(F32) + hots[3].astype(F32)
        ii = lax.broadcasted_iota(jnp.int32, (tt, tt), 0)
        jj = lax.broadcasted_iota(jnp.int32, (tt, tt), 1)
        before = _mm((jj < ii).astype(F32), hot_all) + cnt_ref[...]
        for kk in range(TOP_K):
            rank_ref[:, kk:kk + 1] = jnp.sum(jnp.where(hots[kk], before, 0.0), axis=-1, keepdims=True).astype(jnp.int32)
        cnt_ref[...] += jnp.sum(hot_all, axis=0, keepdims=True)


def norm_mod(h, g, geo, *, mod=None, shift_k=None, scale_k=None, delta=None, gate_k=None, prev_mod=None,
             router=None, emit_h=True, out_dtype=None, lat_only=False):
    m, d = h.shape
    out_dtype = BF16 if out_dtype is None else out_dtype
    tt, tps, ctx_t, nb = geo["tt"], geo["tiles_per_seq"], geo["ctx_tiles"], geo["batch"]
    if lat_only:
        lat_t = tps - ctx_t
        grid = (nb * lat_t,)
        in_row = lambda i: (i // lat_t) * tps + ctx_t + i % lat_t
        mod_row = lambda i: i // lat_t
        out_rows = nb * lat_t * tt
    else:
        grid = (m // tt,)
        in_row = lambda i: i
        mod_row = lambda i: jnp.where(i % tps < ctx_t, nb, i // tps)
        out_rows = m
    tok = lambda w: pl.BlockSpec((tt, w), lambda i: (in_row(i), 0))
    out_tok = lambda w: pl.BlockSpec((tt, w), lambda i: (i, 0))

    def mod_spec(k):
        return pl.BlockSpec((1, 1, d), lambda i: (mod_row(i) * N_MOD + k, 0, 0))

    args, specs = [h], [tok(d)]
    delta_is_rows = delta is not None and delta.shape[0] == TOP_K * m
    if delta is not None:
        args += [delta, prev_mod]
        specs += [pl.BlockSpec((TOP_K * tt, d), lambda i: (in_row(i), 0)) if delta_is_rows else tok(d),
                  mod_spec(gate_k)]
    args.append(g.reshape(1, d))
    specs.append(pl.BlockSpec((1, d), lambda i: (0, 0)))
    if mod is not None:
        args += [mod, mod]
        specs += [mod_spec(shift_k), mod_spec(scale_k)]
    if router is not None:
        rw, rb = router
        ne = rw.shape[1]
        args += [rw, rb.reshape(1, ne)]
        specs += [pl.BlockSpec(rw.shape, lambda i: (0, 0)), pl.BlockSpec((1, ne), lambda i: (0, 0))]
    out_shape, out_specs = [], []
    if emit_h:
        out_shape.append(jax.ShapeDtypeStruct((out_rows, d), F32))
        out_specs.append(out_tok(d))
    out_shape.append(jax.ShapeDtypeStruct((out_rows, d), out_dtype))
    out_specs.append(out_tok(d))
    if router is not None:
        out_shape += [jax.ShapeDtypeStruct((out_rows, TOP_K), jnp.int32), jax.ShapeDtypeStruct((out_rows, TOP_K), F32),
                      jax.ShapeDtypeStruct((out_rows, TOP_K), jnp.int32), jax.ShapeDtypeStruct((1, ne), F32)]
        out_specs += [out_tok(TOP_K), out_tok(TOP_K), out_tok(TOP_K), pl.BlockSpec((1, ne), lambda i: (0, 0))]
    kern = functools.partial(_norm_kernel, has_delta=delta is not None, delta_is_rows=delta_is_rows,
                             has_mod=mod is not None,
                             has_router=router is not None, emit_h=emit_h)
    return pl.pallas_call(kern, grid=grid, in_specs=specs, out_specs=out_specs, out_shape=out_shape,
                          compiler_params=_cp(("arbitrary",)), name="norm_mod")(*args)


def _matmul_kernel(x_ref, w_ref, o_ref):
    o_ref[...] = jnp.dot(x_ref[...], w_ref[...], preferred_element_type=F32).astype(o_ref.dtype)


def matmul(x, w, out_dtype=F32):
    m, k = x.shape
    n = w.shape[1]
    tm = _pick(m, (1024, 512, 256, 128))
    tn = _pick(n, (768, 512, 256, 128))
    return pl.pallas_call(
        _matmul_kernel,
        grid=(m // tm, n // tn),
        in_specs=[pl.BlockSpec((tm, k), lambda i, j: (i, 0)), pl.BlockSpec((k, tn), lambda i, j: (0, j))],
        out_specs=pl.BlockSpec((tm, tn), lambda i, j: (i, j)),
        out_shape=jax.ShapeDtypeStruct((m, n), out_dtype),
        compiler_params=_cp(("arbitrary", "arbitrary")),
        name="in_proj",
    )(x, w)


def _conv4(x, xp, xn, prev_ok, next_ok, cw_ref, cb_ref, ext_ref):
    tt = x.shape[0]
    ext_ref[0:HALO, :] = xp * prev_ok
    ext_ref[HALO:HALO + tt, :] = x
    ext_ref[HALO + tt:2 * HALO + tt, :] = xn * next_ok
    y = cb_ref[...] + cw_ref[2:3, :] * x
    y = y + cw_ref[0:1, :] * ext_ref[HALO - 2:HALO - 2 + tt, :]
    y = y + cw_ref[1:2, :] * ext_ref[HALO - 1:HALO - 1 + tt, :]
    y = y + cw_ref[3:4, :] * ext_ref[HALO + 1:HALO + 1 + tt, :]
    return y


def _seg_edges(tile, ctx_t, n_t):
    first = jnp.logical_or(tile == 0, tile == ctx_t)
    last = jnp.logical_or(tile == ctx_t - 1, tile == n_t - 1)
    return jnp.where(first, 0.0, 1.0).astype(F32), jnp.where(last, 0.0, 1.0).astype(F32)


def _rev_tile(j, ctx_t, n_t):
    return jnp.where(j < ctx_t, ctx_t - 1 - j, n_t - 1 - (j - ctx_t))


def _halo_specs(width, col_block, tile_of, tt, tps, m):
    r8 = tt // HALO
    last8 = m // HALO - 1

    def cur(b, j):
        return (b * tps + tile_of(j), col_block)

    def prev(b, j):
        return (jnp.maximum((b * tps + tile_of(j)) * r8 - 1, 0), col_block)

    def nxt(b, j):
        return (jnp.minimum((b * tps + tile_of(j) + 1) * r8, last8), col_block)

    return [pl.BlockSpec((tt, width), cur), pl.BlockSpec((HALO, width), prev), pl.BlockSpec((HALO, width), nxt)]


def _lru_kernel(xf_ref, xfp_ref, xfn_ref, xb_ref, xbp_ref, xbn_ref, cw_ref, cb_ref, wa_ref, ba_ref, wx_ref, bx_ref,
                lam_ref, hf_ref, hb_ref, ext_s, a_s, b_s, carry_s, *, tt, ctx_t, n_t):
    j = pl.program_id(1)

    @pl.when(j == 0)
    def _():
        carry_s[...] = jnp.zeros_like(carry_s)

    rows = lax.broadcasted_iota(jnp.int32, (HALO, LRU_WIDTH), 0)
    for d, (x_ref, xp_ref, xn_ref, out_ref) in enumerate(((xf_ref, xfp_ref, xfn_ref, hf_ref),
                                                          (xb_ref, xbp_ref, xbn_ref, hb_ref))):
        tile = j if d == 0 else _rev_tile(j, ctx_t, n_t)
        prev_ok, next_ok = _seg_edges(tile, ctx_t, n_t)
        xc = _conv4(x_ref[...], xp_ref[...], xn_ref[...], prev_ok, next_ok, cw_ref, cb_ref, ext_s)
        sp = _softplus(-lam_ref[d:d + 1, :])
        for g in range(LRU_BLOCKS):
            sl = slice(g * LRU_BLOCK, (g + 1) * LRU_BLOCK)
            xg = xc[:, sl]
            r = jax.nn.sigmoid(_mm(xg, wa_ref[d, g]) + ba_ref[d:d + 1, sl])
            ig = jax.nn.sigmoid(_mm(xg, wx_ref[d, g]) + bx_ref[d:d + 1, sl])
            log_a = -LRU_C * r * sp[:, sl]
            a = jnp.exp(log_a)
            a_s[d, :, sl] = a
            b_s[d, :, sl] = jnp.sqrt(-jnp.tanh(log_a) * (a * a + 1.0)) * (ig * xg)

        n_grp = tt // HALO

        def group(gi, carry, d=d, out_ref=out_ref):
            g0 = gi if d == 0 else n_grp - 1 - gi
            off = pl.multiple_of(g0 * HALO, HALO)
            av = a_s[d, pl.ds(off, HALO), :]
            bv = b_s[d, pl.ds(off, HALO), :]
            for s in (1, 2, 4):
                shift = s if d == 0 else HALO - s
                a_sh = pltpu.roll(av, shift, 0)
                b_sh = pltpu.roll(bv, shift, 0)
                ok = (rows >= s) if d == 0 else (rows < HALO - s)
                bv = jnp.where(ok, av * b_sh + bv, bv)
                av = jnp.where(ok, av * a_sh, av)
            hv = bv + av * carry
            out_ref[pl.ds(off, HALO), :] = hv
            return hv[HALO - 1:HALO, :] if d == 0 else hv[0:1, :]

        carry_s[d:d + 1, :] = lax.fori_loop(0, n_grp, group, carry_s[d:d + 1, :])


def lru_scan(zin, p, geo):
    m = zin.shape[0]
    tt, tps, ctx_t, nb = geo["tt"], geo["tiles_per_seq"], geo["ctx_tiles"], geo["batch"]
    cb = COL_LU // LRU_WIDTH
    fwd = _halo_specs(LRU_WIDTH, cb, lambda j: j, tt, tps, m)
    bwd = _halo_specs(LRU_WIDTH, cb, lambda j: _rev_tile(j, ctx_t, tps), tt, tps, m)
    full = lambda shape: pl.BlockSpec(shape, lambda b, j: (0,) * len(shape))
    w_specs = [full((4, LRU_WIDTH)), full((1, LRU_WIDTH)), full((2, LRU_BLOCKS, LRU_BLOCK, LRU_BLOCK)),
               full((2, LRU_WIDTH)), full((2, LRU_BLOCKS, LRU_BLOCK, LRU_BLOCK)), full((2, LRU_WIDTH)),
               full((2, LRU_WIDTH))]
    out_f = pl.BlockSpec((tt, LRU_WIDTH), lambda b, j: (b * tps + j, 0))
    out_b = pl.BlockSpec((tt, LRU_WIDTH), lambda b, j: (b * tps + _rev_tile(j, ctx_t, tps), 0))
    kern = functools.partial(_lru_kernel, tt=tt, ctx_t=ctx_t, n_t=tps)
    return pl.pallas_call(
        kern, grid=(nb, tps), in_specs=fwd + bwd + w_specs, out_specs=[out_f, out_b],
        out_shape=[jax.ShapeDtypeStruct((m, LRU_WIDTH), F32)] * 2,
        scratch_shapes=[pltpu.VMEM((tt + 2 * HALO, LRU_WIDTH), F32), pltpu.VMEM((2, tt, LRU_WIDTH), F32),
                        pltpu.VMEM((2, tt, LRU_WIDTH), F32), pltpu.VMEM((2, LRU_WIDTH), F32)],
        compiler_params=_cp(("arbitrary", "arbitrary")), name="lru_scan",
    )(zin, zin, zin, zin, zin, zin, p["lru_conv_w"], p["lru_conv_b"].reshape(1, -1), p["lru_w_a"], p["lru_b_a"],
      p["lru_w_x"], p["lru_b_x"], p["lru_lambda"])


def _gdn_prep_kernel(x_ref, xp_ref, xn_ref, ba_ref, cw_ref, cb_ref, nega_ref, dtb_ref, qkv_ref, bg_ref, ext_s,
                     *, ctx_t, n_t):
    tile = pl.program_id(1)
    prev_ok, next_ok = _seg_edges(tile, ctx_t, n_t)
    xc = _conv4(x_ref[...], xp_ref[...], xn_ref[...], prev_ok, next_ok, cw_ref, cb_ref, ext_s)
    xc = xc * jax.nn.sigmoid(xc)
    nq = GDN_HEADS * GDN_DK
    for hd in range(GDN_HEADS):
        for base, scale in ((0, GDN_DK ** -0.5), (nq, 1.0)):
            sl = slice(base + hd * GDN_DK, base + (hd + 1) * GDN_DK)
            v = xc[:, sl]
            qkv_ref[:, sl] = v * (lax.rsqrt(jnp.sum(v * v, axis=-1, keepdims=True) + NORM_EPS) * scale)
    qkv_ref[:, 2 * nq:] = xc[:, 2 * nq:]
    ba = ba_ref[...]
    cols = lax.broadcasted_iota(jnp.int32, ba.shape, 1)
    gval = nega_ref[...] * _softplus(ba + dtb_ref[...])
    bg_ref[...] = jnp.where(cols < 2 * GDN_HEADS, jax.nn.sigmoid(ba), gval)


def gdn_prep(zin, p, geo):
    m = zin.shape[0]
    tt, tps, ctx_t, nb = geo["tt"], geo["tiles_per_seq"], geo["ctx_tiles"], geo["batch"]
    specs = _halo_specs(GDN_QKV, COL_QKV // GDN_QKV, lambda j: j, tt, tps, m)
    specs.append(pl.BlockSpec((tt, 128), lambda b, j: (b * tps + j, COL_BA // 128)))
    full = lambda shape: pl.BlockSpec(shape, lambda b, j: (0,) * len(shape))
    specs += [full((4, GDN_QKV)), full((1, GDN_QKV)), full((1, 128)), full((1, 128))]
    pad = jnp.zeros((1, 128), F32)
    nega = pad.at[0, 8:16].set(-jnp.exp(p["gdn_a_log"].reshape(-1)))
    dtb = pad.at[0, 8:16].set(p["gdn_dt_bias"].reshape(-1))
    kern = functools.partial(_gdn_prep_kernel, ctx_t=ctx_t, n_t=tps)
    return pl.pallas_call(
        kern, grid=(nb, tps), in_specs=specs,
        out_specs=[pl.BlockSpec((tt, GDN_QKV), lambda b, j: (b * tps + j, 0)),
                   pl.BlockSpec((tt, 128), lambda b, j: (b * tps + j, 0))],
        out_shape=[jax.ShapeDtypeStruct((m, GDN_QKV), F32), jax.ShapeDtypeStruct((m, 128), F32)],
        scratch_shapes=[pltpu.VMEM((tt + 2 * HALO, GDN_QKV), F32)],
        compiler_params=_cp(("arbitrary", "arbitrary")), name="gdn_prep",
    )(zin, zin, zin, zin, p["gdn_conv_w"], p["gdn_conv_b"].reshape(1, -1), nega, dtb)


def _gdn_kernel(xf_ref, bgf_ref, xb_ref, bgb_ref, of_ref, ob_ref, s_s, *, gb):
    j = pl.program_id(1)

    @pl.when(j == 0)
    def _():
        s_s[...] = jnp.zeros_like(s_s)

    c = GDN_CHUNK
    nq = GDN_HEADS * GDN_DK
    ii = lax.broadcasted_iota(jnp.int32, (c, c), 0)
    jj = lax.broadcasted_iota(jnp.int32, (c, c), 1)
    incl = ((ii >= jj), (ii <= jj))
    strict = ((ii > jj), (ii < jj))
    blk = jnp.right_shift(ii, 4) == jnp.right_shift(jj, 4)
    eye = (ii == jj).astype(F32)

    chains = []
    for g in range(gb):
        for d, (x_ref, bg_ref) in enumerate(((xf_ref, bgf_ref), (xb_ref, bgb_ref))):
            bg = bg_ref[g]
            gam_all = _mm_f32(incl[d].astype(F32), bg)
            gam_t = gam_all.T
            for hd in range(GDN_HEADS):
                cb = d * GDN_HEADS + hd
                cg = 2 * GDN_HEADS + cb
                chains.append(dict(
                    g=g, d=d, hd=hd,
                    q=x_ref[g, :, hd * GDN_DK:(hd + 1) * GDN_DK],
                    k=x_ref[g, :, nq + hd * GDN_DK:nq + (hd + 1) * GDN_DK],
                    v=x_ref[g, :, 2 * nq + hd * GDN_DV:2 * nq + (hd + 1) * GDN_DV],
                    beta=bg[:, cb:cb + 1], gam=gam_all[:, cg:cg + 1], gam_row=gam_t[cg:cg + 1, :]))

    def put(name, fn):
        vals = [fn(ch) for ch in chains]
        for ch, val in zip(chains, vals):
            ch[name] = val

    put("dec_incl", lambda ch: jnp.where(
        incl[ch["d"]], jnp.exp(jnp.where(incl[ch["d"]], ch["gam"] - ch["gam_row"], 0.0)), 0.0))
    put("dec_strict", lambda ch: jnp.where(strict[ch["d"]], ch["dec_incl"], 0.0))
    put("g_last", lambda ch: ch["gam"][0:1, :] if ch["d"] == 1 else ch["gam"][c - 1:c, :])
    put("eg", lambda ch: jnp.exp(ch["gam"]))
    put("kb", lambda ch: ch["k"] * ch["beta"])
    put("a", lambda ch: _mm_nt(ch["kb"], ch["k"]) * ch["dec_strict"])
    put("rhs", lambda ch: jnp.concatenate([ch["v"] * ch["beta"], ch["kb"] * ch["eg"]], axis=1))
    put("a_d", lambda ch: jnp.where(blk, ch["a"], 0.0))
    put("a_o", lambda ch: ch["a"] - ch["a_d"])
    put("a2", lambda ch: _mm(ch["a_d"], ch["a_d"]))
    put("a4", lambda ch: _mm(ch["a2"], ch["a2"]))
    put("a8", lambda ch: _mm(ch["a4"], ch["a4"]))
    put("dinv", lambda ch: eye - ch["a_d"])
    put("dinv", lambda ch: ch["dinv"] + _mm(ch["dinv"], ch["a2"]))
    put("dinv", lambda ch: ch["dinv"] + _mm(ch["dinv"], ch["a4"]))
    put("dinv", lambda ch: ch["dinv"] + _mm(ch["dinv"], ch["a8"]))
    put("n1", lambda ch: _mm(ch["dinv"], ch["a_o"]))
    put("n2", lambda ch: _mm(ch["n1"], ch["n1"]))
    put("x", lambda ch: _mm(ch["dinv"], ch["rhs"]))
    put("x", lambda ch: ch["x"] + _mm(ch["n2"], ch["x"]))
    put("x", lambda ch: ch["x"] - _mm(ch["n1"], ch["x"]))
    put("qk", lambda ch: _mm_nt(ch["q"], ch["k"]) * ch["dec_incl"])
    put("s", lambda ch: s_s[ch["g"], ch["d"], ch["hd"]])
    put("v_new", lambda ch: ch["x"][:, :GDN_DV] - _mm(ch["x"][:, GDN_DV:], ch["s"]))
    put("o", lambda ch: _mm(ch["q"] * ch["eg"], ch["s"]) + _mm(ch["qk"], ch["v_new"]))
    put("s_new", lambda ch: ch["s"] * jnp.exp(ch["g_last"])
        + _mm_tn(ch["k"] * jnp.exp(ch["g_last"] - ch["gam"]), ch["v_new"]))
    for ch in chains:
        o_ref = of_ref if ch["d"] == 0 else ob_ref
        o_ref[ch["g"], :, ch["hd"] * GDN_DV:(ch["hd"] + 1) * GDN_DV] = ch["o"]
        s_s[ch["g"], ch["d"], ch["hd"]] = ch["s_new"]


def gdn_scan(qkvn, bg, geo):
    m = qkvn.shape[0]
    c = GDN_CHUNK
    nb, s_len, ctx = geo["batch"], geo["s_len"], geo["ctx"]
    gb = 2 if nb % 2 == 0 else 1
    n_c, ctx_c = s_len // c, ctx // c
    fwd = lambda b, j: (b, j, 0)
    bwd = lambda b, j: (b, _rev_tile(j, ctx_c, n_c), 0)
    w = GDN_HEADS * GDN_DV
    kern = functools.partial(_gdn_kernel, gb=gb)
    qkv3 = qkvn.reshape(nb, s_len, GDN_QKV)
    bg3 = bg.reshape(nb, s_len, 128)
    of, ob = pl.pallas_call(
        kern, grid=(nb // gb, n_c),
        in_specs=[pl.BlockSpec((gb, c, GDN_QKV), fwd), pl.BlockSpec((gb, c, 128), fwd),
                  pl.BlockSpec((gb, c, GDN_QKV), bwd), pl.BlockSpec((gb, c, 128), bwd)],
        out_specs=[pl.BlockSpec((gb, c, w), fwd), pl.BlockSpec((gb, c, w), bwd)],
        out_shape=[jax.ShapeDtypeStruct((nb, s_len, w), F32)] * 2,
        scratch_shapes=[pltpu.VMEM((gb, 2, GDN_HEADS, GDN_DK, GDN_DV), F32)],
        compiler_params=_cp(("arbitrary", "arbitrary")), name="gdn_scan",
    )(qkv3, bg3, qkv3, bg3)
    return of.reshape(m, w), ob.reshape(m, w)


def _fourier_kernel(c_ref, s_ref, u_ref, cc_ref, sc_ref, o_ref, *, scale):
    u = u_ref[0]
    a = jnp.dot(c_ref[...], u, preferred_element_type=F32)
    b = jnp.dot(s_ref[...], u, preferred_element_type=F32)
    o = _mm(a, cc_ref[...]) - _mm(b, sc_ref[...])
    o_ref[0] = o * scale


def _dft_mats(t):
    idx = jnp.arange(t, dtype=jnp.int32)
    ang = ((idx[:, None] * idx[None, :]) % t).astype(F32) * (2.0 * math.pi / t)
    return jnp.cos(ang).astype(BF16), jnp.sin(ang).astype(BF16)


def fourier(u, chan_c, chan_s):
    nb, t, w = u.shape
    cm, sm = _dft_mats(t)
    tf = _pick(t, (512, 256, 128))
    kern = functools.partial(_fourier_kernel, scale=1.0 / math.sqrt(t * FFT_GROUP))
    return pl.pallas_call(
        kern, grid=(t // tf, nb),
        in_specs=[pl.BlockSpec((tf, t), lambda f, b: (f, 0)), pl.BlockSpec((tf, t), lambda f, b: (f, 0)),
                  pl.BlockSpec((1, t, w), lambda f, b: (b, 0, 0)),
                  pl.BlockSpec((w, w), lambda f, b: (0, 0)), pl.BlockSpec((w, w), lambda f, b: (0, 0))],
        out_specs=pl.BlockSpec((1, tf, w), lambda f, b: (b, f, 0)),
        out_shape=jax.ShapeDtypeStruct((nb, t, w), F32),
        compiler_params=_cp(("arbitrary", "arbitrary")), name="fourier",
    )(cm, sm, u, chan_c, chan_s)


def _mla_prep_kernel(cq_ref, ckv_ref, misc_ref, cos_ref, sin_ref, gq_ref, gkv_ref, wq_ref, wkv_ref,
                     q_ref, k_ref, v_ref, *, q_scale):
    def rms(x, g):
        return x * lax.rsqrt(jnp.mean(x * x, axis=-1, keepdims=True) + NORM_EPS) * g

    cos = cos_ref[...]
    sin = sin_ref[...]
    qa = _mm(rms(cq_ref[...], gq_ref[...]), wq_ref[...])
    kva = _mm(rms(ckv_ref[...], gkv_ref[...]), wkv_ref[...])
    misc = misc_ref[...]
    k_rope = misc[:, :MLA_ROPE] * cos + misc[:, MLA_ROPE:] * sin
    zpad = jnp.zeros((cos.shape[0], MLA_QK_PAD - MLA_NOPE - MLA_ROPE), F32)
    for hd in range(MLA_HEADS):
        qh = qa[:, hd * 256:(hd + 1) * 256]
        q_rope = qh[:, MLA_NOPE:MLA_NOPE + MLA_ROPE] * cos + qh[:, MLA_NOPE + MLA_ROPE:] * sin
        q_ref[0, hd] = (jnp.concatenate([qh[:, :MLA_NOPE], q_rope, zpad], axis=1) * q_scale).astype(BF16)
        kh = kva[:, hd * 256:hd * 256 + MLA_NOPE]
        k_ref[0, hd] = jnp.concatenate([kh, k_rope, zpad], axis=1).astype(BF16)
        v_ref[0, hd] = kva[:, hd * 256 + MLA_NOPE:(hd + 1) * 256].astype(BF16)


def mla_prep(zin, cos_t, sin_t, p, geo):
    tt, tps, nb, s_len = geo["tt"], geo["tiles_per_seq"], geo["batch"], geo["s_len"]
    row = lambda b, j: b * tps + j
    full = lambda shape: pl.BlockSpec(shape, lambda b, j: (0,) * len(shape))
    ctx_t = geo["ctx_tiles"]
    hs = lambda w: pl.BlockSpec((1, MLA_HEADS, tt, w),
                                lambda b, j: (b, 0, jnp.where(j < ctx_t, j + (tps - ctx_t), j - ctx_t), 0))
    kern = functools.partial(_mla_prep_kernel, q_scale=(MLA_NOPE + MLA_ROPE) ** -0.5)
    return pl.pallas_call(
        kern, grid=(nb, tps),
        in_specs=[pl.BlockSpec((tt, MLA_Q_RANK), lambda b, j: (row(b, j), COL_CQ // MLA_Q_RANK)),
                  pl.BlockSpec((tt, MLA_KV_RANK), lambda b, j: (row(b, j), COL_CKV // MLA_KV_RANK)),
                  pl.BlockSpec((tt, 128), lambda b, j: (row(b, j), COL_MISC // 128)),
                  pl.BlockSpec((tt, MLA_ROPE), lambda b, j: (j, 0)), pl.BlockSpec((tt, MLA_ROPE), lambda b, j: (j, 0)),
                  full((1, MLA_Q_RANK)), full((1, MLA_KV_RANK)),
                  full((MLA_Q_RANK, MLA_HEADS * 256)), full((MLA_KV_RANK, MLA_HEADS * 256))],
        out_specs=[hs(MLA_QK_PAD), hs(MLA_QK_PAD), hs(MLA_V)],
        out_shape=[jax.ShapeDtypeStruct((nb, MLA_HEADS, s_len, MLA_QK_PAD), BF16),
                   jax.ShapeDtypeStruct((nb, MLA_HEADS, s_len, MLA_QK_PAD), BF16),
                   jax.ShapeDtypeStruct((nb, MLA_HEADS, s_len, MLA_V), BF16)],
        compiler_params=_cp(("arbitrary", "arbitrary")), name="mla_prep",
    )(zin, zin, zin, cos_t, sin_t, p["mla_q_norm_g"].reshape(1, -1), p["mla_kv_norm_g"].reshape(1, -1),
      p["mla_wq"], p["mla_wkv"])


def _attn_kernel(q_ref, k_ref, v_ref, o_ref):
    s = lax.dot_general(q_ref[0, 0], k_ref[0, 0], (((1,), (1,)), ((), ())), preferred_element_type=F32)
    pr = jnp.exp(s - jnp.max(s, axis=-1, keepdims=True))
    den = jnp.sum(pr, axis=-1, keepdims=True)
    o_ref[0] = jnp.dot(pr.astype(BF16), v_ref[0, 0], preferred_element_type=F32) / den


def mla_attention(q, k, v, geo):
    nb, nh, s_len, _ = q.shape
    n_ctx = geo["ctx"]
    n_lat = s_len - n_ctx
    assert n_lat % n_ctx == 0

    def call(tq, q_block0, n_q, key_rows, key_block):
        return pl.pallas_call(
            _attn_kernel, grid=(nb, nh, n_q // tq),
            in_specs=[pl.BlockSpec((1, 1, tq, MLA_QK_PAD), lambda b, h, i: (b, h, q_block0 + i, 0)),
                      pl.BlockSpec((1, 1, key_rows, MLA_QK_PAD), lambda b, h, i: (b, h, key_block, 0)),
                      pl.BlockSpec((1, 1, key_rows, MLA_V), lambda b, h, i: (b, h, key_block, 0))],
            out_specs=pl.BlockSpec((1, tq, MLA_V), lambda b, h, i: (b, i, h)),
            out_shape=jax.ShapeDtypeStruct((nb, n_q, nh * MLA_V), F32),
            compiler_params=_cp(("arbitrary", "arbitrary", "arbitrary")), name="mla_attention",
        )(q, k, v)

    tq_lat = _pick(n_lat, (256, 128))
    tq_ctx = _pick(n_ctx, (256, 128, 64))
    o_lat = call(tq_lat, 0, n_lat, s_len, 0)
    o_ctx = call(tq_ctx, n_lat // tq_ctx, n_ctx, n_ctx, n_lat // n_ctx)
    return o_lat, o_ctx


def _merge_kernel(hn_ref, hf_ref, hb_ref, lg_ref, of_ref, ob_ref, z_ref, gg_ref, yc_ref, yd_ref, wg_ref, wb_ref,
                  wo_ref, o_ref, y_s, acc_s, *, n_j):
    j = pl.program_id(1)

    @pl.when(j == 0)
    def _():
        lg = lg_ref[...]
        gelu = 0.5 * lg * (1.0 + jnp.tanh(math.sqrt(2.0 / math.pi) * (lg + 0.044715 * (lg * lg * lg))))
        y_s[0] = (gelu * (hf_ref[...] + hb_ref[...])).astype(BF16)
        for hd in range(GDN_HEADS):
            sl = slice(hd * GDN_DV, (hd + 1) * GDN_DV)
            o = of_ref[:, sl] + ob_ref[:, sl]
            o = o * lax.rsqrt(jnp.mean(o * o, axis=-1, keepdims=True) + NORM_EPS) * gg_ref[...]
            z = z_ref[:, sl]
            y_s[1, :, sl] = (o * (z * jax.nn.sigmoid(z))).astype(BF16)
        y_s[2] = yc_ref[...].astype(BF16)
        y_s[3] = yd_ref[...].astype(BF16)
        acc_s[...] = jnp.zeros_like(acc_s)

    hn = hn_ref[...]
    merged = None
    for i in range(N_BRANCH):
        gate = jax.nn.sigmoid(jnp.dot(hn, wg_ref[i], preferred_element_type=F32))
        term = gate * jnp.dot(y_s[i], wb_ref[i], preferred_element_type=F32)
        merged = term if merged is None else merged + term
    acc_s[...] += jnp.dot(merged.astype(BF16), wo_ref[...], preferred_element_type=F32)

    @pl.when(j == n_j - 1)
    def _():
        o_ref[...] = acc_s[...].astype(o_ref.dtype)


def merge(hn, zin, hf, hb, of, ob, yc, yd, p):
    m, d = hn.shape
    tm = _pick(m, (512, 256, 128))
    tn = 256
    n_j = d // tn
    w = BRANCH_WIDTH
    tok = lambda a_w, cb: pl.BlockSpec((tm, a_w), lambda i, j: (i, cb))
    kern = functools.partial(_merge_kernel, n_j=n_j)
    return pl.pallas_call(
        kern, grid=(m // tm, n_j),
        in_specs=[tok(d, 0), tok(w, 0), tok(w, 0), tok(w, COL_LG // w), tok(w, 0), tok(w, 0), tok(w, COL_Z // w),
                  pl.BlockSpec((1, GDN_DV), lambda i, j: (0, 0)), tok(w, 0), tok(w, 0),
                  pl.BlockSpec((N_BRANCH, d, tn), lambda i, j: (0, 0, j)),
                  pl.BlockSpec((N_BRANCH, w, tn), lambda i, j: (0, 0, j)),
                  pl.BlockSpec((tn, d), lambda i, j: (j, 0))],
        out_specs=pl.BlockSpec((tm, d), lambda i, j: (i, 0)),
        out_shape=jax.ShapeDtypeStruct((m, d), BF16),
        scratch_shapes=[pltpu.VMEM((N_BRANCH, tm, w), BF16), pltpu.VMEM((tm, d), F32)],
        compiler_params=_cp(("arbitrary", "arbitrary")), name="merge",
    )(hn, hf, hb, zin, of, ob, zin, p["gdn_norm_g"].reshape(1, -1), yc, yd, p["w_gate"], p["w_branch"], p["w_out"])


def _moe_kernel(be_ref, nv_ref, x_ref, g_ref, w1_ref, b1_ref, w2_ref, b2_ref, o_ref, w1_s, w2_s):
    i = pl.program_id(0)
    new_expert = jnp.logical_or(i == 0, be_ref[i] != be_ref[jnp.maximum(i - 1, 0)])

    @pl.when(new_expert)
    def _():
        w1_s[...] = w1_ref[0, 0].astype(BF16)
        w2_s[...] = w2_ref[0, 0].astype(BF16)

    @pl.when(i < nv_ref[0])
    def _():
        h = jnp.dot(x_ref[...], w1_s[...], preferred_element_type=F32) + b1_ref[0, 0]
        h_glu = jnp.minimum(h[:, :D_EXPERT], SWIGLU_LIMIT)
        h_lin = jnp.clip(h[:, D_EXPERT:], -SWIGLU_LIMIT, SWIGLU_LIMIT)
        act = h_glu * jax.nn.sigmoid(SWIGLU_ALPHA * h_glu) * (h_lin + 1.0)
        y = jnp.dot(act.astype(BF16), w2_s[...], preferred_element_type=F32) + b2_ref[0, 0]
        o_ref[...] = (y * g_ref[...]).astype(o_ref.dtype)

    @pl.when(i >= nv_ref[0])
    def _():
        o_ref[...] = jnp.zeros_like(o_ref)


def moe_experts(xg, gate, block_e, n_valid, layer, w1, b1, w2, b2):
    p_rows, d = xg.shape
    bm = MOE_ROWS
    n_blocks = p_rows // bm
    depth, ne, _, dh = w1.shape
    grid_spec = pltpu.PrefetchScalarGridSpec(
        num_scalar_prefetch=2, grid=(n_blocks,),
        in_specs=[pl.BlockSpec((bm, d), lambda i, be, nv: (i, 0)),
                  pl.BlockSpec((bm, 1), lambda i, be, nv: (i, 0)),
                  pl.BlockSpec((1, 1, d, dh), lambda i, be, nv: (layer, be[i], 0, 0)),
                  pl.BlockSpec((1, 1, 1, dh), lambda i, be, nv: (layer, be[i], 0, 0)),
                  pl.BlockSpec((1, 1, D_EXPERT, d), lambda i, be, nv: (layer, be[i], 0, 0)),
                  pl.BlockSpec((1, 1, 1, d), lambda i, be, nv: (layer, be[i], 0, 0))],
        out_specs=pl.BlockSpec((bm, d), lambda i, be, nv: (i, 0)),
        scratch_shapes=[pltpu.VMEM((d, dh), BF16), pltpu.VMEM((D_EXPERT, d), BF16)])
    return pl.pallas_call(
        _moe_kernel, grid_spec=grid_spec, out_shape=jax.ShapeDtypeStruct((p_rows, d), BF16),
        compiler_params=_cp(("arbitrary",)), name="moe_experts",
    )(block_e, n_valid, xg, gate.reshape(p_rows, 1), w1, b1.reshape(depth, ne, 1, dh), w2, b2.reshape(depth, ne, 1, d))


def moe_ffn(hn, idx, gates, rank, counts, layer, w1, b1, w2, b2):
    n, d = hn.shape
    bm = MOE_ROWS
    nk = n * TOP_K
    counts = counts.astype(jnp.int32)
    padded = (counts + bm - 1) // bm * bm
    pend = jnp.cumsum(padded)
    pstart = pend - padded
    experts = jnp.arange(N_EXPERTS, dtype=jnp.int32)
    pos = rank + jnp.sum(jnp.where(idx[..., None] == experts, pstart, 0), axis=-1)
    n_blocks = (nk + N_EXPERTS * (bm - 1) + bm - 1) // bm
    p_rows = n_blocks * bm
    real = (idx.reshape(nk) << KEY_EXPERT_SHIFT) + jnp.arange(nk, dtype=jnp.int32)
    fill = jnp.arange(bm - 1, dtype=jnp.int32)
    pad_keys = jnp.where(fill[None, :] < (padded - counts)[:, None],
                         (experts[:, None] << KEY_EXPERT_SHIFT) + (1 << KEY_PAD_BIT) + fill[None, :],
                         jnp.iinfo(jnp.int32).max)
    n_fill = p_rows - nk
    tail = jnp.full((n_fill - N_EXPERTS * (bm - 1),), jnp.iinfo(jnp.int32).max, jnp.int32)
    keys, buf_gate = lax.sort((jnp.concatenate([real, pad_keys.reshape(-1), tail]),
                               jnp.concatenate([gates.reshape(nk), jnp.zeros((n_fill,), F32)])), num_keys=1)
    is_pad = (keys >> KEY_PAD_BIT) & 1
    buf_tok = jnp.where(is_pad == 1, jnp.arange(p_rows, dtype=jnp.int32) % n, (keys & ((1 << KEY_PAD_BIT) - 1)) // TOP_K)
    block_start = jnp.arange(n_blocks, dtype=jnp.int32) * bm
    block_e = jnp.minimum(jnp.sum((pend[None, :] <= block_start[:, None]).astype(jnp.int32), axis=1), N_EXPERTS - 1)
    n_valid = (pend[-1:] // bm).astype(jnp.int32)
    y = moe_experts(hn[buf_tok], buf_gate, block_e, n_valid, layer, w1, b1, w2, b2)
    return y[pos.reshape(nk)]


def _rope_rot(w):
    h = MLA_ROPE // 4
    a, b, c, d = w[..., :h], w[..., h:2 * h], w[..., 2 * h:3 * h], w[..., 3 * h:]
    return jnp.concatenate([-b, a, -d, c], axis=-1)


def _prep_weights(w_in, mla_w_uq, mla_w_ukv):
    depth, d, _ = w_in.shape
    o = np.cumsum([0, 512, 512, GDN_QKV, 512, 8, 8, 512, 512, 256, 64, N_BRANCH * d])
    lu, lg, qkv, z, be, al, fu, cq, ckv, kr, mg = [w_in[..., o[i]:o[i + 1]] for i in range(11)]
    pad = jnp.zeros((depth, d, 128 - 16), w_in.dtype)
    w_main = jnp.concatenate([qkv, lu, lg, z, fu, cq, ckv, kr, _rope_rot(kr), be, al, pad], axis=-1).astype(BF16)
    w_gate = jnp.transpose(mg.reshape(depth, d, N_BRANCH, d), (0, 2, 1, 3)).astype(BF16)
    uq = mla_w_uq.reshape(depth, MLA_Q_RANK, MLA_HEADS, MLA_NOPE + MLA_ROPE)
    wq = jnp.concatenate([uq, _rope_rot(uq[..., MLA_NOPE:])], axis=-1).reshape(depth, MLA_Q_RANK, MLA_HEADS * 256)
    return w_main, w_gate, wq.astype(BF16), mla_w_ukv.astype(BF16)


def _rope_tables(ctx, n_lat):
    half = MLA_ROPE // 4
    inv = jnp.power(ROPE_BASE, -jnp.arange(half, dtype=F32) / half)
    t = jnp.arange(n_lat)
    row = (t // GRID_W).astype(F32)[:, None] * inv
    col = (t % GRID_W).astype(F32)[:, None] * inv
    cos = jnp.concatenate([jnp.cos(row), jnp.cos(row), jnp.cos(col), jnp.cos(col)], axis=1)
    sin = jnp.concatenate([jnp.sin(row), jnp.sin(row), jnp.sin(col), jnp.sin(col)], axis=1)
    cos = jnp.concatenate([jnp.ones((ctx, MLA_ROPE), F32), cos], axis=0)
    sin = jnp.concatenate([jnp.zeros((ctx, MLA_ROPE), F32), sin], axis=0)
    return cos, sin


def _chan_dft():
    idx = np.arange(FFT_GROUP)
    ang = 2.0 * np.pi * ((idx[:, None] * idx[None, :]) % FFT_GROUP) / FFT_GROUP
    eye = np.eye(FFT_GROUPS)
    return (jnp.asarray(np.kron(eye, np.cos(ang)), BF16), jnp.asarray(np.kron(eye, np.sin(ang)), BF16))


def kernel(x, c, ctx, c_ctx, w_ada, b_ada, norm1_g, norm2_g, w_in, lru_conv_w, lru_conv_b, lru_w_a, lru_b_a, lru_w_x, lru_b_x, lru_lambda, gdn_conv_w, gdn_conv_b, gdn_a_log, gdn_dt_bias, gdn_norm_g, mla_q_norm_g, mla_kv_norm_g, mla_w_uq, mla_w_ukv, w_branch, w_out, router_w, router_b, exp_w1, exp_b1, exp_w2, exp_b2, final_norm_g):
    nb, n_lat, d = x.shape
    n_ctx = ctx.shape[1]
    depth = w_ada.shape[0]
    s_len = n_ctx + n_lat
    m = nb * s_len
    tt = _pick(math.gcd(n_ctx, n_lat), (256, 128, 64))
    geo = dict(tt=tt, tiles_per_seq=s_len // tt, ctx_tiles=n_ctx // tt, batch=nb, s_len=s_len, ctx=n_ctx)

    mod_rows = 8 * ((nb + 1 + 7) // 8)
    cmat = jnp.zeros((mod_rows, d), F32).at[:nb].set(c).at[nb].set(c_ctx)
    mod_all = ada_table(cmat, w_ada, b_ada).reshape(depth, mod_rows * N_MOD, 1, d)

    cos_t, sin_t = _rope_tables(n_ctx, n_lat)
    chan_c, chan_s = _chan_dft()
    h = jnp.concatenate([ctx, x], axis=1).reshape(m, d)

    w_main_all, w_gate_all, wq_all, wkv_all = _prep_weights(w_in, mla_w_uq, mla_w_ukv)
    w_branch_all, w_out_all = w_branch.astype(BF16), w_out.astype(BF16)
    delta, prev_mod, gate_k = None, None, None
    for l in range(depth):
        w_main, w_gate, wq, wkv = w_main_all[l], w_gate_all[l], wq_all[l], wkv_all[l]
        p = dict(lru_conv_w=lru_conv_w[l], lru_conv_b=lru_conv_b[l], lru_w_a=lru_w_a[l].astype(BF16),
                 lru_b_a=lru_b_a[l], lru_w_x=lru_w_x[l].astype(BF16), lru_b_x=lru_b_x[l], lru_lambda=lru_lambda[l],
                 gdn_conv_w=gdn_conv_w[l], gdn_conv_b=gdn_conv_b[l], gdn_a_log=gdn_a_log[l],
                 gdn_dt_bias=gdn_dt_bias[l], gdn_norm_g=gdn_norm_g[l], mla_q_norm_g=mla_q_norm_g[l],
                 mla_kv_norm_g=mla_kv_norm_g[l], mla_wq=wq, mla_wkv=wkv, w_gate=w_gate,
                 w_branch=w_branch_all[l], w_out=w_out_all[l])
        mod = mod_all[l]
        h, hn = norm_mod(h, norm1_g[l], geo, mod=mod, shift_k=0, scale_k=1, delta=delta, gate_k=gate_k,
                         prev_mod=prev_mod)
        zin = matmul(hn, w_main)
        hf, hb = lru_scan(zin, p, geo)
        qkvn, bg = gdn_prep(zin, p, geo)
        of, ob = gdn_scan(qkvn, bg, geo)
        fu = zin[:, COL_FU:COL_FU + FFT_WIDTH].astype(BF16).reshape(nb, s_len, FFT_WIDTH)
        yc = jnp.concatenate([fourier(fu[:, :n_ctx], chan_c, chan_s), fourier(fu[:, n_ctx:], chan_c, chan_s)],
                             axis=1).reshape(m, FFT_WIDTH)
        q, k, v = mla_prep(zin, cos_t, sin_t, p, geo)
        yd_lat, yd_ctx = mla_attention(q, k, v, geo)
        yd = jnp.concatenate([yd_ctx, yd_lat], axis=1).reshape(m, MLA_HEADS * MLA_V)
        mixed = merge(hn, zin, hf, hb, of, ob, yc, yd, p)
        h, hn2, idx, gates, rank, counts = norm_mod(
            h, norm2_g[l], geo, mod=mod, shift_k=3, scale_k=4, delta=mixed, gate_k=2, prev_mod=mod,
            router=(router_w[l], router_b[l]))
        ffn = moe_ffn(hn2, idx, gates, rank, counts[0], l, exp_w1, exp_b1, exp_w2, exp_b2)
        delta, prev_mod, gate_k = ffn, mod, 5
    (out,) = norm_mod(h, final_norm_g, geo, delta=delta, gate_k=5, prev_mod=prev_mod, emit_h=False,
                      out_dtype=F32, lat_only=True)
    return out.reshape(nb, n_lat, d)
```

```python
import functools
import math

import jax
import jax.numpy as jnp
import numpy as np
from jax import lax
from jax.experimental import pallas as pl
from jax.experimental.pallas import tpu as pltpu

F32 = jnp.float32
BF16 = jnp.bfloat16

GRID_W = 64
NORM_EPS = 1e-6
N_MOD = 6
LRU_WIDTH = 512
LRU_BLOCKS = 4
LRU_BLOCK = 128
LRU_C = 8.0
GDN_HEADS = 4
GDN_DK = 128
GDN_DV = 128
GDN_CHUNK = 64
GDN_QKV = GDN_HEADS * (2 * GDN_DK + GDN_DV)
FFT_GROUPS = 4
FFT_GROUP = 128
FFT_WIDTH = 512
MLA_HEADS = 4
MLA_Q_RANK = 512
MLA_KV_RANK = 256
MLA_NOPE = 128
MLA_ROPE = 64
MLA_V = 128
MLA_QK_PAD = 256
ROPE_BASE = 10000.0
N_BRANCH = 4
BRANCH_WIDTH = 512
N_EXPERTS = 32
TOP_K = 4
D_EXPERT = 512
SWIGLU_LIMIT = 7.0
SWIGLU_ALPHA = 1.702
MOE_ROWS = 512
KEY_EXPERT_SHIFT = 20
KEY_PAD_BIT = 19

COL_QKV = 0
COL_LU = 1536
COL_LG = 2048
COL_Z = 2560
COL_FU = 3072
COL_CQ = 3584
COL_CKV = 4096
COL_MISC = 4352
COL_BA = 4480
ZIN_W = 4608

VMEM_LIMIT = 56 * 1024 * 1024
HALO = 8


def _cp(sem, vmem=VMEM_LIMIT):
    return pltpu.CompilerParams(dimension_semantics=sem, vmem_limit_bytes=vmem)


def _pick(n, cands):
    for c in cands:
        if n % c == 0:
            return c
    raise ValueError(f"no tile for {n} in {cands}")


def _mm(a, b):
    return jnp.dot(a.astype(BF16), b.astype(BF16), preferred_element_type=F32)


def _mm_nt(a, b):
    return lax.dot_general(a.astype(BF16), b.astype(BF16), (((1,), (1,)), ((), ())), preferred_element_type=F32)


def _mm_tn(a, b):
    return lax.dot_general(a.astype(BF16), b.astype(BF16), (((0,), (0,)), ((), ())), preferred_element_type=F32)


def _mm_f32(a, b):
    return jnp.dot(a, b, preferred_element_type=F32, precision=lax.Precision.HIGHEST)


def _softplus(y):
    return jnp.maximum(y, 0.0) + jnp.log1p(jnp.exp(-jnp.abs(y)))


def _ada_kernel(c_ref, w_ref, b_ref, o_ref):
    cv = c_ref[...]
    s = cv * jax.nn.sigmoid(cv)
    o_ref[0] = _mm(s, w_ref[0]) + b_ref[0]


def ada_table(cmat, w_ada, b_ada):
    depth, d, n = w_ada.shape
    rows = cmat.shape[0]
    tn = _pick(n, (1024, 512, 256, 128))
    return pl.pallas_call(
        _ada_kernel,
        grid=(depth, n // tn),
        in_specs=[pl.BlockSpec((rows, d), lambda l, j: (0, 0)),
                  pl.BlockSpec((1, d, tn), lambda l, j: (l, 0, j)),
                  pl.BlockSpec((1, 1, tn), lambda l, j: (l, 0, j))],
        out_specs=pl.BlockSpec((1, rows, tn), lambda l, j: (l, 0, j)),
        out_shape=jax.ShapeDtypeStruct((depth, rows, n), F32),
        compiler_params=_cp(("arbitrary", "arbitrary")),
        name="ada_table",
    )(cmat, w_ada, b_ada.reshape(depth, 1, n))


def _norm_kernel(*refs, has_delta, delta_is_rows, has_mod, has_router, emit_h):
    it = iter(refs)
    h_ref = next(it)
    if has_delta:
        d_ref = next(it)
        gate_ref = next(it)
    g_ref = next(it)
    if has_mod:
        shift_ref = next(it)
        scale_ref = next(it)
    if has_router:
        rw_ref = next(it)
        rb_ref = next(it)
    if emit_h:
        ho_ref = next(it)
    y_ref = next(it)
    if has_router:
        idx_ref = next(it)
        gt_ref = next(it)
        rank_ref = next(it)
        cnt_ref = next(it)

    h = h_ref[...]
    if has_delta and delta_is_rows:
        tt = h.shape[0]
        rows = lax.broadcasted_iota(jnp.int32, (tt, TOP_K * tt), 0)
        cols = lax.broadcasted_iota(jnp.int32, (tt, TOP_K * tt), 1)
        pick = (jnp.right_shift(cols, TOP_K.bit_length() - 1) == rows).astype(BF16)
        h = h + gate_ref[0] * jnp.dot(pick, d_ref[...], preferred_element_type=F32)
    elif has_delta:
        h = h + gate_ref[0] * d_ref[...].astype(F32)
    if emit_h:
        ho_ref[...] = h
    y = h * lax.rsqrt(jnp.mean(h * h, axis=-1, keepdims=True) + NORM_EPS) * g_ref[...]
    if has_mod:
        y = y * (1.0 + scale_ref[0]) + shift_ref[0]
    y_ref[...] = y.astype(y_ref.dtype)
    if has_router:
        @pl.when(pl.program_id(0) == 0)
        def _():
            cnt_ref[...] = jnp.zeros_like(cnt_ref)

        logit = _mm_f32(y, rw_ref[...]) + rb_ref[...]
        tt, ne = logit.shape
        lane = lax.broadcasted_iota(jnp.int32, (tt, ne), 1).astype(F32)
        vals, hots = [], []
        for kk in range(TOP_K):
            top = jnp.max(logit, axis=-1, keepdims=True)
            arg = jnp.min(jnp.where(logit == top, lane, float(ne)), axis=-1, keepdims=True)
            hot = lane == arg
            idx_ref[:, kk:kk + 1] = arg.astype(jnp.int32)
            vals.append(top)
            hots.append(hot)
            logit = jnp.where(hot, -jnp.inf, logit)
        exps = [jnp.exp(v - vals[0]) for v in vals]
        den = exps[0] + exps[1] + exps[2] + exps[3]
        for kk in range(TOP_K):
            gt_ref[:, kk:kk + 1] = exps[kk] / den
        hot_all = hots[0].astype(F32) + hots[1].astype(F32) + hots[2].astype(F32) + hots[3].astype(F32)
        ii = lax.broadcasted_iota(jnp.int32, (tt, tt), 0)
        jj = lax.broadcasted_iota(jnp.int32, (tt, tt), 1)
        before = _mm((jj < ii).astype(F32), hot_all) + cnt_ref[...]
        for kk in range(TOP_K):
            rank_ref[:, kk:kk + 1] = jnp.sum(jnp.where(hots[kk], before, 0.0), axis=-1, keepdims=True).astype(jnp.int32)
        cnt_ref[...] += jnp.sum(hot_all, axis=0, keepdims=True)


def norm_mod(h, g, geo, *, mod=None, shift_k=None, scale_k=None, delta=None, gate_k=None, prev_mod=None,
             router=None, emit_h=True, out_dtype=None, lat_only=False):
    m, d = h.shape
    out_dtype = BF16 if out_dtype is None else out_dtype
    tt, tps, ctx_t, nb = geo["tt"], geo["tiles_per_seq"], geo["ctx_tiles"], geo["batch"]
    if lat_only:
        lat_t = tps - ctx_t
        grid = (nb * lat_t,)
        in_row = lambda i: (i // lat_t) * tps + ctx_t + i % lat_t
        mod_row = lambda i: i // lat_t
        out_rows = nb * lat_t * tt
    else:
        grid = (m // tt,)
        in_row = lambda i: i
        mod_row = lambda i: jnp.where(i % tps < ctx_t, nb, i // tps)
        out_rows = m
    tok = lambda w: pl.BlockSpec((tt, w), lambda i: (in_row(i), 0))
    out_tok = lambda w: pl.BlockSpec((tt, w), lambda i: (i, 0))

    def mod_spec(k):
        return pl.BlockSpec((1, 1, d), lambda i: (mod_row(i) * N_MOD + k, 0, 0))

    args, specs = [h], [tok(d)]
    delta_is_rows = delta is not None and delta.shape[0] == TOP_K * m
    if delta is not None:
        args += [delta, prev_mod]
        specs += [pl.BlockSpec((TOP_K * tt, d), lambda i: (in_row(i), 0)) if delta_is_rows else tok(d),
                  mod_spec(gate_k)]
    args.append(g.reshape(1, d))
    specs.append(pl.BlockSpec((1, d), lambda i: (0, 0)))
    if mod is not None:
        args += [mod, mod]
        specs += [mod_spec(shift_k), mod_spec(scale_k)]
    if router is not None:
        rw, rb = router
        ne = rw.shape[1]
        args += [rw, rb.reshape(1, ne)]
        specs += [pl.BlockSpec(rw.shape, lambda i: (0, 0)), pl.BlockSpec((1, ne), lambda i: (0, 0))]
    out_shape, out_specs = [], []
    if emit_h:
        out_shape.append(jax.ShapeDtypeStruct((out_rows, d), F32))
        out_specs.append(out_tok(d))
    out_shape.append(jax.ShapeDtypeStruct((out_rows, d), out_dtype))
    out_specs.append(out_tok(d))
    if router is not None:
        out_shape += [jax.ShapeDtypeStruct((out_rows, TOP_K), jnp.int32), jax.ShapeDtypeStruct((out_rows, TOP_K), F32),
                      jax.ShapeDtypeStruct((out_rows, TOP_K), jnp.int32), jax.ShapeDtypeStruct((1, ne), F32)]
        out_specs += [out_tok(TOP_K), out_tok(TOP_K), out_tok(TOP_K), pl.BlockSpec((1, ne), lambda i: (0, 0))]
    kern = functools.partial(_norm_kernel, has_delta=delta is not None, delta_is_rows=delta_is_rows,
                             has_mod=mod is not None,
                             has_router=router is not None, emit_h=emit_h)
    return pl.pallas_call(kern, grid=grid, in_specs=specs, out_specs=out_specs, out_shape=out_shape,
                          compiler_params=_cp(("arbitrary",)), name="norm_mod")(*args)


def _matmul_kernel(x_ref, w_ref, o_ref):
    o_ref[...] = jnp.dot(x_ref[...], w_ref[...], preferred_element_type=F32).astype(o_ref.dtype)


def matmul(x, w, out_dtype=F32):
    m, k = x.shape
    n = w.shape[1]
    tm = _pick(m, (1024, 512, 256, 128))
    tn = _pick(n, (768, 512, 256, 128))
    return pl.pallas_call(
        _matmul_kernel,
        grid=(m // tm, n // tn),
        in_specs=[pl.BlockSpec((tm, k), lambda i, j: (i, 0)), pl.BlockSpec((k, tn), lambda i, j: (0, j))],
        out_specs=pl.BlockSpec((tm, tn), lambda i, j: (i, j)),
        out_shape=jax.ShapeDtypeStruct((m, n), out_dtype),
        compiler_params=_cp(("arbitrary", "arbitrary")),
        name="in_proj",
    )(x, w)


def _conv4(x, xp, xn, prev_ok, next_ok, cw_ref, cb_ref, ext_ref):
    tt = x.shape[0]
    ext_ref[0:HALO, :] = xp * prev_ok
    ext_ref[HALO:HALO + tt, :] = x
    ext_ref[HALO + tt:2 * HALO + tt, :] = xn * next_ok
    y = cb_ref[...] + cw_ref[2:3, :] * x
    y = y + cw_ref[0:1, :] * ext_ref[HALO - 2:HALO - 2 + tt, :]
    y = y + cw_ref[1:2, :] * ext_ref[HALO - 1:HALO - 1 + tt, :]
    y = y + cw_ref[3:4, :] * ext_ref[HALO + 1:HALO + 1 + tt, :]
    return y


def _seg_edges(tile, ctx_t, n_t):
    first = jnp.logical_or(tile == 0, tile == ctx_t)
    last = jnp.logical_or(tile == ctx_t - 1, tile == n_t - 1)
    return jnp.where(first, 0.0, 1.0).astype(F32), jnp.where(last, 0.0, 1.0).astype(F32)


def _rev_tile(j, ctx_t, n_t):
    return jnp.where(j < ctx_t, ctx_t - 1 - j, n_t - 1 - (j - ctx_t))


def _halo_specs(width, col_block, tile_of, tt, tps, m):
    r8 = tt // HALO
    last8 = m // HALO - 1

    def cur(b, j):
        return (b * tps + tile_of(j), col_block)

    def prev(b, j):
        return (jnp.maximum((b * tps + tile_of(j)) * r8 - 1, 0), col_block)

    def nxt(b, j):
        return (jnp.minimum((b * tps + tile_of(j) + 1) * r8, last8), col_block)

    return [pl.BlockSpec((tt, width), cur), pl.BlockSpec((HALO, width), prev), pl.BlockSpec((HALO, width), nxt)]


def _lru_kernel(xf_ref, xfp_ref, xfn_ref, xb_ref, xbp_ref, xbn_ref, cw_ref, cb_ref, wa_ref, ba_ref, wx_ref, bx_ref,
                lam_ref, hf_ref, hb_ref, ext_s, a_s, b_s, carry_s, *, tt, ctx_t, n_t):
    j = pl.program_id(1)

    @pl.when(j == 0)
    def _():
        carry_s[...] = jnp.zeros_like(carry_s)

    rows = lax.broadcasted_iota(jnp.int32, (HALO, LRU_WIDTH), 0)
    for d, (x_ref, xp_ref, xn_ref, out_ref) in enumerate(((xf_ref, xfp_ref, xfn_ref, hf_ref),
                                                          (xb_ref, xbp_ref, xbn_ref, hb_ref))):
        tile = j if d == 0 else _rev_tile(j, ctx_t, n_t)
        prev_ok, next_ok = _seg_edges(tile, ctx_t, n_t)
        xc = _conv4(x_ref[...], xp_ref[...], xn_ref[...], prev_ok, next_ok, cw_ref, cb_ref, ext_s)
        sp = _softplus(-lam_ref[d:d + 1, :])
        for g in range(LRU_BLOCKS):
            sl = slice(g * LRU_BLOCK, (g + 1) * LRU_BLOCK)
            xg = xc[:, sl]
            r = jax.nn.sigmoid(_mm(xg, wa_ref[d, g]) + ba_ref[d:d + 1, sl])
            ig = jax.nn.sigmoid(_mm(xg, wx_ref[d, g]) + bx_ref[d:d + 1, sl])
            log_a = -LRU_C * r * sp[:, sl]
            a = jnp.exp(log_a)
            a_s[d, :, sl] = a
            b_s[d, :, sl] = jnp.sqrt(-jnp.tanh(log_a) * (a * a + 1.0)) * (ig * xg)

        n_grp = tt // HALO

        def group(gi, carry, d=d, out_ref=out_ref):
            g0 = gi if d == 0 else n_grp - 1 - gi
            off = pl.multiple_of(g0 * HALO, HALO)
            av = a_s[d, pl.ds(off, HALO), :]
            bv = b_s[d, pl.ds(off, HALO), :]
            for s in (1, 2, 4):
                shift = s if d == 0 else HALO - s
                a_sh = pltpu.roll(av, shift, 0)
                b_sh = pltpu.roll(bv, shift, 0)
                ok = (rows >= s) if d == 0 else (rows < HALO - s)
                bv = jnp.where(ok, av * b_sh + bv, bv)
                av = jnp.where(ok, av * a_sh, av)
            hv = bv + av * carry
            out_ref[pl.ds(off, HALO), :] = hv
            return hv[HALO - 1:HALO, :] if d == 0 else hv[0:1, :]

        carry_s[d:d + 1, :] = lax.fori_loop(0, n_grp, group, carry_s[d:d + 1, :])


def lru_scan(zin, p, geo):
    m = zin.shape[0]
    tt, tps, ctx_t, nb = geo["tt"], geo["tiles_per_seq"], geo["ctx_tiles"], geo["batch"]
    cb = COL_LU // LRU_WIDTH
    fwd = _halo_specs(LRU_WIDTH, cb, lambda j: j, tt, tps, m)
    bwd = _halo_specs(LRU_WIDTH, cb, lambda j: _rev_tile(j, ctx_t, tps), tt, tps, m)
    full = lambda shape: pl.BlockSpec(shape, lambda b, j: (0,) * len(shape))
    w_specs = [full((4, LRU_WIDTH)), full((1, LRU_WIDTH)), full((2, LRU_BLOCKS, LRU_BLOCK, LRU_BLOCK)),
               full((2, LRU_WIDTH)), full((2, LRU_BLOCKS, LRU_BLOCK, LRU_BLOCK)), full((2, LRU_WIDTH)),
               full((2, LRU_WIDTH))]
    out_f = pl.BlockSpec((tt, LRU_WIDTH), lambda b, j: (b * tps + j, 0))
    out_b = pl.BlockSpec((tt, LRU_WIDTH), lambda b, j: (b * tps + _rev_tile(j, ctx_t, tps), 0))
    kern = functools.partial(_lru_kernel, tt=tt, ctx_t=ctx_t, n_t=tps)
    return pl.pallas_call(
        kern, grid=(nb, tps), in_specs=fwd + bwd + w_specs, out_specs=[out_f, out_b],
        out_shape=[jax.ShapeDtypeStruct((m, LRU_WIDTH), F32)] * 2,
        scratch_shapes=[pltpu.VMEM((tt + 2 * HALO, LRU_WIDTH), F32), pltpu.VMEM((2, tt, LRU_WIDTH), F32),
                        pltpu.VMEM((2, tt, LRU_WIDTH), F32), pltpu.VMEM((2, LRU_WIDTH), F32)],
        compiler_params=_cp(("arbitrary", "arbitrary")), name="lru_scan",
    )(zin, zin, zin, zin, zin, zin, p["lru_conv_w"], p["lru_conv_b"].reshape(1, -1), p["lru_w_a"], p["lru_b_a"],
      p["lru_w_x"], p["lru_b_x"], p["lru_lambda"])


def _gdn_prep_kernel(x_ref, xp_ref, xn_ref, ba_ref, cw_ref, cb_ref, nega_ref, dtb_ref, qkv_ref, bg_ref, ext_s,
                     *, ctx_t, n_t):
    tile = pl.program_id(1)
    prev_ok, next_ok = _seg_edges(tile, ctx_t, n_t)
    xc = _conv4(x_ref[...], xp_ref[...], xn_ref[...], prev_ok, next_ok, cw_ref, cb_ref, ext_s)
    xc = xc * jax.nn.sigmoid(xc)
    nq = GDN_HEADS * GDN_DK
    for hd in range(GDN_HEADS):
        for base, scale in ((0, GDN_DK ** -0.5), (nq, 1.0)):
            sl = slice(base + hd * GDN_DK, base + (hd + 1) * GDN_DK)
            v = xc[:, sl]
            qkv_ref[:, sl] = v * (lax.rsqrt(jnp.sum(v * v, axis=-1, keepdims=True) + NORM_EPS) * scale)
    qkv_ref[:, 2 * nq:] = xc[:, 2 * nq:]
    ba = ba_ref[...]
    cols = lax.broadcasted_iota(jnp.int32, ba.shape, 1)
    gval = nega_ref[...] * _softplus(ba + dtb_ref[...])
    bg_ref[...] = jnp.where(cols < 2 * GDN_HEADS, jax.nn.sigmoid(ba), gval)


def gdn_prep(zin, p, geo):
    m = zin.shape[0]
    tt, tps, ctx_t, nb = geo["tt"], geo["tiles_per_seq"], geo["ctx_tiles"], geo["batch"]
    specs = _halo_specs(GDN_QKV, COL_QKV // GDN_QKV, lambda j: j, tt, tps, m)
    specs.append(pl.BlockSpec((tt, 128), lambda b, j: (b * tps + j, COL_BA // 128)))
    full = lambda shape: pl.BlockSpec(shape, lambda b, j: (0,) * len(shape))
    specs += [full((4, GDN_QKV)), full((1, GDN_QKV)), full((1, 128)), full((1, 128))]
    pad = jnp.zeros((1, 128), F32)
    nega = pad.at[0, 8:16].set(-jnp.exp(p["gdn_a_log"].reshape(-1)))
    dtb = pad.at[0, 8:16].set(p["gdn_dt_bias"].reshape(-1))
    kern = functools.partial(_gdn_prep_kernel, ctx_t=ctx_t, n_t=tps)
    return pl.pallas_call(
        kern, grid=(nb, tps), in_specs=specs,
        out_specs=[pl.BlockSpec((tt, GDN_QKV), lambda b, j: (b * tps + j, 0)),
                   pl.BlockSpec((tt, 128), lambda b, j: (b * tps + j, 0))],
        out_shape=[jax.ShapeDtypeStruct((m, GDN_QKV), F32), jax.ShapeDtypeStruct((m, 128), F32)],
        scratch_shapes=[pltpu.VMEM((tt + 2 * HALO, GDN_QKV), F32)],
        compiler_params=_cp(("arbitrary", "arbitrary")), name="gdn_prep",
    )(zin, zin, zin, zin, p["gdn_conv_w"], p["gdn_conv_b"].reshape(1, -1), nega, dtb)


def _gdn_kernel(xf_ref, bgf_ref, xb_ref, bgb_ref, of_ref, ob_ref, s_s, *, gb):
    j = pl.program_id(1)

    @pl.when(j == 0)
    def _():
        s_s[...] = jnp.zeros_like(s_s)

    c = GDN_CHUNK
    nq = GDN_HEADS * GDN_DK
    ii = lax.broadcasted_iota(jnp.int32, (c, c), 0)
    jj = lax.broadcasted_iota(jnp.int32, (c, c), 1)
    incl = ((ii >= jj), (ii <= jj))
    strict = ((ii > jj), (ii < jj))
    blk = jnp.right_shift(ii, 4) == jnp.right_shift(jj, 4)
    eye = (ii == jj).astype(F32)

    chains = []
    for g in range(gb):
        for d, (x_ref, bg_ref) in enumerate(((xf_ref, bgf_ref), (xb_ref, bgb_ref))):
            bg = bg_ref[g]
            gam_all = _mm_f32(incl[d].astype(F32), bg)
            gam_t = gam_all.T
            for hd in range(GDN_HEADS):
                cb = d * GDN_HEADS + hd
                cg = 2 * GDN_HEADS + cb
                chains.append(dict(
                    g=g, d=d, hd=hd,
                    q=x_ref[g, :, hd * GDN_DK:(hd + 1) * GDN_DK],
                    k=x_ref[g, :, nq + hd * GDN_DK:nq + (hd + 1) * GDN_DK],
                    v=x_ref[g, :, 2 * nq + hd * GDN_DV:2 * nq + (hd + 1) * GDN_DV],
                    beta=bg[:, cb:cb + 1], gam=gam_all[:, cg:cg + 1], gam_row=gam_t[cg:cg + 1, :]))

    def put(name, fn):
        vals = [fn(ch) for ch in chains]
        for ch, val in zip(chains, vals):
            ch[name] = val

    put("dec_incl", lambda ch: jnp.where(
        incl[ch["d"]], jnp.exp(jnp.where(incl[ch["d"]], ch["gam"] - ch["gam_row"], 0.0)), 0.0))
    put("dec_strict", lambda ch: jnp.where(strict[ch["d"]], ch["dec_incl"], 0.0))
    put("g_last", lambda ch: ch["gam"][0:1, :] if ch["d"] == 1 else ch["gam"][c - 1:c, :])
    put("eg", lambda ch: jnp.exp(ch["gam"]))
    put("kb", lambda ch: ch["k"] * ch["beta"])
    put("a", lambda ch: _mm_nt(ch["kb"], ch["k"]) * ch["dec_strict"])
    put("rhs", lambda ch: jnp.concatenate([ch["v"] * ch["beta"], ch["kb"] * ch["eg"]], axis=1))
    put("a_d", lambda ch: jnp.where(blk, ch["a"], 0.0))
    put("a_o", lambda ch: ch["a"] - ch["a_d"])
    put("a2", lambda ch: _mm(ch["a_d"], ch["a_d"]))
    put("a4", lambda ch: _mm(ch["a2"], ch["a2"]))
    put("a8", lambda ch: _mm(ch["a4"], ch["a4"]))
    put("dinv", lambda ch: eye - ch["a_d"])
    put("dinv", lambda ch: ch["dinv"] + _mm(ch["dinv"], ch["a2"]))
    put("dinv", lambda ch: ch["dinv"] + _mm(ch["dinv"], ch["a4"]))
    put("dinv", lambda ch: ch["dinv"] + _mm(ch["dinv"], ch["a8"]))
    put("n1", lambda ch: _mm(ch["dinv"], ch["a_o"]))
    put("n2", lambda ch: _mm(ch["n1"], ch["n1"]))
    put("x", lambda ch: _mm(ch["dinv"], ch["rhs"]))
    put("x", lambda ch: ch["x"] + _mm(ch["n2"], ch["x"]))
    put("x", lambda ch: ch["x"] - _mm(ch["n1"], ch["x"]))
    put("qk", lambda ch: _mm_nt(ch["q"], ch["k"]) * ch["dec_incl"])
    put("s", lambda ch: s_s[ch["g"], ch["d"], ch["hd"]])
    put("v_new", lambda ch: ch["x"][:, :GDN_DV] - _mm(ch["x"][:, GDN_DV:], ch["s"]))
    put("o", lambda ch: _mm(ch["q"] * ch["eg"], ch["s"]) + _mm(ch["qk"], ch["v_new"]))
    put("s_new", lambda ch: ch["s"] * jnp.exp(ch["g_last"])
        + _mm_tn(ch["k"] * jnp.exp(ch["g_last"] - ch["gam"]), ch["v_new"]))
    for ch in chains:
        o_ref = of_ref if ch["d"] == 0 else ob_ref
        o_ref[ch["g"], :, ch["hd"] * GDN_DV:(ch["hd"] + 1) * GDN_DV] = ch["o"]
        s_s[ch["g"], ch["d"], ch["hd"]] = ch["s_new"]


def gdn_scan(qkvn, bg, geo):
    m = qkvn.shape[0]
    c = GDN_CHUNK
    nb, s_len, ctx = geo["batch"], geo["s_len"], geo["ctx"]
    gb = 2 if nb % 2 == 0 else 1
    n_c, ctx_c = s_len // c, ctx // c
    fwd = lambda b, j: (b, j, 0)
    bwd = lambda b, j: (b, _rev_tile(j, ctx_c, n_c), 0)
    w = GDN_HEADS * GDN_DV
    kern = functools.partial(_gdn_kernel, gb=gb)
    qkv3 = qkvn.reshape(nb, s_len, GDN_QKV)
    bg3 = bg.reshape(nb, s_len, 128)
    of, ob = pl.pallas_call(
        kern, grid=(nb // gb, n_c),
        in_specs=[pl.BlockSpec((gb, c, GDN_QKV), fwd), pl.BlockSpec((gb, c, 128), fwd),
                  pl.BlockSpec((gb, c, GDN_QKV), bwd), pl.BlockSpec((gb, c, 128), bwd)],
        out_specs=[pl.BlockSpec((gb, c, w), fwd), pl.BlockSpec((gb, c, w), bwd)],
        out_shape=[jax.ShapeDtypeStruct((nb, s_len, w), F32)] * 2,
        scratch_shapes=[pltpu.VMEM((gb, 2, GDN_HEADS, GDN_DK, GDN_DV), F32)],
        compiler_params=_cp(("arbitrary", "arbitrary")), name="gdn_scan",
    )(qkv3, bg3, qkv3, bg3)
    return of.reshape(m, w), ob.reshape(m, w)


def _fourier_kernel(c_ref, s_ref, u_ref, cc_ref, sc_ref, o_ref, *, scale):
    u = u_ref[0]
    a = jnp.dot(c_ref[...], u, preferred_element_type=F32)
    b = jnp.dot(s_ref[...], u, preferred_element_type=F32)
    o = _mm(a, cc_ref[...]) - _mm(b, sc_ref[...])
    o_ref[0] = o * scale


def _dft_mats(t):
    lo = 64
    f = jnp.arange(t, dtype=jnp.int32)[:, None]
    ang_hi = ((f * (jnp.arange(t // lo, dtype=jnp.int32) * lo)[None, :]) % t).astype(F32) * (2.0 * math.pi / t)
    ang_lo = ((f * jnp.arange(lo, dtype=jnp.int32)[None, :]) % t).astype(F32) * (2.0 * math.pi / t)
    ch, sh = jnp.cos(ang_hi)[:, :, None], jnp.sin(ang_hi)[:, :, None]
    cl, sl = jnp.cos(ang_lo)[:, None, :], jnp.sin(ang_lo)[:, None, :]
    return ((ch * cl - sh * sl).reshape(t, t).astype(BF16), (sh * cl + ch * sl).reshape(t, t).astype(BF16))


def fourier(u, chan_c, chan_s):
    nb, t, w = u.shape
    cm, sm = _dft_mats(t)
    tf = _pick(t, (512, 256, 128))
    kern = functools.partial(_fourier_kernel, scale=1.0 / math.sqrt(t * FFT_GROUP))
    return pl.pallas_call(
        kern, grid=(t // tf, nb),
        in_specs=[pl.BlockSpec((tf, t), lambda f, b: (f, 0)), pl.BlockSpec((tf, t), lambda f, b: (f, 0)),
                  pl.BlockSpec((1, t, w), lambda f, b: (b, 0, 0)),
                  pl.BlockSpec((w, w), lambda f, b: (0, 0)), pl.BlockSpec((w, w), lambda f, b: (0, 0))],
        out_specs=pl.BlockSpec((1, tf, w), lambda f, b: (b, f, 0)),
        out_shape=jax.ShapeDtypeStruct((nb, t, w), F32),
        compiler_params=_cp(("arbitrary", "arbitrary")), name="fourier",
    )(cm, sm, u, chan_c, chan_s)


def _mla_prep_kernel(cq_ref, ckv_ref, misc_ref, cos_ref, sin_ref, gq_ref, gkv_ref, wq_ref, wkv_ref,
                     q_ref, k_ref, v_ref, *, q_scale):
    def rms(x, g):
        return x * lax.rsqrt(jnp.mean(x * x, axis=-1, keepdims=True) + NORM_EPS) * g

    cos = cos_ref[...]
    sin = sin_ref[...]
    qa = _mm(rms(cq_ref[...], gq_ref[...]), wq_ref[...])
    kva = _mm(rms(ckv_ref[...], gkv_ref[...]), wkv_ref[...])
    misc = misc_ref[...]
    k_rope = misc[:, :MLA_ROPE] * cos + misc[:, MLA_ROPE:] * sin
    zpad = jnp.zeros((cos.shape[0], MLA_QK_PAD - MLA_NOPE - MLA_ROPE), F32)
    for hd in range(MLA_HEADS):
        qh = qa[:, hd * 256:(hd + 1) * 256]
        q_rope = qh[:, MLA_NOPE:MLA_NOPE + MLA_ROPE] * cos + qh[:, MLA_NOPE + MLA_ROPE:] * sin
        q_ref[0, hd] = (jnp.concatenate([qh[:, :MLA_NOPE], q_rope, zpad], axis=1) * q_scale).astype(BF16)
        kh = kva[:, hd * 256:hd * 256 + MLA_NOPE]
        k_ref[0, hd] = jnp.concatenate([kh, k_rope, zpad], axis=1).astype(BF16)
        v_ref[0, hd] = kva[:, hd * 256 + MLA_NOPE:(hd + 1) * 256].astype(BF16)


def mla_prep(zin, cos_t, sin_t, p, geo):
    tt, tps, nb, s_len = geo["tt"], geo["tiles_per_seq"], geo["batch"], geo["s_len"]
    row = lambda b, j: b * tps + j
    full = lambda shape: pl.BlockSpec(shape, lambda b, j: (0,) * len(shape))
    ctx_t = geo["ctx_tiles"]
    hs = lambda w: pl.BlockSpec((1, MLA_HEADS, tt, w),
                                lambda b, j: (b, 0, jnp.where(j < ctx_t, j + (tps - ctx_t), j - ctx_t), 0))
    kern = functools.partial(_mla_prep_kernel, q_scale=(MLA_NOPE + MLA_ROPE) ** -0.5)
    return pl.pallas_call(
        kern, grid=(nb, tps),
        in_specs=[pl.BlockSpec((tt, MLA_Q_RANK), lambda b, j: (row(b, j), COL_CQ // MLA_Q_RANK)),
                  pl.BlockSpec((tt, MLA_KV_RANK), lambda b, j: (row(b, j), COL_CKV // MLA_KV_RANK)),
                  pl.BlockSpec((tt, 128), lambda b, j: (row(b, j), COL_MISC // 128)),
                  pl.BlockSpec((tt, MLA_ROPE), lambda b, j: (j, 0)), pl.BlockSpec((tt, MLA_ROPE), lambda b, j: (j, 0)),
                  full((1, MLA_Q_RANK)), full((1, MLA_KV_RANK)),
                  full((MLA_Q_RANK, MLA_HEADS * 256)), full((MLA_KV_RANK, MLA_HEADS * 256))],
        out_specs=[hs(MLA_QK_PAD), hs(MLA_QK_PAD), hs(MLA_V)],
        out_shape=[jax.ShapeDtypeStruct((nb, MLA_HEADS, s_len, MLA_QK_PAD), BF16),
                   jax.ShapeDtypeStruct((nb, MLA_HEADS, s_len, MLA_QK_PAD), BF16),
                   jax.ShapeDtypeStruct((nb, MLA_HEADS, s_len, MLA_V), BF16)],
        compiler_params=_cp(("arbitrary", "arbitrary")), name="mla_prep",
    )(zin, zin, zin, cos_t, sin_t, p["mla_q_norm_g"].reshape(1, -1), p["mla_kv_norm_g"].reshape(1, -1),
      p["mla_wq"], p["mla_wkv"])


def _attn_kernel(q_ref, k_ref, v_ref, o_ref):
    s = lax.dot_general(q_ref[0, 0], k_ref[0, 0], (((1,), (1,)), ((), ())), preferred_element_type=F32)
    pr = jnp.exp(s - jnp.max(s, axis=-1, keepdims=True))
    den = jnp.sum(pr, axis=-1, keepdims=True)
    o_ref[0] = jnp.dot(pr.astype(BF16), v_ref[0, 0], preferred_element_type=F32) / den


def mla_attention(q, k, v, geo):
    nb, nh, s_len, _ = q.shape
    n_ctx = geo["ctx"]
    n_lat = s_len - n_ctx
    assert n_lat % n_ctx == 0

    def call(tq, q_block0, n_q, key_rows, key_block):
        return pl.pallas_call(
            _attn_kernel, grid=(nb, nh, n_q // tq),
            in_specs=[pl.BlockSpec((1, 1, tq, MLA_QK_PAD), lambda b, h, i: (b, h, q_block0 + i, 0)),
                      pl.BlockSpec((1, 1, key_rows, MLA_QK_PAD), lambda b, h, i: (b, h, key_block, 0)),
                      pl.BlockSpec((1, 1, key_rows, MLA_V), lambda b, h, i: (b, h, key_block, 0))],
            out_specs=pl.BlockSpec((1, tq, MLA_V), lambda b, h, i: (b, i, h)),
            out_shape=jax.ShapeDtypeStruct((nb, n_q, nh * MLA_V), F32),
            compiler_params=_cp(("arbitrary", "arbitrary", "arbitrary")), name="mla_attention",
        )(q, k, v)

    tq_lat = _pick(n_lat, (256, 128))
    tq_ctx = _pick(n_ctx, (256, 128, 64))
    o_lat = call(tq_lat, 0, n_lat, s_len, 0)
    o_ctx = call(tq_ctx, n_lat // tq_ctx, n_ctx, n_ctx, n_lat // n_ctx)
    return o_lat, o_ctx


def _merge_kernel(hn_ref, hf_ref, hb_ref, lg_ref, of_ref, ob_ref, z_ref, gg_ref, yc_ref, yd_ref, wg_ref, wb_ref,
                  wo_ref, o_ref, y_s, acc_s, *, n_j):
    j = pl.program_id(1)

    @pl.when(j == 0)
    def _():
        lg = lg_ref[...]
        gelu = 0.5 * lg * (1.0 + jnp.tanh(math.sqrt(2.0 / math.pi) * (lg + 0.044715 * (lg * lg * lg))))
        y_s[0] = (gelu * (hf_ref[...] + hb_ref[...])).astype(BF16)
        for hd in range(GDN_HEADS):
            sl = slice(hd * GDN_DV, (hd + 1) * GDN_DV)
            o = of_ref[:, sl] + ob_ref[:, sl]
            o = o * lax.rsqrt(jnp.mean(o * o, axis=-1, keepdims=True) + NORM_EPS) * gg_ref[...]
            z = z_ref[:, sl]
            y_s[1, :, sl] = (o * (z * jax.nn.sigmoid(z))).astype(BF16)
        y_s[2] = yc_ref[...].astype(BF16)
        y_s[3] = yd_ref[...].astype(BF16)
        acc_s[...] = jnp.zeros_like(acc_s)

    hn = hn_ref[...]
    merged = None
    for i in range(N_BRANCH):
        gate = jax.nn.sigmoid(jnp.dot(hn, wg_ref[i], preferred_element_type=F32))
        term = gate * jnp.dot(y_s[i], wb_ref[i], preferred_element_type=F32)
        merged = term if merged is None else merged + term
    acc_s[...] += jnp.dot(merged.astype(BF16), wo_ref[...], preferred_element_type=F32)

    @pl.when(j == n_j - 1)
    def _():
        o_ref[...] = acc_s[...].astype(o_ref.dtype)


def merge(hn, zin, hf, hb, of, ob, yc, yd, p):
    m, d = hn.shape
    tm = _pick(m, (512, 256, 128))
    tn = 256
    n_j = d // tn
    w = BRANCH_WIDTH
    tok = lambda a_w, cb: pl.BlockSpec((tm, a_w), lambda i, j: (i, cb))
    kern = functools.partial(_merge_kernel, n_j=n_j)
    return pl.pallas_call(
        kern, grid=(m // tm, n_j),
        in_specs=[tok(d, 0), tok(w, 0), tok(w, 0), tok(w, COL_LG // w), tok(w, 0), tok(w, 0), tok(w, COL_Z // w),
                  pl.BlockSpec((1, GDN_DV), lambda i, j: (0, 0)), tok(w, 0), tok(w, 0),
                  pl.BlockSpec((N_BRANCH, d, tn), lambda i, j: (0, 0, j)),
                  pl.BlockSpec((N_BRANCH, w, tn), lambda i, j: (0, 0, j)),
                  pl.BlockSpec((tn, d), lambda i, j: (j, 0))],
        out_specs=pl.BlockSpec((tm, d), lambda i, j: (i, 0)),
        out_shape=jax.ShapeDtypeStruct((m, d), BF16),
        scratch_shapes=[pltpu.VMEM((N_BRANCH, tm, w), BF16), pltpu.VMEM((tm, d), F32)],
        compiler_params=_cp(("arbitrary", "arbitrary")), name="merge",
    )(hn, hf, hb, zin, of, ob, zin, p["gdn_norm_g"].reshape(1, -1), yc, yd, p["w_gate"], p["w_branch"], p["w_out"])


def _moe_kernel(be_ref, nv_ref, x_ref, g_ref, w1_ref, b1_ref, w2_ref, b2_ref, o_ref, w1_s, w2_s):
    i = pl.program_id(0)
    new_expert = jnp.logical_or(i == 0, be_ref[i] != be_ref[jnp.maximum(i - 1, 0)])

    @pl.when(new_expert)
    def _():
        w1_s[...] = w1_ref[0, 0].astype(BF16)
        w2_s[...] = w2_ref[0, 0].astype(BF16)

    @pl.when(i < nv_ref[0])
    def _():
        h = jnp.dot(x_ref[...], w1_s[...], preferred_element_type=F32) + b1_ref[0, 0]
        h_glu = jnp.minimum(h[:, :D_EXPERT], SWIGLU_LIMIT)
        h_lin = jnp.clip(h[:, D_EXPERT:], -SWIGLU_LIMIT, SWIGLU_LIMIT)
        act = h_glu * jax.nn.sigmoid(SWIGLU_ALPHA * h_glu) * (h_lin + 1.0)
        y = jnp.dot(act.astype(BF16), w2_s[...], preferred_element_type=F32) + b2_ref[0, 0]
        o_ref[...] = (y * g_ref[...]).astype(o_ref.dtype)

    @pl.when(i >= nv_ref[0])
    def _():
        o_ref[...] = jnp.zeros_like(o_ref)


def moe_experts(xg, gate, block_e, n_valid, layer, w1, b1, w2, b2):
    p_rows, d = xg.shape
    bm = MOE_ROWS
    n_blocks = p_rows // bm
    depth, ne, _, dh = w1.shape
    grid_spec = pltpu.PrefetchScalarGridSpec(
        num_scalar_prefetch=2, grid=(n_blocks,),
        in_specs=[pl.BlockSpec((bm, d), lambda i, be, nv: (i, 0)),
                  pl.BlockSpec((bm, 1), lambda i, be, nv: (i, 0)),
                  pl.BlockSpec((1, 1, d, dh), lambda i, be, nv: (layer, be[i], 0, 0)),
                  pl.BlockSpec((1, 1, 1, dh), lambda i, be, nv: (layer, be[i], 0, 0)),
                  pl.BlockSpec((1, 1, D_EXPERT, d), lambda i, be, nv: (layer, be[i], 0, 0)),
                  pl.BlockSpec((1, 1, 1, d), lambda i, be, nv: (layer, be[i], 0, 0))],
        out_specs=pl.BlockSpec((bm, d), lambda i, be, nv: (i, 0)),
        scratch_shapes=[pltpu.VMEM((d, dh), BF16), pltpu.VMEM((D_EXPERT, d), BF16)])
    return pl.pallas_call(
        _moe_kernel, grid_spec=grid_spec, out_shape=jax.ShapeDtypeStruct((p_rows, d), BF16),
        compiler_params=_cp(("arbitrary",)), name="moe_experts",
    )(block_e, n_valid, xg, gate.reshape(p_rows, 1), w1, b1.reshape(depth, ne, 1, dh), w2, b2.reshape(depth, ne, 1, d))


def moe_ffn(hn, idx, gates, rank, counts, layer, w1, b1, w2, b2):
    n, d = hn.shape
    bm = MOE_ROWS
    nk = n * TOP_K
    counts = counts.astype(jnp.int32)
    padded = (counts + bm - 1) // bm * bm
    pend = jnp.cumsum(padded)
    pstart = pend - padded
    experts = jnp.arange(N_EXPERTS, dtype=jnp.int32)
    pos = rank + jnp.sum(jnp.where(idx[..., None] == experts, pstart, 0), axis=-1)
    n_blocks = (nk + N_EXPERTS * (bm - 1) + bm - 1) // bm
    p_rows = n_blocks * bm
    real = (idx.reshape(nk) << KEY_EXPERT_SHIFT) + jnp.arange(nk, dtype=jnp.int32)
    fill = jnp.arange(bm - 1, dtype=jnp.int32)
    pad_keys = jnp.where(fill[None, :] < (padded - counts)[:, None],
                         (experts[:, None] << KEY_EXPERT_SHIFT) + (1 << KEY_PAD_BIT) + fill[None, :],
                         jnp.iinfo(jnp.int32).max)
    n_fill = p_rows - nk
    tail = jnp.full((n_fill - N_EXPERTS * (bm - 1),), jnp.iinfo(jnp.int32).max, jnp.int32)
    keys, buf_gate = lax.sort((jnp.concatenate([real, pad_keys.reshape(-1), tail]),
                               jnp.concatenate([gates.reshape(nk), jnp.zeros((n_fill,), F32)])), num_keys=1)
    is_pad = (keys >> KEY_PAD_BIT) & 1
    buf_tok = jnp.where(is_pad == 1, jnp.arange(p_rows, dtype=jnp.int32) % n, (keys & ((1 << KEY_PAD_BIT) - 1)) // TOP_K)
    block_start = jnp.arange(n_blocks, dtype=jnp.int32) * bm
    block_e = jnp.minimum(jnp.sum((pend[None, :] <= block_start[:, None]).astype(jnp.int32), axis=1), N_EXPERTS - 1)
    n_valid = (pend[-1:] // bm).astype(jnp.int32)
    y = moe_experts(hn[buf_tok], buf_gate, block_e, n_valid, layer, w1, b1, w2, b2)
    return y[pos.reshape(nk)]


def _rope_rot(w):
    h = MLA_ROPE // 4
    a, b, c, d = w[..., :h], w[..., h:2 * h], w[..., 2 * h:3 * h], w[..., 3 * h:]
    return jnp.concatenate([-b, a, -d, c], axis=-1)


def _prep_weights(w_in, mla_w_uq, mla_w_ukv):
    depth, d, _ = w_in.shape
    o = np.cumsum([0, 512, 512, GDN_QKV, 512, 8, 8, 512, 512, 256, 64, N_BRANCH * d])
    lu, lg, qkv, z, be, al, fu, cq, ckv, kr, mg = [w_in[..., o[i]:o[i + 1]] for i in range(11)]
    pad = jnp.zeros((depth, d, 128 - 16), w_in.dtype)
    w_main = jnp.concatenate([qkv, lu, lg, z, fu, cq, ckv, kr, _rope_rot(kr), be, al, pad], axis=-1).astype(BF16)
    w_gate = jnp.transpose(mg.reshape(depth, d, N_BRANCH, d), (0, 2, 1, 3)).astype(BF16)
    uq = mla_w_uq.reshape(depth, MLA_Q_RANK, MLA_HEADS, MLA_NOPE + MLA_ROPE)
    wq = jnp.concatenate([uq, _rope_rot(uq[..., MLA_NOPE:])], axis=-1).reshape(depth, MLA_Q_RANK, MLA_HEADS * 256)
    return w_main, w_gate, wq.astype(BF16), mla_w_ukv.astype(BF16)


def _rope_tables(ctx, n_lat):
    half = MLA_ROPE // 4
    inv = jnp.power(ROPE_BASE, -jnp.arange(half, dtype=F32) / half)
    t = jnp.arange(n_lat)
    row = (t // GRID_W).astype(F32)[:, None] * inv
    col = (t % GRID_W).astype(F32)[:, None] * inv
    cos = jnp.concatenate([jnp.cos(row), jnp.cos(row), jnp.cos(col), jnp.cos(col)], axis=1)
    sin = jnp.concatenate([jnp.sin(row), jnp.sin(row), jnp.sin(col), jnp.sin(col)], axis=1)
    cos = jnp.concatenate([jnp.ones((ctx, MLA_ROPE), F32), cos], axis=0)
    sin = jnp.concatenate([jnp.zeros((ctx, MLA_ROPE), F32), sin], axis=0)
    return cos, sin


def _chan_dft():
    idx = np.arange(FFT_GROUP)
    ang = 2.0 * np.pi * ((idx[:, None] * idx[None, :]) % FFT_GROUP) / FFT_GROUP
    eye = np.eye(FFT_GROUPS)
    return (jnp.asarray(np.kron(eye, np.cos(ang)), BF16), jnp.asarray(np.kron(eye, np.sin(ang)), BF16))


def kernel(x, c, ctx, c_ctx, w_ada, b_ada, norm1_g, norm2_g, w_in, lru_conv_w, lru_conv_b, lru_w_a, lru_b_a, lru_w_x, lru_b_x, lru_lambda, gdn_conv_w, gdn_conv_b, gdn_a_log, gdn_dt_bias, gdn_norm_g, mla_q_norm_g, mla_kv_norm_g, mla_w_uq, mla_w_ukv, w_branch, w_out, router_w, router_b, exp_w1, exp_b1, exp_w2, exp_b2, final_norm_g):
    nb, n_lat, d = x.shape
    n_ctx = ctx.shape[1]
    depth = w_ada.shape[0]
    s_len = n_ctx + n_lat
    m = nb * s_len
    tt = _pick(math.gcd(n_ctx, n_lat), (256, 128, 64))
    geo = dict(tt=tt, tiles_per_seq=s_len // tt, ctx_tiles=n_ctx // tt, batch=nb, s_len=s_len, ctx=n_ctx)

    mod_rows = 8 * ((nb + 1 + 7) // 8)
    cmat = jnp.zeros((mod_rows, d), F32).at[:nb].set(c).at[nb].set(c_ctx)
    mod_all = ada_table(cmat, w_ada, b_ada).reshape(depth, mod_rows * N_MOD, 1, d)

    cos_t, sin_t = _rope_tables(n_ctx, n_lat)
    chan_c, chan_s = _chan_dft()
    h = jnp.concatenate([ctx, x], axis=1).reshape(m, d)

    w_main_all, w_gate_all, wq_all, wkv_all = _prep_weights(w_in, mla_w_uq, mla_w_ukv)
    w_branch_all, w_out_all = w_branch.astype(BF16), w_out.astype(BF16)
    delta, prev_mod, gate_k = None, None, None
    for l in range(depth):
        w_main, w_gate, wq, wkv = w_main_all[l], w_gate_all[l], wq_all[l], wkv_all[l]
        p = dict(lru_conv_w=lru_conv_w[l], lru_conv_b=lru_conv_b[l], lru_w_a=lru_w_a[l].astype(BF16),
                 lru_b_a=lru_b_a[l], lru_w_x=lru_w_x[l].astype(BF16), lru_b_x=lru_b_x[l], lru_lambda=lru_lambda[l],
                 gdn_conv_w=gdn_conv_w[l], gdn_conv_b=gdn_conv_b[l], gdn_a_log=gdn_a_log[l],
                 gdn_dt_bias=gdn_dt_bias[l], gdn_norm_g=gdn_norm_g[l], mla_q_norm_g=mla_q_norm_g[l],
                 mla_kv_norm_g=mla_kv_norm_g[l], mla_wq=wq, mla_wkv=wkv, w_gate=w_gate,
                 w_branch=w_branch_all[l], w_out=w_out_all[l])
        mod = mod_all[l]
        h, hn = norm_mod(h, norm1_g[l], geo, mod=mod, shift_k=0, scale_k=1, delta=delta, gate_k=gate_k,
                         prev_mod=prev_mod)
        zin = matmul(hn, w_main)
        hf, hb = lru_scan(zin, p, geo)
        qkvn, bg = gdn_prep(zin, p, geo)
        of, ob = gdn_scan(qkvn, bg, geo)
        fu = zin[:, COL_FU:COL_FU + FFT_WIDTH].astype(BF16).reshape(nb, s_len, FFT_WIDTH)
        yc = jnp.concatenate([fourier(fu[:, :n_ctx], chan_c, chan_s), fourier(fu[:, n_ctx:], chan_c, chan_s)],
                             axis=1).reshape(m, FFT_WIDTH)
        q, k, v = mla_prep(zin, cos_t, sin_t, p, geo)
        yd_lat, yd_ctx = mla_attention(q, k, v, geo)
        yd = jnp.concatenate([yd_ctx, yd_lat], axis=1).reshape(m, MLA_HEADS * MLA_V)
        mixed = merge(hn, zin, hf, hb, of, ob, yc, yd, p)
        h, hn2, idx, gates, rank, counts = norm_mod(
            h, norm2_g[l], geo, mod=mod, shift_k=3, scale_k=4, delta=mixed, gate_k=2, prev_mod=mod,
            router=(router_w[l], router_b[l]))
        ffn = moe_ffn(hn2, idx, gates, rank, counts[0], l, exp_w1, exp_b1, exp_w2, exp_b2)
        delta, prev_mod, gate_k = ffn, mod, 5
    (out,) = norm_mod(h, final_norm_g, geo, delta=delta, gate_k=5, prev_mod=prev_mod, emit_h=False,
                      out_dtype=F32, lat_only=True)
    return out.reshape(nb, n_lat, d)
```
